```python
import jax, jax.numpy as jnp
from jax import lax
import numpy as np

D_MODEL = 1024
BATCH = 8
SEQ = 8192
DEPTH = 2
DEC_BATCH = 8
DEC_SEQ = 2048
PAST_LEN = 128

D_MIX = D_MODEL
D_POOL = D_MIX // 2
D_LRU = D_MIX - D_POOL
POOL_WINDOWS = (2, 4, 8, 16)
N_POOL_GROUPS = len(POOL_WINDOWS)
POOL_GROUP_W = D_POOL // N_POOL_GROUPS
N_LRU_HEADS = 4
LRU_BLOCK = D_LRU // N_LRU_HEADS
LRU_C = 8.0
CONV_W = 4
CONV_LEFT = 1
D_IN = D_POOL + 2 * D_LRU
N_EXPERTS = 32
TOP_K = 4
D_FF = D_MODEL
SWIGLU_LIMIT = 7.0
SWIGLU_ALPHA = 1.702
EPS = 1e-6

kernel_name = "hybrid_pool_rglru_moe_encoder"


def rmsnorm(x, g):
    xf = x.astype(jnp.float32)
    y = xf * lax.rsqrt(jnp.mean(xf * xf, axis=-1, keepdims=True) + EPS) * g.astype(jnp.float32)
    return y.astype(x.dtype)


def pool_mixer(u, w, scale):
    S = u.shape[1]
    uf = u.astype(jnp.float32)
    cs = jnp.pad(jnp.cumsum(uf, axis=1), ((0, 0), (1, 0), (0, 0)))
    t = jnp.arange(S)
    outs = []
    for g, win in enumerate(POOL_WINDOWS):
        lo = jnp.clip(t - win // 2, 0, S)
        hi = jnp.clip(t + win // 2, 0, S)
        seg = cs[:, :, g * POOL_GROUP_W:(g + 1) * POOL_GROUP_W]
        cnt = (hi - lo).astype(jnp.float32)[None, :, None]
        mean = (jnp.take(seg, hi, axis=1) - jnp.take(seg, lo, axis=1)) / cnt
        outs.append(mean - uf[..., g * POOL_GROUP_W:(g + 1) * POOL_GROUP_W])
    p = jnp.stack(outs, axis=2)
    y = jnp.einsum('bsgi,gij->bsgj', p, w.astype(jnp.float32))
    y = y.reshape(u.shape) * scale.astype(jnp.float32)
    return y.astype(u.dtype)


def depthwise_conv(u, w, b):
    S = u.shape[1]
    up = jnp.pad(u, ((0, 0), (CONV_LEFT, CONV_W - 1 - CONV_LEFT), (0, 0)))
    y = b
    for k in range(CONV_W):
        y = y + up[:, k:k + S] * w[k]
    return y


def _lin_combine(e, l):
    a_e, b_e = e
    a_l, b_l = l
    return (a_l * a_e, a_l * b_e + b_l)


def rglru_direction(xc, wa, ba, wx, bx, lam, reverse):
    B, S, _ = xc.shape
    xb = xc.astype(jnp.float32).reshape(B, S, N_LRU_HEADS, LRU_BLOCK)
    r = jax.nn.sigmoid(jnp.einsum('bshi,hij->bshj', xb, wa.astype(jnp.float32)) + ba.astype(jnp.float32))
    i = jax.nn.sigmoid(jnp.einsum('bshi,hij->bshj', xb, wx.astype(jnp.float32)) + bx.astype(jnp.float32))
    log_a = -LRU_C * r * jax.nn.softplus(-lam.astype(jnp.float32).reshape(N_LRU_HEADS, LRU_BLOCK))
    a = jnp.exp(log_a)
    beta = jnp.sqrt(-jnp.expm1(2.0 * log_a))
    bterm = beta * i * xb
    _, h = lax.associative_scan(_lin_combine, (a, bterm), axis=1, reverse=reverse)
    return h.reshape(B, S, D_LRU)


def mixer_layer(x, norm_mix, w_in, pool_w, pool_scale, conv_w, conv_b,
                lru_wa, lru_ba, lru_wx, lru_bx, lru_lambda, out_norm_pool, out_norm_lru, w_out):
    h = rmsnorm(x, norm_mix)
    z = jnp.einsum('bsd,de->bse', h, w_in)
    u_pool = z[..., :D_POOL]
    u_lru = z[..., D_POOL:D_POOL + D_LRU]
    u_gate = z[..., D_POOL + D_LRU:]
    y_pool = pool_mixer(u_pool, pool_w, pool_scale)
    xc = depthwise_conv(u_lru, conv_w, conv_b)
    h_f = rglru_direction(xc, lru_wa[0], lru_ba[0], lru_wx[0], lru_bx[0], lru_lambda[0], False)
    h_b = rglru_direction(xc, lru_wa[1], lru_ba[1], lru_wx[1], lru_bx[1], lru_lambda[1], True)
    y_lru = ((h_f + h_b) * jax.nn.gelu(u_gate.astype(jnp.float32), approximate=True)).astype(x.dtype)
    y = jnp.concatenate([rmsnorm(y_pool, out_norm_pool), rmsnorm(y_lru, out_norm_lru)], axis=-1)
    return x + jnp.einsum('bse,ed->bsd', y, w_out)


def moe_layer(x, norm_ffn, router_w, router_b, w_up, b_up, w_down, b_down):
    B, S, D = x.shape
    hf = rmsnorm(x, norm_ffn).reshape(B * S, D)
    logits = hf.astype(jnp.float32) @ router_w.astype(jnp.float32) + router_b.astype(jnp.float32)
    vals, idx = lax.top_k(logits, TOP_K)
    wts = jax.nn.softmax(vals, axis=-1)
    gate = jnp.sum(jax.nn.one_hot(idx, N_EXPERTS, dtype=jnp.float32) * wts[..., None], axis=1)

    def expert_step(acc, ew):
        wu, bu, wd, bd, g = ew
        hh = (hf @ wu + bu).astype(jnp.float32)
        x_glu = jnp.minimum(hh[:, :D_FF], SWIGLU_LIMIT)
        x_lin = jnp.clip(hh[:, D_FF:], -SWIGLU_LIMIT, SWIGLU_LIMIT)
        act = (x_glu * jax.nn.sigmoid(SWIGLU_ALPHA * x_glu) * (x_lin + 1.0)).astype(hf.dtype)
        out = (act @ wd + bd).astype(jnp.float32)
        return acc + g[:, None] * out, None

    acc0 = jnp.zeros((B * S, D), jnp.float32)
    acc, _ = lax.scan(expert_step, acc0, (w_up, b_up, w_down, b_down, gate.T))
    return x + acc.reshape(B, S, D).astype(x.dtype)


def trunk(x, norm_mix, w_in, pool_w, pool_scale, conv_w, conv_b, lru_wa, lru_ba, lru_wx, lru_bx,
          lru_lambda, out_norm_pool, out_norm_lru, w_out, norm_ffn, router_w, router_b,
          w_up, b_up, w_down, b_down, norm_final):
    for l in range(DEPTH):
        x = mixer_layer(x, norm_mix[l], w_in[l], pool_w[l], pool_scale[l], conv_w[l], conv_b[l],
                        lru_wa[l], lru_ba[l], lru_wx[l], lru_bx[l], lru_lambda[l],
                        out_norm_pool[l], out_norm_lru[l], w_out[l])
        x = moe_layer(x, norm_ffn[l], router_w[l], router_b[l], w_up[l], b_up[l], w_down[l], b_down[l])
    return rmsnorm(x, norm_final)


def setup_inputs(seed: int = 0) -> dict:
    key = jax.random.key(seed)
    ks = jax.random.split(key, 26)

    def nrm(k, shape, scale):
        return jax.random.normal(k, shape, jnp.float32) * scale

    u = jax.random.uniform(ks[13], (DEPTH, 2, D_LRU), jnp.float32, minval=0.9, maxval=0.999)
    a = u ** (1.0 / LRU_C)
    lru_lambda = jnp.log(a) - jnp.log1p(-a)
    return {
        "x_prompt": nrm(ks[0], (BATCH, SEQ, D_MODEL), 1.0),
        "x_sample": nrm(ks[1], (DEC_BATCH, DEC_SEQ, D_MODEL), 1.0),
        "norm_mix": 1.0 + nrm(ks[2], (DEPTH, D_MODEL), 0.02),
        "w_in": nrm(ks[3], (DEPTH, D_MODEL, D_IN), D_MODEL ** -0.5),
        "pool_w": nrm(ks[4], (DEPTH, N_POOL_GROUPS, POOL_GROUP_W, POOL_GROUP_W), POOL_GROUP_W ** -0.5),
        "pool_scale": 1.0 + nrm(ks[5], (DEPTH, D_POOL), 0.1),
        "conv_w": nrm(ks[6], (DEPTH, CONV_W, D_LRU), CONV_W ** -0.5),
        "conv_b": nrm(ks[7], (DEPTH, D_LRU), 0.01),
        "lru_wa": nrm(ks[8], (DEPTH, 2, N_LRU_HEADS, LRU_BLOCK, LRU_BLOCK), LRU_BLOCK ** -0.5),
        "lru_ba": nrm(ks[9], (DEPTH, 2, N_LRU_HEADS, LRU_BLOCK), 0.01),
        "lru_wx": nrm(ks[10], (DEPTH, 2, N_LRU_HEADS, LRU_BLOCK, LRU_BLOCK), LRU_BLOCK ** -0.5),
        "lru_bx": nrm(ks[11], (DEPTH, 2, N_LRU_HEADS, LRU_BLOCK), 0.01),
        "lru_lambda": lru_lambda,
        "out_norm_pool": 1.0 + nrm(ks[14], (DEPTH, D_POOL), 0.02),
        "out_norm_lru": 1.0 + nrm(ks[15], (DEPTH, D_LRU), 0.02),
        "w_out": nrm(ks[16], (DEPTH, D_MIX, D_MODEL), D_MIX ** -0.5),
        "norm_ffn": 1.0 + nrm(ks[17], (DEPTH, D_MODEL), 0.02),
        "router_w": nrm(ks[18], (DEPTH, D_MODEL, N_EXPERTS), D_MODEL ** -0.5),
        "router_b": nrm(ks[19], (DEPTH, N_EXPERTS), 0.01),
        "w_up": nrm(ks[20], (DEPTH, N_EXPERTS, D_MODEL, 2 * D_FF), D_MODEL ** -0.5),
        "b_up": nrm(ks[21], (DEPTH, N_EXPERTS, 2 * D_FF), 0.01),
        "w_down": nrm(ks[22], (DEPTH, N_EXPERTS, D_FF, D_MODEL), D_FF ** -0.5),
        "b_down": nrm(ks[23], (DEPTH, N_EXPERTS, D_MODEL), 0.01),
        "norm_final": 1.0 + nrm(ks[24], (D_MODEL,), 0.02),
    }


def reference(x_prompt, x_sample, norm_mix, w_in, pool_w, pool_scale, conv_w, conv_b, lru_wa, lru_ba,
              lru_wx, lru_bx, lru_lambda, out_norm_pool, out_norm_lru, w_out, norm_ffn, router_w,
              router_b, w_up, b_up, w_down, b_down, norm_final):
    y_prompt = trunk(x_prompt, norm_mix, w_in, pool_w, pool_scale, conv_w, conv_b, lru_wa, lru_ba,
                     lru_wx, lru_bx, lru_lambda, out_norm_pool, out_norm_lru, w_out, norm_ffn,
                     router_w, router_b, w_up, b_up, w_down, b_down, norm_final)
    y_sample = trunk(x_sample, norm_mix, w_in, pool_w, pool_scale, conv_w, conv_b, lru_wa, lru_ba,
                     lru_wx, lru_bx, lru_lambda, out_norm_pool, out_norm_lru, w_out, norm_ffn,
                     router_w, router_b, w_up, b_up, w_down, b_down, norm_final)
    return (y_prompt, y_sample)
```

```python
import functools

import numpy as np
import jax
import jax.numpy as jnp
from jax import lax
from jax.experimental import pallas as pl
from jax.experimental.pallas import tpu as pltpu

D_MODEL = 1024
D_POOL = 512
D_LRU = 512
D_IN = D_POOL + 2 * D_LRU
POOL_WINDOWS = (2, 4, 8, 16)
POOL_GROUP_W = D_POOL // len(POOL_WINDOWS)
N_LRU_HEADS = 4
LRU_BLOCK = D_LRU // N_LRU_HEADS
LRU_C = 8.0
CONV_W = 4
CONV_LEFT = 1
N_EXPERTS = 32
TOP_K = 4
D_FF = D_MODEL
SWIGLU_LIMIT = 7.0
SWIGLU_ALPHA = 1.702
EPS = 1e-6

LANES = 128
SUBLANES = 8
HALO = 16
VMEM_LIMIT = 48 * 1024 * 1024

F32 = jnp.float32
BF16 = jnp.bfloat16


def _rms(x, g):
    return x * lax.rsqrt(jnp.mean(x * x, axis=-1, keepdims=True) + EPS) * g


def _shift_rows(v, s):
    n = v.shape[0]
    return pltpu.roll(v, s % n, axis=0)


def _lru_gates(xc, wg_ref, ba_ref, bx_ref, sp_ref, a_scr, b_scr):
    for h in range(N_LRU_HEADS):
        cs = slice(h * LRU_BLOCK, (h + 1) * LRU_BLOCK)
        xh = xc[:, cs]
        gr = jnp.dot(xh.astype(BF16), wg_ref[h], preferred_element_type=F32)
        r = jax.nn.sigmoid(gr[:, :LRU_BLOCK] + ba_ref[:, cs])
        i = jax.nn.sigmoid(gr[:, LRU_BLOCK:] + bx_ref[:, cs])
        log_a = -LRU_C * r * sp_ref[:, cs]
        a = jnp.exp(log_a)
        beta = jnp.sqrt(1.0 - a * a)
        a_scr[:, cs] = a
        b_scr[:, cs] = beta * i * xh


def _scan_rows(a_ref, b_ref, h_ref, carry0, n_rows, reverse):
    row = lax.broadcasted_iota(jnp.int32, (SUBLANES, D_LRU), 0)
    n_grp = n_rows // SUBLANES

    def body(k, carry):
        grp = (n_grp - 1 - k) if reverse else k
        r0 = pl.multiple_of(grp * SUBLANES, SUBLANES)
        a = a_ref[pl.ds(r0, SUBLANES), :]
        b = b_ref[pl.ds(r0, SUBLANES), :]
        for s in (1, 2, 4):
            if reverse:
                ok = row < SUBLANES - s
                a_sh = jnp.where(ok, pltpu.roll(a, SUBLANES - s, axis=0), 1.0)
                b_sh = jnp.where(ok, pltpu.roll(b, SUBLANES - s, axis=0), 0.0)
            else:
                ok = row >= s
                a_sh = jnp.where(ok, pltpu.roll(a, s, axis=0), 1.0)
                b_sh = jnp.where(ok, pltpu.roll(b, s, axis=0), 0.0)
            b = a * b_sh + b
            a = a * a_sh
        h = a * carry + b
        h_ref[pl.ds(r0, SUBLANES), :] = h
        return h[0:1, :] if reverse else h[SUBLANES - 1:SUBLANES, :]

    return lax.fori_loop(0, n_grp, body, carry0)


def _mix_a_kernel(pos_ref, len_ref,
                  x_ref, xp_ref, xn_ref, nm_ref, win_ref, pw_ref, ps_ref, cw_ref, cb_ref,
                  wg_ref, ba_ref, bx_ref, sp_ref, onp_ref,
                  ypn_ref, hf_ref, gate_ref, xc_ref,
                  h_scr, a_scr, b_scr, carry_ref, *, ts):
    g = pl.program_id(0)
    pos0 = pos_ref[g]
    slen = len_ref[g]
    keep_prev = jnp.where(pos0 == 0, 0.0, 1.0).astype(F32)
    keep_next = jnp.where(pos0 + ts == slen, 0.0, 1.0).astype(F32)
    nm = nm_ref[...]

    h_scr[0:HALO, :] = (_rms(xp_ref[...], nm) * keep_prev).astype(BF16)
    h_scr[HALO:HALO + ts, :] = _rms(x_ref[...], nm).astype(BF16)
    h_scr[HALO + ts:, :] = (_rms(xn_ref[...], nm) * keep_next).astype(BF16)
    z = jnp.dot(h_scr[...], win_ref[...], preferred_element_type=F32)

    trow = pos0 + lax.broadcasted_iota(jnp.int32, (ts, POOL_GROUP_W), 0)
    ys = []
    for gi, win in enumerate(POOL_WINDOWS):
        u = z[:, gi * POOL_GROUP_W:(gi + 1) * POOL_GROUP_W]
        acc = u + _shift_rows(u, 1)
        half = 1
        while 2 * half < win:
            acc = _shift_rows(acc, half) + _shift_rows(acc, -half)
            half *= 2
        half = win // 2
        cnt = (jnp.minimum(trow + half, slen) - jnp.maximum(trow - half, 0)).astype(F32)
        p = acc[HALO:HALO + ts] / cnt - u[HALO:HALO + ts]
        ys.append(jnp.dot(p.astype(BF16), pw_ref[gi], preferred_element_type=F32))
    y_pool = jnp.concatenate(ys, axis=-1) * ps_ref[...]
    ypn_ref[...] = _rms(y_pool, onp_ref[...]).astype(BF16)

    ul = z[:, D_POOL:D_POOL + D_LRU]
    xc_ext = cb_ref[...] + _shift_rows(ul, CONV_LEFT) * cw_ref[0:1, :]
    xc_ext = xc_ext + ul * cw_ref[1:2, :]
    xc_ext = xc_ext + _shift_rows(ul, -1) * cw_ref[2:3, :]
    xc_ext = xc_ext + _shift_rows(ul, -2) * cw_ref[3:4, :]
    xc = xc_ext[HALO:HALO + ts]
    xc_ref[...] = xc

    gate_ref[...] = jax.nn.gelu(z[HALO:HALO + ts, D_POOL + D_LRU:], approximate=True)

    _lru_gates(xc, wg_ref, ba_ref, bx_ref, sp_ref, a_scr, b_scr)
    carry0 = jnp.where(pos0 == 0, 0.0, carry_ref[...])
    carry_ref[...] = _scan_rows(a_scr, b_scr, hf_ref, carry0, ts, reverse=False)


def _mix_a(x, tile_pos, tile_len, p, *, ts):
    n = x.shape[0]
    g_tiles = n // ts
    hb = ts // HALO
    n_hblk = n // HALO
    row = lambda g, *_: (g, 0)
    const2 = lambda g, *_: (0, 0)
    const3 = lambda g, *_: (0, 0, 0)
    grid_spec = pltpu.PrefetchScalarGridSpec(
        num_scalar_prefetch=2,
        grid=(g_tiles,),
        in_specs=[
            pl.BlockSpec((ts, D_MODEL), row),
            pl.BlockSpec((HALO, D_MODEL), lambda g, *_: (jnp.maximum(g * hb - 1, 0), 0)),
            pl.BlockSpec((HALO, D_MODEL), lambda g, *_: (jnp.minimum((g + 1) * hb, n_hblk - 1), 0)),
            pl.BlockSpec((1, D_MODEL), const2),
            pl.BlockSpec((D_MODEL, D_IN), const2),
            pl.BlockSpec((len(POOL_WINDOWS), POOL_GROUP_W, POOL_GROUP_W), const3),
            pl.BlockSpec((1, D_POOL), const2),
            pl.BlockSpec((CONV_W, D_LRU), const2),
            pl.BlockSpec((1, D_LRU), const2),
            pl.BlockSpec((N_LRU_HEADS, LRU_BLOCK, 2 * LRU_BLOCK), const3),
            pl.BlockSpec((1, D_LRU), const2),
            pl.BlockSpec((1, D_LRU), const2),
            pl.BlockSpec((1, D_LRU), const2),
            pl.BlockSpec((1, D_POOL), const2),
        ],
        out_specs=[
            pl.BlockSpec((ts, D_POOL), row),
            pl.BlockSpec((ts, D_LRU), row),
            pl.BlockSpec((ts, D_LRU), row),
            pl.BlockSpec((ts, D_LRU), row),
        ],
        scratch_shapes=[
            pltpu.VMEM((ts + 2 * HALO, D_MODEL), BF16),
            pltpu.VMEM((ts, D_LRU), F32),
            pltpu.VMEM((ts, D_LRU), F32),
            pltpu.VMEM((1, D_LRU), F32),
        ],
    )
    return pl.pallas_call(
        functools.partial(_mix_a_kernel, ts=ts),
        grid_spec=grid_spec,
        out_shape=[
            jax.ShapeDtypeStruct((n, D_POOL), BF16),
            jax.ShapeDtypeStruct((n, D_LRU), F32),
            jax.ShapeDtypeStruct((n, D_LRU), F32),
            jax.ShapeDtypeStruct((n, D_LRU), F32),
        ],
        compiler_params=pltpu.CompilerParams(
            dimension_semantics=("arbitrary",), vmem_limit_bytes=VMEM_LIMIT),
        name="mix_a",
    )(tile_pos, tile_len, x, x, x, p["norm_mix"], p["w_in"], p["pool_w"], p["pool_scale"],
      p["conv_w"], p["conv_b"], p["wg_f"], p["ba_f"], p["bx_f"], p["sp_f"], p["out_norm_pool"])


def _mix_b_kernel(pos_ref, len_ref,
                  x_ref, xc_ref, hf_ref, gate_ref, ypn_ref,
                  wg_ref, ba_ref, bx_ref, sp_ref, onl_ref, wout_ref,
                  o_ref,
                  a_scr, b_scr, hb_scr, carry_ref, *, ts, g_tiles):
    g = g_tiles - 1 - pl.program_id(0)
    at_end = pos_ref[g] + ts == len_ref[g]
    _lru_gates(xc_ref[...], wg_ref, ba_ref, bx_ref, sp_ref, a_scr, b_scr)
    carry0 = jnp.where(at_end, 0.0, carry_ref[...])
    carry_ref[...] = _scan_rows(a_scr, b_scr, hb_scr, carry0, ts, reverse=True)
    y_lru = (hf_ref[...] + hb_scr[...]) * gate_ref[...]
    yln = _rms(y_lru, onl_ref[...]).astype(BF16)
    out = jnp.dot(ypn_ref[...], wout_ref[0:D_POOL, :], preferred_element_type=F32)
    out = out + jnp.dot(yln, wout_ref[D_POOL:, :], preferred_element_type=F32)
    o_ref[...] = x_ref[...] + out


def _mix_b(x, xc, hf, gate, ypn, tile_pos, tile_len, p, *, ts):
    n = x.shape[0]
    g_tiles = n // ts
    row = lambda g, *_: (g_tiles - 1 - g, 0)
    const2 = lambda g, *_: (0, 0)
    const3 = lambda g, *_: (0, 0, 0)
    grid_spec = pltpu.PrefetchScalarGridSpec(
        num_scalar_prefetch=2,
        grid=(g_tiles,),
        in_specs=[
            pl.BlockSpec((ts, D_MODEL), row),
            pl.BlockSpec((ts, D_LRU), row),
            pl.BlockSpec((ts, D_LRU), row),
            pl.BlockSpec((ts, D_LRU), row),
            pl.BlockSpec((ts, D_POOL), row),
            pl.BlockSpec((N_LRU_HEADS, LRU_BLOCK, 2 * LRU_BLOCK), const3),
            pl.BlockSpec((1, D_LRU), const2),
            pl.BlockSpec((1, D_LRU), const2),
            pl.BlockSpec((1, D_LRU), const2),
            pl.BlockSpec((1, D_LRU), const2),
            pl.BlockSpec((D_POOL + D_LRU, D_MODEL), const2),
        ],
        out_specs=pl.BlockSpec((ts, D_MODEL), row),
        scratch_shapes=[
            pltpu.VMEM((ts, D_LRU), F32),
            pltpu.VMEM((ts, D_LRU), F32),
            pltpu.VMEM((ts, D_LRU), F32),
            pltpu.VMEM((1, D_LRU), F32),
        ],
    )
    return pl.pallas_call(
        functools.partial(_mix_b_kernel, ts=ts, g_tiles=g_tiles),
        grid_spec=grid_spec,
        out_shape=jax.ShapeDtypeStruct((n, D_MODEL), F32),
        compiler_params=pltpu.CompilerParams(
            dimension_semantics=("arbitrary",), vmem_limit_bytes=VMEM_LIMIT),
        name="mix_b",
    )(tile_pos, tile_len, x, xc, hf, gate, ypn, p["wg_b"], p["ba_b"], p["bx_b"], p["sp_b"],
      p["out_norm_lru"], p["w_out"])


def _router_kernel(x_ref, nf_ref, rw_ref, rb_ref,
                   hf_ref, meta_ref, wts_ref, cnt_ref,
                   carry_ref, *, tr):
    @pl.when(pl.program_id(0) == 0)
    def _():
        carry_ref[...] = jnp.zeros_like(carry_ref)

    hf = _rms(x_ref[...], nf_ref[...])
    hf_ref[...] = hf.reshape(tr, 1, D_MODEL)
    logits = jnp.dot(hf, rw_ref[...], preferred_element_type=F32,
                     precision=lax.Precision.HIGHEST) + rb_ref[...]
    lane = lax.broadcasted_iota(jnp.int32, (tr, LANES), 1).astype(F32)

    vals, idxs = [], []
    l = logits
    for _ in range(TOP_K):
        m = jnp.max(l, axis=-1, keepdims=True)
        ik = jnp.min(jnp.where(l == m, lane, float(LANES)), axis=-1, keepdims=True)
        vals.append(m)
        idxs.append(ik)
        l = jnp.where(lane == ik, -jnp.inf, l)
    es = [jnp.exp(v - vals[0]) for v in vals]
    den = es[0] + es[1] + es[2] + es[3]

    hits = [lane == ik for ik in idxs]
    chosen = jnp.zeros((tr, LANES), F32)
    for hit in hits:
        chosen = chosen + jnp.where(hit, 1.0, 0.0)
    tri = jnp.where(lax.broadcasted_iota(jnp.int32, (tr, tr), 1) < lax.broadcasted_iota(jnp.int32, (tr, tr), 0),
                    1.0, 0.0).astype(BF16)
    before = jnp.dot(tri, chosen.astype(BF16), preferred_element_type=F32) + carry_ref[...]
    total = carry_ref[...] + jnp.sum(chosen, axis=0, keepdims=True)
    carry_ref[...] = total
    cnt_ref[...] = total

    meta = jnp.zeros((tr, LANES), F32)
    wts = jnp.zeros((tr, LANES), F32)
    for k in range(TOP_K):
        pos_k = jnp.sum(jnp.where(hits[k], before, 0.0), axis=-1, keepdims=True)
        meta = jnp.where(lane == float(k), idxs[k], meta)
        meta = jnp.where(lane == float(TOP_K + k), pos_k, meta)
        wts = jnp.where(lane == float(k), es[k] / den, wts)
    meta_ref[...] = meta.astype(jnp.int32)
    wts_ref[...] = wts


def _router(x, p, *, tr):
    n = x.shape[0]
    row = lambda j: (j, 0)
    const2 = lambda j: (0, 0)
    return pl.pallas_call(
        functools.partial(_router_kernel, tr=tr),
        grid=(n // tr,),
        in_specs=[
            pl.BlockSpec((tr, D_MODEL), row),
            pl.BlockSpec((1, D_MODEL), const2),
            pl.BlockSpec((D_MODEL, LANES), const2),
            pl.BlockSpec((1, LANES), const2),
        ],
        out_specs=[
            pl.BlockSpec((tr, 1, D_MODEL), lambda j: (j, 0, 0)),
            pl.BlockSpec((tr, LANES), row),
            pl.BlockSpec((tr, LANES), row),
            pl.BlockSpec((1, LANES), const2),
        ],
        out_shape=[
            jax.ShapeDtypeStruct((n, 1, D_MODEL), F32),
            jax.ShapeDtypeStruct((n, LANES), jnp.int32),
            jax.ShapeDtypeStruct((n, LANES), F32),
            jax.ShapeDtypeStruct((1, LANES), F32),
        ],
        scratch_shapes=[pltpu.VMEM((1, LANES), F32)],
        compiler_params=pltpu.CompilerParams(
            dimension_semantics=("arbitrary",), vmem_limit_bytes=VMEM_LIMIT),
        name="router",
    )(x, p["norm_ffn"], p["router_w"], p["router_b"])


def _moe_kernel(te_ref, used_ref, nv_ref,
                tokc_ref, tokn_ref, dst_ref, hf_hbm, wup_ref, bup_ref, wdn_ref, bdn_ref,
                ys_hbm,
                xbuf, obuf, x2d, dump, gsem, ssem, *, tm):
    i = pl.program_id(0)
    used = used_ref[0]
    n_valid = nv_ref[i]
    slot = lax.rem(i, 2)

    def gather_rows(tok_ref, s):
        base = s * tm

        def body(r, c):
            t = tok_ref[0, 0, r]
            pltpu.make_async_copy(hf_hbm.at[pl.ds(t, 1)], xbuf.at[pl.ds(base + r, 1)], gsem.at[s]).start()
            return c

        lax.fori_loop(0, tm, body, 0)

    def gather_wait(s):
        pltpu.make_async_copy(hf_hbm.at[pl.ds(0, tm)], xbuf.at[pl.ds(s * tm, tm)], gsem.at[s]).wait()

    def scatter_rows(s):
        base = s * tm

        def body(r, c):
            d = dst_ref[0, 0, r]
            pltpu.make_async_copy(obuf.at[pl.ds(base + r, 1)], ys_hbm.at[pl.ds(d, 1)], ssem.at[s]).start()
            return c

        def pad_body(r, c):
            pltpu.make_async_copy(obuf.at[pl.ds(base + r, 1)], dump.at[pl.ds(r, 1)], ssem.at[s]).start()
            return c

        lax.fori_loop(0, n_valid, body, 0)
        lax.fori_loop(n_valid, tm, pad_body, 0)

    def scatter_wait(s):
        pltpu.make_async_copy(obuf.at[pl.ds(s * tm, tm)], ys_hbm.at[pl.ds(0, tm)], ssem.at[s]).wait()

    @pl.when(i == 0)
    def _():
        gather_rows(tokc_ref, 0)

    @pl.when(i + 1 < used)
    def _():
        gather_rows(tokn_ref, 1 - slot)

    @pl.when(i < used)
    def _():
        gather_wait(slot)
        x2d[...] = xbuf[pl.ds(pl.multiple_of(slot * tm, tm), tm)].reshape(tm, D_MODEL)
        hh = jnp.dot(x2d[...].astype(BF16), wup_ref[...], preferred_element_type=F32) + bup_ref[...]
        x_glu = jnp.minimum(hh[:, :D_FF], SWIGLU_LIMIT)
        x_lin = jnp.clip(hh[:, D_FF:], -SWIGLU_LIMIT, SWIGLU_LIMIT)
        act = (x_glu * jax.nn.sigmoid(SWIGLU_ALPHA * x_glu) * (x_lin + 1.0)).astype(BF16)
        out = jnp.dot(act, wdn_ref[...], preferred_element_type=F32) + bdn_ref[...]
        obuf[pl.ds(pl.multiple_of(slot * tm, tm), tm)] = out.reshape(tm, 1, D_MODEL)

        @pl.when(i >= 1)
        def _():
            scatter_wait(1 - slot)

        scatter_rows(slot)

        @pl.when(i == used - 1)
        def _():
            scatter_wait(slot)


def _moe(hf3, tile_expert, used, n_valid, tok3, dst3, p, *, tm, n_out_rows):
    nt = tok3.shape[0]
    smem_blk = lambda fn: pl.BlockSpec((1, 1, tm), fn, memory_space=pltpu.SMEM)
    grid_spec = pltpu.PrefetchScalarGridSpec(
        num_scalar_prefetch=3,
        grid=(nt,),
        in_specs=[
            smem_blk(lambda i, te, u, nv: (i, 0, 0)),
            smem_blk(lambda i, te, u, nv: (jnp.minimum(i + 1, nt - 1), 0, 0)),
            smem_blk(lambda i, te, u, nv: (i, 0, 0)),
            pl.BlockSpec(memory_space=pl.ANY),
            pl.BlockSpec((None, D_MODEL, 2 * D_FF), lambda i, te, u, nv: (te[i], 0, 0)),
            pl.BlockSpec((None, 1, 2 * D_FF), lambda i, te, u, nv: (te[i], 0, 0)),
            pl.BlockSpec((None, D_FF, D_MODEL), lambda i, te, u, nv: (te[i], 0, 0)),
            pl.BlockSpec((None, 1, D_MODEL), lambda i, te, u, nv: (te[i], 0, 0)),
        ],
        out_specs=pl.BlockSpec(memory_space=pl.ANY),
        scratch_shapes=[
            pltpu.VMEM((2 * tm, 1, D_MODEL), F32),
            pltpu.VMEM((2 * tm, 1, D_MODEL), F32),
            pltpu.VMEM((tm, D_MODEL), F32),
            pltpu.VMEM((tm, 1, D_MODEL), F32),
            pltpu.SemaphoreType.DMA((2,)),
            pltpu.SemaphoreType.DMA((2,)),
        ],
    )
    return pl.pallas_call(
        functools.partial(_moe_kernel, tm=tm),
        grid_spec=grid_spec,
        out_shape=jax.ShapeDtypeStruct((n_out_rows, 1, D_MODEL), F32),
        compiler_params=pltpu.CompilerParams(
            dimension_semantics=("arbitrary",), vmem_limit_bytes=VMEM_LIMIT),
        name="moe",
    )(tile_expert, used, n_valid, tok3, tok3, dst3, hf3, p["w_up"], p["b_up"], p["w_down"], p["b_down"])


def _combine_kernel(x_ref, y0_ref, y1_ref, y2_ref, y3_ref, wts_ref, nf_ref, o_ref, y2d, *, tc, final):
    for k, y_ref in enumerate((y0_ref, y1_ref, y2_ref, y3_ref)):
        y2d[k] = y_ref[...].reshape(tc, D_MODEL)
    w = wts_ref[...]
    acc = w[:, 0:1] * y2d[0]
    for k in range(1, TOP_K):
        acc = acc + w[:, k:k + 1] * y2d[k]
    x = x_ref[...] + acc
    o_ref[...] = _rms(x, nf_ref[...]) if final else x


def _combine(x, ys3, wts, norm_final, *, tc, final, row0, n_rows):
    n = x.shape[0]
    b0 = row0 // tc
    kb = n // tc
    row = lambda j: (b0 + j, 0)
    y_spec = lambda k: pl.BlockSpec((tc, 1, D_MODEL), lambda j: (k * kb + b0 + j, 0, 0))
    return pl.pallas_call(
        functools.partial(_combine_kernel, tc=tc, final=final),
        grid=(n_rows // tc,),
        in_specs=[
            pl.BlockSpec((tc, D_MODEL), row),
            y_spec(0), y_spec(1), y_spec(2), y_spec(3),
            pl.BlockSpec((tc, LANES), row),
            pl.BlockSpec((1, D_MODEL), lambda j: (0, 0)),
        ],
        out_specs=pl.BlockSpec((tc, D_MODEL), lambda j: (j, 0)),
        out_shape=jax.ShapeDtypeStruct((n_rows, D_MODEL), F32),
        scratch_shapes=[pltpu.VMEM((TOP_K, tc, D_MODEL), F32)],
        compiler_params=pltpu.CompilerParams(
            dimension_semantics=("arbitrary",), vmem_limit_bytes=VMEM_LIMIT),
        name="combine_final" if final else "combine",
    )(x, ys3, ys3, ys3, ys3, wts, norm_final)


def _seq_tables(groups, ts):
    pos, length = [], []
    for n_seq, seq_len in groups:
        assert seq_len % ts == 0
        for _ in range(n_seq):
            for t in range(seq_len // ts):
                pos.append(t * ts)
                length.append(seq_len)
    return np.asarray(pos, np.int32), np.asarray(length, np.int32)


def _routing_plan(meta, counts, *, n, tm, nt):
    idx = meta[:, 0:TOP_K]
    pos = meta[:, TOP_K:2 * TOP_K]
    cnt = counts[0, :N_EXPERTS].astype(jnp.int32)
    tiles_e = (cnt + tm - 1) // tm
    tile_end = jnp.cumsum(tiles_e)
    used = tile_end[-1:]
    base = (tile_end - tiles_e) * tm
    slot = (jnp.take(base, idx) + pos).T.reshape(-1)
    tile_ids = jnp.arange(nt, dtype=jnp.int32)
    tile_expert = jnp.minimum(jnp.searchsorted(tile_end, tile_ids, side="right"), N_EXPERTS - 1).astype(jnp.int32)
    rows_left = jnp.take(cnt, tile_expert) - (tile_ids - jnp.take(tile_end - tiles_e, tile_expert)) * tm
    n_valid = jnp.clip(rows_left, 0, tm).astype(jnp.int32)
    dst = jnp.zeros((nt * tm,), jnp.int32).at[slot].set(jnp.arange(TOP_K * n, dtype=jnp.int32))
    tok = dst % n
    return tile_expert, used.astype(jnp.int32), n_valid, tok.reshape(nt, 1, tm), dst.reshape(nt, 1, tm)


def _layer_params(l, norm_mix, w_in, pool_w, pool_scale, conv_w, conv_b, lru_wa, lru_ba, lru_wx, lru_bx,
                  lru_lambda, out_norm_pool, out_norm_lru, w_out, norm_ffn, router_w, router_b,
                  w_up, b_up, w_down, b_down):
    def gates(d):
        return dict(
            wg=jnp.concatenate([lru_wa[l, d], lru_wx[l, d]], axis=-1).astype(BF16),
            ba=lru_ba[l, d].reshape(1, D_LRU),
            bx=lru_bx[l, d].reshape(1, D_LRU),
            sp=jax.nn.softplus(-lru_lambda[l, d]).reshape(1, D_LRU),
        )
    gf, gb = gates(0), gates(1)
    pad = LANES - N_EXPERTS
    return dict(
        norm_mix=norm_mix[l].reshape(1, D_MODEL),
        w_in=w_in[l].astype(BF16),
        pool_w=pool_w[l].astype(BF16),
        pool_scale=pool_scale[l].reshape(1, D_POOL),
        conv_w=conv_w[l],
        conv_b=conv_b[l].reshape(1, D_LRU),
        wg_f=gf["wg"], ba_f=gf["ba"], bx_f=gf["bx"], sp_f=gf["sp"],
        wg_b=gb["wg"], ba_b=gb["ba"], bx_b=gb["bx"], sp_b=gb["sp"],
        out_norm_pool=out_norm_pool[l].reshape(1, D_POOL),
        out_norm_lru=out_norm_lru[l].reshape(1, D_LRU),
        w_out=w_out[l].astype(BF16),
        norm_ffn=norm_ffn[l].reshape(1, D_MODEL),
        router_w=jnp.pad(router_w[l], ((0, 0), (0, pad))),
        router_b=jnp.pad(router_b[l].reshape(1, N_EXPERTS), ((0, 0), (0, pad)), constant_values=-jnp.inf),
        w_up=w_up[l].astype(BF16),
        b_up=b_up[l].reshape(N_EXPERTS, 1, 2 * D_FF),
        w_down=w_down[l].astype(BF16),
        b_down=b_down[l].reshape(N_EXPERTS, 1, D_MODEL),
    )


def _trunk(xs, weights, norm_final, *, ts, tr, tm, tc):
    groups = [(x.shape[0], x.shape[1]) for x in xs]
    sizes = [b * s for b, s in groups]
    n = sum(sizes)
    depth = weights[0].shape[0]
    x = jnp.concatenate([xi.reshape(-1, D_MODEL) for xi in xs], axis=0)
    tile_pos, tile_len = _seq_tables(groups, ts)
    tile_pos, tile_len = jnp.asarray(tile_pos), jnp.asarray(tile_len)
    nt = (TOP_K * n) // tm + N_EXPERTS
    n_out_rows = TOP_K * n
    nf = norm_final.reshape(1, D_MODEL)
    outs = None
    for l in range(depth):
        p = _layer_params(l, *weights)
        ypn, hf, gate, xc = _mix_a(x, tile_pos, tile_len, p, ts=ts)
        x = _mix_b(x, xc, hf, gate, ypn, tile_pos, tile_len, p, ts=ts)
        hf3, meta, wts, counts = _router(x, p, tr=tr)
        tile_expert, used, n_valid, tok3, dst3 = _routing_plan(meta, counts, n=n, tm=tm, nt=nt)
        ys3 = _moe(hf3, tile_expert, used, n_valid, tok3, dst3, p, tm=tm, n_out_rows=n_out_rows)
        if l + 1 < depth:
            x = _combine(x, ys3, wts, nf, tc=tc, final=False, row0=0, n_rows=n)
        else:
            outs, row0 = [], 0
            for (b, s), sz in zip(groups, sizes):
                y = _combine(x, ys3, wts, nf, tc=tc, final=True, row0=row0, n_rows=sz)
                outs.append(y.reshape(b, s, D_MODEL))
                row0 += sz
    return outs


def kernel(x_prompt, x_sample, norm_mix, w_in, pool_w, pool_scale, conv_w, conv_b, lru_wa, lru_ba, lru_wx, lru_bx,
           lru_lambda, out_norm_pool, out_norm_lru, w_out, norm_ffn, router_w, router_b, w_up, b_up, w_down,
           b_down, norm_final):
    weights = (norm_mix, w_in, pool_w, pool_scale, conv_w, conv_b, lru_wa, lru_ba, lru_wx, lru_bx, lru_lambda,
               out_norm_pool, out_norm_lru, w_out, norm_ffn, router_w, router_b, w_up, b_up, w_down, b_down)
    y_prompt, y_sample = _trunk([x_prompt, x_sample], weights, norm_final, ts=512, tr=512, tm=512, tc=256)
    return (y_prompt, y_sample)
```

```python
import functools

import numpy as np
import jax
import jax.numpy as jnp
from jax import lax
from jax.experimental import pallas as pl
from jax.experimental.pallas import tpu as pltpu

D_MODEL = 1024
D_POOL = 512
D_LRU = 512
D_IN = D_POOL + 2 * D_LRU
POOL_WINDOWS = (2, 4, 8, 16)
POOL_GROUP_W = D_POOL // len(POOL_WINDOWS)
N_LRU_HEADS = 4
LRU_BLOCK = D_LRU // N_LRU_HEADS
LRU_C = 8.0
CONV_W = 4
CONV_LEFT = 1
N_EXPERTS = 32
TOP_K = 4
D_FF = D_MODEL
SWIGLU_LIMIT = 7.0
SWIGLU_ALPHA = 1.702
EPS = 1e-6

LANES = 128
SUBLANES = 8
HALO = 16
VMEM_LIMIT = 48 * 1024 * 1024
K_CHUNKS = 4

F32 = jnp.float32
BF16 = jnp.bfloat16


def _rms(x, g):
    return x * lax.rsqrt(jnp.mean(x * x, axis=-1, keepdims=True) + EPS) * g


def _shift_rows(v, s):
    n = v.shape[0]
    return pltpu.roll(v, s % n, axis=0)


def _lru_gates(xc, wg_ref, ba_ref, bx_ref, sp_ref, a_scr, b_scr):
    for h in range(N_LRU_HEADS):
        cs = slice(h * LRU_BLOCK, (h + 1) * LRU_BLOCK)
        xh = xc[:, cs]
        gr = jnp.dot(xh.astype(BF16), wg_ref[h], preferred_element_type=F32)
        r = jax.nn.sigmoid(gr[:, :LRU_BLOCK] + ba_ref[:, cs])
        i = jax.nn.sigmoid(gr[:, LRU_BLOCK:] + bx_ref[:, cs])
        log_a = -LRU_C * r * sp_ref[:, cs]
        a = jnp.exp(log_a)
        beta = jnp.sqrt(1.0 - a * a)
        a_scr[:, cs] = a
        b_scr[:, cs] = beta * i * xh


def _scan_rows(a_ref, b_ref, h_ref, carry0, n_rows, reverse):
    row = lax.broadcasted_iota(jnp.int32, (SUBLANES, D_LRU), 0)
    n_grp = n_rows // SUBLANES

    def body(k, carry):
        grp = (n_grp - 1 - k) if reverse else k
        r0 = pl.multiple_of(grp * SUBLANES, SUBLANES)
        a = a_ref[pl.ds(r0, SUBLANES), :]
        b = b_ref[pl.ds(r0, SUBLANES), :]
        for s in (1, 2, 4):
            if reverse:
                ok = row < SUBLANES - s
                a_sh = jnp.where(ok, pltpu.roll(a, SUBLANES - s, axis=0), 1.0)
                b_sh = jnp.where(ok, pltpu.roll(b, SUBLANES - s, axis=0), 0.0)
            else:
                ok = row >= s
                a_sh = jnp.where(ok, pltpu.roll(a, s, axis=0), 1.0)
                b_sh = jnp.where(ok, pltpu.roll(b, s, axis=0), 0.0)
            b = a * b_sh + b
            a = a * a_sh
        h = a * carry + b
        h_ref[pl.ds(r0, SUBLANES), :] = h
        return h[0:1, :] if reverse else h[SUBLANES - 1:SUBLANES, :]

    return lax.fori_loop(0, n_grp, body, carry0)


def _mix_a_kernel(pos_ref, len_ref,
                  x_ref, xp_ref, xn_ref, nm_ref, win_ref, pw_ref, ps_ref, cw_ref, cb_ref,
                  wg_ref, ba_ref, bx_ref, sp_ref, onp_ref,
                  ypn_ref, hf_ref, gate_ref, xc_ref,
                  h_scr, a_scr, b_scr, carry_ref, *, ts):
    g = pl.program_id(0)
    pos0 = pos_ref[g]
    slen = len_ref[g]
    keep_prev = jnp.where(pos0 == 0, 0.0, 1.0).astype(F32)
    keep_next = jnp.where(pos0 + ts == slen, 0.0, 1.0).astype(F32)
    nm = nm_ref[...]

    h_scr[0:HALO, :] = (_rms(xp_ref[...], nm) * keep_prev).astype(BF16)
    h_scr[HALO:HALO + ts, :] = _rms(x_ref[...], nm).astype(BF16)
    h_scr[HALO + ts:, :] = (_rms(xn_ref[...], nm) * keep_next).astype(BF16)
    z = jnp.dot(h_scr[...], win_ref[...], preferred_element_type=F32)

    trow = pos0 + lax.broadcasted_iota(jnp.int32, (ts, POOL_GROUP_W), 0)
    ys = []
    for gi, win in enumerate(POOL_WINDOWS):
        u = z[:, gi * POOL_GROUP_W:(gi + 1) * POOL_GROUP_W]
        acc = u + _shift_rows(u, 1)
        half = 1
        while 2 * half < win:
            acc = _shift_rows(acc, half) + _shift_rows(acc, -half)
            half *= 2
        half = win // 2
        cnt = (jnp.minimum(trow + half, slen) - jnp.maximum(trow - half, 0)).astype(F32)
        p = acc[HALO:HALO + ts] / cnt - u[HALO:HALO + ts]
        ys.append(jnp.dot(p.astype(BF16), pw_ref[gi], preferred_element_type=F32))
    y_pool = jnp.concatenate(ys, axis=-1) * ps_ref[...]
    ypn_ref[...] = _rms(y_pool, onp_ref[...]).astype(BF16)

    ul = z[:, D_POOL:D_POOL + D_LRU]
    xc_ext = cb_ref[...] + _shift_rows(ul, CONV_LEFT) * cw_ref[0:1, :]
    xc_ext = xc_ext + ul * cw_ref[1:2, :]
    xc_ext = xc_ext + _shift_rows(ul, -1) * cw_ref[2:3, :]
    xc_ext = xc_ext + _shift_rows(ul, -2) * cw_ref[3:4, :]
    xc = xc_ext[HALO:HALO + ts]
    xc_ref[...] = xc

    gate_ref[...] = jax.nn.gelu(z[HALO:HALO + ts, D_POOL + D_LRU:], approximate=True)

    _lru_gates(xc, wg_ref, ba_ref, bx_ref, sp_ref, a_scr, b_scr)
    carry0 = jnp.where(pos0 == 0, 0.0, carry_ref[...])
    carry_ref[...] = _scan_rows(a_scr, b_scr, hf_ref, carry0, ts, reverse=False)


def _mix_a(x, tile_pos, tile_len, p, *, ts):
    n = x.shape[0]
    g_tiles = n // ts
    hb = ts // HALO
    n_hblk = n // HALO
    row = lambda g, *_: (g, 0)
    const2 = lambda g, *_: (0, 0)
    const3 = lambda g, *_: (0, 0, 0)
    grid_spec = pltpu.PrefetchScalarGridSpec(
        num_scalar_prefetch=2,
        grid=(g_tiles,),
        in_specs=[
            pl.BlockSpec((ts, D_MODEL), row),
            pl.BlockSpec((HALO, D_MODEL), lambda g, *_: (jnp.maximum(g * hb - 1, 0), 0)),
            pl.BlockSpec((HALO, D_MODEL), lambda g, *_: (jnp.minimum((g + 1) * hb, n_hblk - 1), 0)),
            pl.BlockSpec((1, D_MODEL), const2),
            pl.BlockSpec((D_MODEL, D_IN), const2),
            pl.BlockSpec((len(POOL_WINDOWS), POOL_GROUP_W, POOL_GROUP_W), const3),
            pl.BlockSpec((1, D_POOL), const2),
            pl.BlockSpec((CONV_W, D_LRU), const2),
            pl.BlockSpec((1, D_LRU), const2),
            pl.BlockSpec((N_LRU_HEADS, LRU_BLOCK, 2 * LRU_BLOCK), const3),
            pl.BlockSpec((1, D_LRU), const2),
            pl.BlockSpec((1, D_LRU), const2),
            pl.BlockSpec((1, D_LRU), const2),
            pl.BlockSpec((1, D_POOL), const2),
        ],
        out_specs=[
            pl.BlockSpec((ts, D_POOL), row),
            pl.BlockSpec((ts, D_LRU), row),
            pl.BlockSpec((ts, D_LRU), row),
            pl.BlockSpec((ts, D_LRU), row),
        ],
        scratch_shapes=[
            pltpu.VMEM((ts + 2 * HALO, D_MODEL), BF16),
            pltpu.VMEM((ts, D_LRU), F32),
            pltpu.VMEM((ts, D_LRU), F32),
            pltpu.VMEM((1, D_LRU), F32),
        ],
    )
    return pl.pallas_call(
        functools.partial(_mix_a_kernel, ts=ts),
        grid_spec=grid_spec,
        out_shape=[
            jax.ShapeDtypeStruct((n, D_POOL), BF16),
            jax.ShapeDtypeStruct((n, D_LRU), F32),
            jax.ShapeDtypeStruct((n, D_LRU), F32),
            jax.ShapeDtypeStruct((n, D_LRU), F32),
        ],
        compiler_params=pltpu.CompilerParams(
            dimension_semantics=("arbitrary",), vmem_limit_bytes=VMEM_LIMIT),
        name="mix_a",
    )(tile_pos, tile_len, x, x, x, p["norm_mix"], p["w_in"], p["pool_w"], p["pool_scale"],
      p["conv_w"], p["conv_b"], p["wg_f"], p["ba_f"], p["bx_f"], p["sp_f"], p["out_norm_pool"])


def _mix_b_kernel(pos_ref, len_ref,
                  x_ref, xc_ref, hf_ref, gate_ref, ypn_ref,
                  wg_ref, ba_ref, bx_ref, sp_ref, onl_ref, wout_ref,
                  o_ref,
                  a_scr, b_scr, hb_scr, carry_ref, *, ts, g_tiles):
    g = g_tiles - 1 - pl.program_id(0)
    at_end = pos_ref[g] + ts == len_ref[g]
    _lru_gates(xc_ref[...], wg_ref, ba_ref, bx_ref, sp_ref, a_scr, b_scr)
    carry0 = jnp.where(at_end, 0.0, carry_ref[...])
    carry_ref[...] = _scan_rows(a_scr, b_scr, hb_scr, carry0, ts, reverse=True)
    y_lru = (hf_ref[...] + hb_scr[...]) * gate_ref[...]
    yln = _rms(y_lru, onl_ref[...]).astype(BF16)
    out = jnp.dot(ypn_ref[...], wout_ref[0:D_POOL, :], preferred_element_type=F32)
    out = out + jnp.dot(yln, wout_ref[D_POOL:, :], preferred_element_type=F32)
    o_ref[...] = x_ref[...] + out


def _mix_b(x, xc, hf, gate, ypn, tile_pos, tile_len, p, *, ts):
    n = x.shape[0]
    g_tiles = n // ts
    row = lambda g, *_: (g_tiles - 1 - g, 0)
    const2 = lambda g, *_: (0, 0)
    const3 = lambda g, *_: (0, 0, 0)
    grid_spec = pltpu.PrefetchScalarGridSpec(
        num_scalar_prefetch=2,
        grid=(g_tiles,),
        in_specs=[
            pl.BlockSpec((ts, D_MODEL), row),
            pl.BlockSpec((ts, D_LRU), row),
            pl.BlockSpec((ts, D_LRU), row),
            pl.BlockSpec((ts, D_LRU), row),
            pl.BlockSpec((ts, D_POOL), row),
            pl.BlockSpec((N_LRU_HEADS, LRU_BLOCK, 2 * LRU_BLOCK), const3),
            pl.BlockSpec((1, D_LRU), const2),
            pl.BlockSpec((1, D_LRU), const2),
            pl.BlockSpec((1, D_LRU), const2),
            pl.BlockSpec((1, D_LRU), const2),
            pl.BlockSpec((D_POOL + D_LRU, D_MODEL), const2),
        ],
        out_specs=pl.BlockSpec((ts, D_MODEL), row),
        scratch_shapes=[
            pltpu.VMEM((ts, D_LRU), F32),
            pltpu.VMEM((ts, D_LRU), F32),
            pltpu.VMEM((ts, D_LRU), F32),
            pltpu.VMEM((1, D_LRU), F32),
        ],
    )
    return pl.pallas_call(
        functools.partial(_mix_b_kernel, ts=ts, g_tiles=g_tiles),
        grid_spec=grid_spec,
        out_shape=jax.ShapeDtypeStruct((n, D_MODEL), F32),
        compiler_params=pltpu.CompilerParams(
            dimension_semantics=("arbitrary",), vmem_limit_bytes=VMEM_LIMIT),
        name="mix_b",
    )(tile_pos, tile_len, x, xc, hf, gate, ypn, p["wg_b"], p["ba_b"], p["bx_b"], p["sp_b"],
      p["out_norm_lru"], p["w_out"])


def _router_kernel(x_ref, nf_ref, rw_ref, rb_ref,
                   hf_ref, meta_ref, wts_ref, cnt_ref,
                   carry_ref, *, tr):
    @pl.when(pl.program_id(0) == 0)
    def _():
        carry_ref[...] = jnp.zeros_like(carry_ref)

    hf = _rms(x_ref[...], nf_ref[...])
    hf_ref[...] = hf.reshape(tr, 1, D_MODEL)
    logits = jnp.dot(hf, rw_ref[...], preferred_element_type=F32,
                     precision=lax.Precision.HIGHEST) + rb_ref[...]
    lane = lax.broadcasted_iota(jnp.int32, (tr, LANES), 1).astype(F32)

    vals, idxs = [], []
    l = logits
    for _ in range(TOP_K):
        m = jnp.max(l, axis=-1, keepdims=True)
        ik = jnp.min(jnp.where(l == m, lane, float(LANES)), axis=-1, keepdims=True)
        vals.append(m)
        idxs.append(ik)
        l = jnp.where(lane == ik, -jnp.inf, l)
    es = [jnp.exp(v - vals[0]) for v in vals]
    den = es[0] + es[1] + es[2] + es[3]

    hits = [lane == ik for ik in idxs]
    chosen = jnp.zeros((tr, LANES), F32)
    for hit in hits:
        chosen = chosen + jnp.where(hit, 1.0, 0.0)
    tri = jnp.where(lax.broadcasted_iota(jnp.int32, (tr, tr), 1) < lax.broadcasted_iota(jnp.int32, (tr, tr), 0),
                    1.0, 0.0).astype(BF16)
    before = jnp.dot(tri, chosen.astype(BF16), preferred_element_type=F32) + carry_ref[...]
    total = carry_ref[...] + jnp.sum(chosen, axis=0, keepdims=True)
    carry_ref[...] = total
    cnt_ref[...] = total

    meta = jnp.zeros((tr, LANES), F32)
    wts = jnp.zeros((tr, LANES), F32)
    for k in range(TOP_K):
        pos_k = jnp.sum(jnp.where(hits[k], before, 0.0), axis=-1, keepdims=True)
        meta = jnp.where(lane == float(k), idxs[k], meta)
        meta = jnp.where(lane == float(TOP_K + k), pos_k, meta)
        wts = jnp.where(lane == float(k), es[k] / den, wts)
    meta_ref[...] = meta.astype(jnp.int32)
    wts_ref[...] = wts


def _router(x, p, *, tr):
    n = x.shape[0]
    row = lambda j: (j, 0)
    const2 = lambda j: (0, 0)
    return pl.pallas_call(
        functools.partial(_router_kernel, tr=tr),
        grid=(n // tr,),
        in_specs=[
            pl.BlockSpec((tr, D_MODEL), row),
            pl.BlockSpec((1, D_MODEL), const2),
            pl.BlockSpec((D_MODEL, LANES), const2),
            pl.BlockSpec((1, LANES), const2),
        ],
        out_specs=[
            pl.BlockSpec((tr, 1, D_MODEL), lambda j: (j, 0, 0)),
            pl.BlockSpec((tr, LANES), row),
            pl.BlockSpec((tr, LANES), row),
            pl.BlockSpec((1, LANES), const2),
        ],
        out_shape=[
            jax.ShapeDtypeStruct((n, 1, D_MODEL), F32),
            jax.ShapeDtypeStruct((n, LANES), jnp.int32),
            jax.ShapeDtypeStruct((n, LANES), F32),
            jax.ShapeDtypeStruct((1, LANES), F32),
        ],
        scratch_shapes=[pltpu.VMEM((1, LANES), F32)],
        compiler_params=pltpu.CompilerParams(
            dimension_semantics=("arbitrary",), vmem_limit_bytes=VMEM_LIMIT),
        name="router",
    )(x, p["norm_ffn"], p["router_w"], p["router_b"])


def _moe_kernel(te_ref,
                tok0_ref, tokn_ref, dst_ref, hf_hbm, wup_ref, bup_ref, wdn_ref, bdn_ref,
                ys_hbm,
                xbuf, obuf, x2d, o2d, gsem, ssem, *, tm, nt):
    s = pl.program_id(0)
    slot = lax.rem(s, 2)
    other = 1 - slot

    def gather_wait(b):
        pltpu.make_async_copy(hf_hbm.at[pl.ds(0, tm)], xbuf.at[pl.ds(b * tm, tm)], gsem.at[b]).wait()

    def scatter_wait(b):
        pltpu.make_async_copy(obuf.at[pl.ds(b * tm, tm)], ys_hbm.at[pl.ds(0, tm)], ssem.at[b]).wait()

    def gather_row(tok, b, r):
        pltpu.make_async_copy(hf_hbm.at[pl.ds(tok, 1)], xbuf.at[pl.ds(b * tm + r, 1)], gsem.at[b]).start()

    def scatter_row(dst, b, r):
        pltpu.make_async_copy(obuf.at[pl.ds(b * tm + r, 1)], ys_hbm.at[pl.ds(dst, 1)], ssem.at[b]).start()

    @pl.when(s == 0)
    def _():
        obuf[...] = jnp.zeros_like(obuf)
        o2d[...] = jnp.zeros_like(o2d)

        def body(r, c):
            gather_row(tok0_ref[0, 0, r], 0, r)
            return c
        lax.fori_loop(0, tm, body, 0)

    @pl.when(s >= 1)
    def _():
        scatter_wait(other)

    gather_wait(slot)

    xrows = xbuf.at[pl.ds(pl.multiple_of(slot * tm, tm), tm)]
    orows = obuf.at[pl.ds(pl.multiple_of(other * tm, tm), tm)]
    kw = D_MODEL // K_CHUNKS
    bounds = [(tm * g) // (K_CHUNKS - 1) for g in range(K_CHUNKS)] + [tm]
    hh = bup_ref[...]
    for kc in range(K_CHUNKS):
        cols = slice(kc * kw, (kc + 1) * kw)
        x2d[:, cols] = xrows[:, :, cols].reshape(tm, kw)
        orows[:, :, cols] = o2d[:, cols].reshape(tm, 1, kw)
        for r in range(bounds[kc], bounds[kc + 1]):
            gather_row(tokn_ref[0, 0, r], other, r)
            scatter_row(dst_ref[0, 0, r], slot, r)
        hh = hh + jnp.dot(x2d[:, cols].astype(BF16), wup_ref[cols, :], preferred_element_type=F32)
    x_glu = jnp.minimum(hh[:, :D_FF], SWIGLU_LIMIT)
    x_lin = jnp.clip(hh[:, D_FF:], -SWIGLU_LIMIT, SWIGLU_LIMIT)
    act = (x_glu * jax.nn.sigmoid(SWIGLU_ALPHA * x_glu) * (x_lin + 1.0)).astype(BF16)
    o2d[...] = jnp.dot(act, wdn_ref[...], preferred_element_type=F32) + bdn_ref[...]

    @pl.when(s == nt - 1)
    def _():
        scatter_wait(slot)
        gather_wait(other)


def _moe(hf3, tile_expert, tok3, dst3, p, *, tm, n_out_rows):
    nt = tok3.shape[0]
    smem_blk = lambda fn: pl.BlockSpec((1, 1, tm), fn, memory_space=pltpu.SMEM)
    grid_spec = pltpu.PrefetchScalarGridSpec(
        num_scalar_prefetch=1,
        grid=(nt,),
        in_specs=[
            smem_blk(lambda s, te: (0, 0, 0)),
            smem_blk(lambda s, te: (jnp.minimum(s + 1, nt - 1), 0, 0)),
            smem_blk(lambda s, te: (s, 0, 0)),
            pl.BlockSpec(memory_space=pl.ANY),
            pl.BlockSpec((None, D_MODEL, 2 * D_FF), lambda s, te: (te[s], 0, 0)),
            pl.BlockSpec((None, 1, 2 * D_FF), lambda s, te: (te[s], 0, 0)),
            pl.BlockSpec((None, D_FF, D_MODEL), lambda s, te: (te[s], 0, 0)),
            pl.BlockSpec((None, 1, D_MODEL), lambda s, te: (te[s], 0, 0)),
        ],
        out_specs=pl.BlockSpec(memory_space=pl.ANY),
        scratch_shapes=[
            pltpu.VMEM((2 * tm, 1, D_MODEL), F32),
            pltpu.VMEM((2 * tm, 1, D_MODEL), F32),
            pltpu.VMEM((tm, D_MODEL), F32),
            pltpu.VMEM((tm, D_MODEL), F32),
            pltpu.SemaphoreType.DMA((2,)),
            pltpu.SemaphoreType.DMA((2,)),
        ],
    )
    return pl.pallas_call(
        functools.partial(_moe_kernel, tm=tm, nt=nt),
        grid_spec=grid_spec,
        out_shape=jax.ShapeDtypeStruct((n_out_rows, 1, D_MODEL), F32),
        compiler_params=pltpu.CompilerParams(
            dimension_semantics=("arbitrary",), vmem_limit_bytes=VMEM_LIMIT),
        name="moe",
    )(tile_expert, tok3, tok3, dst3, hf3, p["w_up"], p["b_up"], p["w_down"], p["b_down"])


def _combine_kernel(x_ref, y0_ref, y1_ref, y2_ref, y3_ref, wts_ref, nf_ref, o_ref, y2d, *, tc, final):
    for k, y_ref in enumerate((y0_ref, y1_ref, y2_ref, y3_ref)):
        y2d[k] = y_ref[...].reshape(tc, D_MODEL)
    w = wts_ref[...]
    acc = w[:, 0:1] * y2d[0]
    for k in range(1, TOP_K):
        acc = acc + w[:, k:k + 1] * y2d[k]
    x = x_ref[...] + acc
    o_ref[...] = _rms(x, nf_ref[...]) if final else x


def _combine(x, ys3, wts, norm_final, *, tc, final, row0, n_rows):
    n = x.shape[0]
    b0 = row0 // tc
    kb = n // tc
    row = lambda j: (b0 + j, 0)
    y_spec = lambda k: pl.BlockSpec((tc, 1, D_MODEL), lambda j: (k * kb + b0 + j, 0, 0))
    return pl.pallas_call(
        functools.partial(_combine_kernel, tc=tc, final=final),
        grid=(n_rows // tc,),
        in_specs=[
            pl.BlockSpec((tc, D_MODEL), row),
            y_spec(0), y_spec(1), y_spec(2), y_spec(3),
            pl.BlockSpec((tc, LANES), row),
            pl.BlockSpec((1, D_MODEL), lambda j: (0, 0)),
        ],
        out_specs=pl.BlockSpec((tc, D_MODEL), lambda j: (j, 0)),
        out_shape=jax.ShapeDtypeStruct((n_rows, D_MODEL), F32),
        scratch_shapes=[pltpu.VMEM((TOP_K, tc, D_MODEL), F32)],
        compiler_params=pltpu.CompilerParams(
            dimension_semantics=("arbitrary",), vmem_limit_bytes=VMEM_LIMIT),
        name="combine_final" if final else "combine",
    )(x, ys3, ys3, ys3, ys3, wts, norm_final)


def _seq_tables(groups, ts):
    pos, length = [], []
    for n_seq, seq_len in groups:
        assert seq_len % ts == 0
        for _ in range(n_seq):
            for t in range(seq_len // ts):
                pos.append(t * ts)
                length.append(seq_len)
    return np.asarray(pos, np.int32), np.asarray(length, np.int32)


def _routing_plan(meta, counts, *, n, tm, nt):
    idx = meta[:, 0:TOP_K]
    pos = meta[:, TOP_K:2 * TOP_K]
    cnt = counts[0, :N_EXPERTS].astype(jnp.int32)
    tiles_e = (cnt + tm - 1) // tm
    tile_end = jnp.cumsum(tiles_e)
    base = (tile_end - tiles_e) * tm
    slot = (jnp.take(base, idx) + pos).T.reshape(-1)
    tile_ids = jnp.arange(nt, dtype=jnp.int32)
    tile_expert = jnp.minimum(jnp.sum(tile_end[None, :] <= tile_ids[:, None], axis=1), N_EXPERTS - 1).astype(jnp.int32)
    pairs = jnp.arange(TOP_K * n, dtype=jnp.int32)
    rows = jnp.arange((nt + 2) * tm, dtype=jnp.int32)
    spare = TOP_K * n + ((rows // tm) % 2) * tm + rows % tm
    dst = spare.at[slot + 2 * tm].set(pairs, unique_indices=True, mode="promise_in_bounds")
    src = dst[2 * tm:]
    tok = jnp.where(src < TOP_K * n, src % n, 0)
    return tile_expert, tok.reshape(nt, 1, tm), dst[:nt * tm].reshape(nt, 1, tm)


def _layer_params(l, norm_mix, w_in, pool_w, pool_scale, conv_w, conv_b, lru_wa, lru_ba, lru_wx, lru_bx,
                  lru_lambda, out_norm_pool, out_norm_lru, w_out, norm_ffn, router_w, router_b,
                  w_up, b_up, w_down, b_down):
    def gates(d):
        return dict(
            wg=jnp.concatenate([lru_wa[l, d], lru_wx[l, d]], axis=-1).astype(BF16),
            ba=lru_ba[l, d].reshape(1, D_LRU),
            bx=lru_bx[l, d].reshape(1, D_LRU),
            sp=jax.nn.softplus(-lru_lambda[l, d]).reshape(1, D_LRU),
        )
    gf, gb = gates(0), gates(1)
    pad = LANES - N_EXPERTS
    return dict(
        norm_mix=norm_mix[l].reshape(1, D_MODEL),
        w_in=w_in[l].astype(BF16),
        pool_w=pool_w[l].astype(BF16),
        pool_scale=pool_scale[l].reshape(1, D_POOL),
        conv_w=conv_w[l],
        conv_b=conv_b[l].reshape(1, D_LRU),
        wg_f=gf["wg"], ba_f=gf["ba"], bx_f=gf["bx"], sp_f=gf["sp"],
        wg_b=gb["wg"], ba_b=gb["ba"], bx_b=gb["bx"], sp_b=gb["sp"],
        out_norm_pool=out_norm_pool[l].reshape(1, D_POOL),
        out_norm_lru=out_norm_lru[l].reshape(1, D_LRU),
        w_out=w_out[l].astype(BF16),
        norm_ffn=norm_ffn[l].reshape(1, D_MODEL),
        router_w=jnp.pad(router_w[l], ((0, 0), (0, pad))),
        router_b=jnp.pad(router_b[l].reshape(1, N_EXPERTS), ((0, 0), (0, pad)), constant_values=-jnp.inf),
        w_up=w_up[l].astype(BF16),
        b_up=b_up[l].reshape(N_EXPERTS, 1, 2 * D_FF),
        w_down=w_down[l].astype(BF16),
        b_down=b_down[l].reshape(N_EXPERTS, 1, D_MODEL),
    )


def _trunk(xs, weights, norm_final, *, ts, tr, tm, tc):
    groups = [(x.shape[0], x.shape[1]) for x in xs]
    sizes = [b * s for b, s in groups]
    n = sum(sizes)
    depth = weights[0].shape[0]
    x = jnp.concatenate([xi.reshape(-1, D_MODEL) for xi in xs], axis=0)
    tile_pos, tile_len = _seq_tables(groups, ts)
    tile_pos, tile_len = jnp.asarray(tile_pos), jnp.asarray(tile_len)
    nt = (TOP_K * n) // tm + N_EXPERTS + 2
    n_out_rows = TOP_K * n + 2 * tm
    nf = norm_final.reshape(1, D_MODEL)
    outs = None
    for l in range(depth):
        p = _layer_params(l, *weights)
        ypn, hf, gate, xc = _mix_a(x, tile_pos, tile_len, p, ts=ts)
        x = _mix_b(x, xc, hf, gate, ypn, tile_pos, tile_len, p, ts=ts)
        hf3, meta, wts, counts = _router(x, p, tr=tr)
        tile_expert, tok3, dst3 = _routing_plan(meta, counts, n=n, tm=tm, nt=nt)
        ys3 = _moe(hf3, tile_expert, tok3, dst3, p, tm=tm, n_out_rows=n_out_rows)
        if l + 1 < depth:
            x = _combine(x, ys3, wts, nf, tc=tc, final=False, row0=0, n_rows=n)
        else:
            outs, row0 = [], 0
            for (b, s), sz in zip(groups, sizes):
                y = _combine(x, ys3, wts, nf, tc=tc, final=True, row0=row0, n_rows=sz)
                outs.append(y.reshape(b, s, D_MODEL))
                row0 += sz
    return outs


def kernel(x_prompt, x_sample, norm_mix, w_in, pool_w, pool_scale, conv_w, conv_b, lru_wa, lru_ba, lru_wx, lru_bx,
           lru_lambda, out_norm_pool, out_norm_lru, w_out, norm_ffn, router_w, router_b, w_up, b_up, w_down,
           b_down, norm_final):
    weights = (norm_mix, w_in, pool_w, pool_scale, conv_w, conv_b, lru_wa, lru_ba, lru_wx, lru_bx, lru_lambda,
               out_norm_pool, out_norm_lru, w_out, norm_ffn, router_w, router_b, w_up, b_up, w_down, b_down)
    y_prompt, y_sample = _trunk([x_prompt, x_sample], weights, norm_final, ts=512, tr=512, tm=512, tc=256)
    return (y_prompt, y_sample)
```

```python
import functools

import numpy as np
import jax
import jax.numpy as jnp
from jax import lax
from jax.experimental import pallas as pl
from jax.experimental.pallas import tpu as pltpu

D_MODEL = 1024
D_POOL = 512
D_LRU = 512
D_IN = D_POOL + 2 * D_LRU
POOL_WINDOWS = (2, 4, 8, 16)
POOL_GROUP_W = D_POOL // len(POOL_WINDOWS)
N_LRU_HEADS = 4
LRU_BLOCK = D_LRU // N_LRU_HEADS
LRU_C = 8.0
CONV_W = 4
CONV_LEFT = 1
N_EXPERTS = 32
TOP_K = 4
D_FF = D_MODEL
SWIGLU_LIMIT = 7.0
SWIGLU_ALPHA = 1.702
EPS = 1e-6

LANES = 128
SUBLANES = 8
HALO = 16
VMEM_LIMIT = 48 * 1024 * 1024
K_CHUNKS = 4
OUT_SLOTS = 3

F32 = jnp.float32
BF16 = jnp.bfloat16


def _rms(x, g):
    return x * lax.rsqrt(jnp.mean(x * x, axis=-1, keepdims=True) + EPS) * g


def _shift_rows(v, s):
    n = v.shape[0]
    return pltpu.roll(v, s % n, axis=0)


def _lru_gates(xc, wg_ref, ba_ref, bx_ref, sp_ref, a_scr, b_scr):
    for h in range(N_LRU_HEADS):
        cs = slice(h * LRU_BLOCK, (h + 1) * LRU_BLOCK)
        xh = xc[:, cs]
        gr = jnp.dot(xh.astype(BF16), wg_ref[h], preferred_element_type=F32)
        r = jax.nn.sigmoid(gr[:, :LRU_BLOCK] + ba_ref[:, cs])
        i = jax.nn.sigmoid(gr[:, LRU_BLOCK:] + bx_ref[:, cs])
        log_a = -LRU_C * r * sp_ref[:, cs]
        a = jnp.exp(log_a)
        beta = jnp.sqrt(1.0 - a * a)
        a_scr[:, cs] = a
        b_scr[:, cs] = beta * i * xh


def _scan_rows(a_ref, b_ref, h_ref, carry0, n_rows, reverse):
    row = lax.broadcasted_iota(jnp.int32, (SUBLANES, D_LRU), 0)
    n_grp = n_rows // SUBLANES

    def body(k, carry):
        grp = (n_grp - 1 - k) if reverse else k
        r0 = pl.multiple_of(grp * SUBLANES, SUBLANES)
        a = a_ref[pl.ds(r0, SUBLANES), :]
        b = b_ref[pl.ds(r0, SUBLANES), :]
        for s in (1, 2, 4):
            if reverse:
                ok = row < SUBLANES - s
                a_sh = jnp.where(ok, pltpu.roll(a, SUBLANES - s, axis=0), 1.0)
                b_sh = jnp.where(ok, pltpu.roll(b, SUBLANES - s, axis=0), 0.0)
            else:
                ok = row >= s
                a_sh = jnp.where(ok, pltpu.roll(a, s, axis=0), 1.0)
                b_sh = jnp.where(ok, pltpu.roll(b, s, axis=0), 0.0)
            b = a * b_sh + b
            a = a * a_sh
        h = a * carry + b
        h_ref[pl.ds(r0, SUBLANES), :] = h
        return h[0:1, :] if reverse else h[SUBLANES - 1:SUBLANES, :]

    return lax.fori_loop(0, n_grp, body, carry0)


def _mix_a_kernel(pos_ref, len_ref,
                  x_ref, xp_ref, xn_ref, nm_ref, win_ref, pw_ref, ps_ref, cw_ref, cb_ref,
                  wg_ref, ba_ref, bx_ref, sp_ref, onp_ref,
                  ypn_ref, hf_ref, gate_ref, xc_ref,
                  h_scr, a_scr, b_scr, carry_ref, *, ts):
    g = pl.program_id(0)
    pos0 = pos_ref[g]
    slen = len_ref[g]
    keep_prev = jnp.where(pos0 == 0, 0.0, 1.0).astype(F32)
    keep_next = jnp.where(pos0 + ts == slen, 0.0, 1.0).astype(F32)
    nm = nm_ref[...]

    h_scr[0:HALO, :] = (_rms(xp_ref[...], nm) * keep_prev).astype(BF16)
    h_scr[HALO:HALO + ts, :] = _rms(x_ref[...], nm).astype(BF16)
    h_scr[HALO + ts:, :] = (_rms(xn_ref[...], nm) * keep_next).astype(BF16)
    z = jnp.dot(h_scr[...], win_ref[...], preferred_element_type=F32)

    trow = pos0 + lax.broadcasted_iota(jnp.int32, (ts, POOL_GROUP_W), 0)
    ys = []
    for gi, win in enumerate(POOL_WINDOWS):
        u = z[:, gi * POOL_GROUP_W:(gi + 1) * POOL_GROUP_W]
        acc = u + _shift_rows(u, 1)
        half = 1
        while 2 * half < win:
            acc = _shift_rows(acc, half) + _shift_rows(acc, -half)
            half *= 2
        half = win // 2
        cnt = (jnp.minimum(trow + half, slen) - jnp.maximum(trow - half, 0)).astype(F32)
        p = acc[HALO:HALO + ts] / cnt - u[HALO:HALO + ts]
        ys.append(jnp.dot(p.astype(BF16), pw_ref[gi], preferred_element_type=F32))
    y_pool = jnp.concatenate(ys, axis=-1) * ps_ref[...]
    ypn_ref[...] = _rms(y_pool, onp_ref[...]).astype(BF16)

    ul = z[:, D_POOL:D_POOL + D_LRU]
    xc_ext = cb_ref[...] + _shift_rows(ul, CONV_LEFT) * cw_ref[0:1, :]
    xc_ext = xc_ext + ul * cw_ref[1:2, :]
    xc_ext = xc_ext + _shift_rows(ul, -1) * cw_ref[2:3, :]
    xc_ext = xc_ext + _shift_rows(ul, -2) * cw_ref[3:4, :]
    xc = xc_ext[HALO:HALO + ts]
    xc_ref[...] = xc

    gate_ref[...] = jax.nn.gelu(z[HALO:HALO + ts, D_POOL + D_LRU:], approximate=True)

    _lru_gates(xc, wg_ref, ba_ref, bx_ref, sp_ref, a_scr, b_scr)
    carry0 = jnp.where(pos0 == 0, 0.0, carry_ref[...])
    carry_ref[...] = _scan_rows(a_scr, b_scr, hf_ref, carry0, ts, reverse=False)


def _mix_a(x, tile_pos, tile_len, p, *, ts):
    n = x.shape[0]
    g_tiles = n // ts
    hb = ts // HALO
    n_hblk = n // HALO
    row = lambda g, *_: (g, 0)
    const2 = lambda g, *_: (0, 0)
    const3 = lambda g, *_: (0, 0, 0)
    grid_spec = pltpu.PrefetchScalarGridSpec(
        num_scalar_prefetch=2,
        grid=(g_tiles,),
        in_specs=[
            pl.BlockSpec((ts, D_MODEL), row),
            pl.BlockSpec((HALO, D_MODEL), lambda g, *_: (jnp.maximum(g * hb - 1, 0), 0)),
            pl.BlockSpec((HALO, D_MODEL), lambda g, *_: (jnp.minimum((g + 1) * hb, n_hblk - 1), 0)),
            pl.BlockSpec((1, D_MODEL), const2),
            pl.BlockSpec((D_MODEL, D_IN), const2),
            pl.BlockSpec((len(POOL_WINDOWS), POOL_GROUP_W, POOL_GROUP_W), const3),
            pl.BlockSpec((1, D_POOL), const2),
            pl.BlockSpec((CONV_W, D_LRU), const2),
            pl.BlockSpec((1, D_LRU), const2),
            pl.BlockSpec((N_LRU_HEADS, LRU_BLOCK, 2 * LRU_BLOCK), const3),
            pl.BlockSpec((1, D_LRU), const2),
            pl.BlockSpec((1, D_LRU), const2),
            pl.BlockSpec((1, D_LRU), const2),
            pl.BlockSpec((1, D_POOL), const2),
        ],
        out_specs=[
            pl.BlockSpec((ts, D_POOL), row),
            pl.BlockSpec((ts, D_LRU), row),
            pl.BlockSpec((ts, D_LRU), row),
            pl.BlockSpec((ts, D_LRU), row),
        ],
        scratch_shapes=[
            pltpu.VMEM((ts + 2 * HALO, D_MODEL), BF16),
            pltpu.VMEM((ts, D_LRU), F32),
            pltpu.VMEM((ts, D_LRU), F32),
            pltpu.VMEM((1, D_LRU), F32),
        ],
    )
    return pl.pallas_call(
        functools.partial(_mix_a_kernel, ts=ts),
        grid_spec=grid_spec,
        out_shape=[
            jax.ShapeDtypeStruct((n, D_POOL), BF16),
            jax.ShapeDtypeStruct((n, D_LRU), F32),
            jax.ShapeDtypeStruct((n, D_LRU), F32),
            jax.ShapeDtypeStruct((n, D_LRU), F32),
        ],
        compiler_params=pltpu.CompilerParams(
            dimension_semantics=("arbitrary",), vmem_limit_bytes=VMEM_LIMIT),
        name="mix_a",
    )(tile_pos, tile_len, x, x, x, p["norm_mix"], p["w_in"], p["pool_w"], p["pool_scale"],
      p["conv_w"], p["conv_b"], p["wg_f"], p["ba_f"], p["bx_f"], p["sp_f"], p["out_norm_pool"])


def _mix_b_kernel(pos_ref, len_ref,
                  x_ref, xc_ref, hf_ref, gate_ref, ypn_ref,
                  wg_ref, ba_ref, bx_ref, sp_ref, onl_ref, wout_ref,
                  o_ref,
                  a_scr, b_scr, hb_scr, carry_ref, *, ts, g_tiles):
    g = g_tiles - 1 - pl.program_id(0)
    at_end = pos_ref[g] + ts == len_ref[g]
    _lru_gates(xc_ref[...], wg_ref, ba_ref, bx_ref, sp_ref, a_scr, b_scr)
    carry0 = jnp.where(at_end, 0.0, carry_ref[...])
    carry_ref[...] = _scan_rows(a_scr, b_scr, hb_scr, carry0, ts, reverse=True)
    y_lru = (hf_ref[...] + hb_scr[...]) * gate_ref[...]
    yln = _rms(y_lru, onl_ref[...]).astype(BF16)
    out = jnp.dot(ypn_ref[...], wout_ref[0:D_POOL, :], preferred_element_type=F32)
    out = out + jnp.dot(yln, wout_ref[D_POOL:, :], preferred_element_type=F32)
    o_ref[...] = x_ref[...] + out


def _mix_b(x, xc, hf, gate, ypn, tile_pos, tile_len, p, *, ts):
    n = x.shape[0]
    g_tiles = n // ts
    row = lambda g, *_: (g_tiles - 1 - g, 0)
    const2 = lambda g, *_: (0, 0)
    const3 = lambda g, *_: (0, 0, 0)
    grid_spec = pltpu.PrefetchScalarGridSpec(
        num_scalar_prefetch=2,
        grid=(g_tiles,),
        in_specs=[
            pl.BlockSpec((ts, D_MODEL), row),
            pl.BlockSpec((ts, D_LRU), row),
            pl.BlockSpec((ts, D_LRU), row),
            pl.BlockSpec((ts, D_LRU), row),
            pl.BlockSpec((ts, D_POOL), row),
            pl.BlockSpec((N_LRU_HEADS, LRU_BLOCK, 2 * LRU_BLOCK), const3),
            pl.BlockSpec((1, D_LRU), const2),
            pl.BlockSpec((1, D_LRU), const2),
            pl.BlockSpec((1, D_LRU), const2),
            pl.BlockSpec((1, D_LRU), const2),
            pl.BlockSpec((D_POOL + D_LRU, D_MODEL), const2),
        ],
        out_specs=pl.BlockSpec((ts, D_MODEL), row),
        scratch_shapes=[
            pltpu.VMEM((ts, D_LRU), F32),
            pltpu.VMEM((ts, D_LRU), F32),
            pltpu.VMEM((ts, D_LRU), F32),
            pltpu.VMEM((1, D_LRU), F32),
        ],
    )
    return pl.pallas_call(
        functools.partial(_mix_b_kernel, ts=ts, g_tiles=g_tiles),
        grid_spec=grid_spec,
        out_shape=jax.ShapeDtypeStruct((n, D_MODEL), F32),
        compiler_params=pltpu.CompilerParams(
            dimension_semantics=("arbitrary",), vmem_limit_bytes=VMEM_LIMIT),
        name="mix_b",
    )(tile_pos, tile_len, x, xc, hf, gate, ypn, p["wg_b"], p["ba_b"], p["bx_b"], p["sp_b"],
      p["out_norm_lru"], p["w_out"])


def _router_kernel(x_ref, nf_ref, rwh_ref, rwl_ref, rb_ref,
                   hf_ref, meta_ref, wts_ref, cnt_ref,
                   carry_ref, *, tr):
    @pl.when(pl.program_id(0) == 0)
    def _():
        carry_ref[...] = jnp.zeros_like(carry_ref)

    hf = _rms(x_ref[...], nf_ref[...])
    hf_ref[...] = hf.reshape(tr, 1, D_MODEL)
    h_hi = hf.astype(BF16)
    h_lo = (hf - h_hi.astype(F32)).astype(BF16)
    logits = (jnp.dot(h_hi, rwh_ref[...], preferred_element_type=F32)
              + jnp.dot(h_lo, rwh_ref[...], preferred_element_type=F32)
              + jnp.dot(h_hi, rwl_ref[...], preferred_element_type=F32)) + rb_ref[...]
    lane = lax.broadcasted_iota(jnp.int32, (tr, LANES), 1).astype(F32)

    vals, idxs = [], []
    l = logits
    for _ in range(TOP_K):
        m = jnp.max(l, axis=-1, keepdims=True)
        ik = jnp.min(jnp.where(l == m, lane, float(LANES)), axis=-1, keepdims=True)
        vals.append(m)
        idxs.append(ik)
        l = jnp.where(lane == ik, -jnp.inf, l)
    es = [jnp.exp(v - vals[0]) for v in vals]
    den = es[0] + es[1] + es[2] + es[3]

    hits = [lane == ik for ik in idxs]
    chosen = jnp.zeros((tr, LANES), F32)
    for hit in hits:
        chosen = chosen + jnp.where(hit, 1.0, 0.0)
    tri = jnp.where(lax.broadcasted_iota(jnp.int32, (tr, tr), 1) < lax.broadcasted_iota(jnp.int32, (tr, tr), 0),
                    1.0, 0.0).astype(BF16)
    before = jnp.dot(tri, chosen.astype(BF16), preferred_element_type=F32) + carry_ref[...]
    total = carry_ref[...] + jnp.sum(chosen, axis=0, keepdims=True)
    carry_ref[...] = total
    cnt_ref[...] = total

    meta = jnp.zeros((tr, LANES), F32)
    wts = jnp.zeros((tr, LANES), F32)
    for k in range(TOP_K):
        pos_k = jnp.sum(jnp.where(hits[k], before, 0.0), axis=-1, keepdims=True)
        meta = jnp.where(lane == float(k), idxs[k], meta)
        meta = jnp.where(lane == float(TOP_K + k), pos_k, meta)
        wts = jnp.where(lane == float(k), es[k] / den, wts)
    meta_ref[...] = meta.astype(jnp.int32)
    wts_ref[...] = wts


def _router(x, p, *, tr):
    n = x.shape[0]
    row = lambda j: (j, 0)
    const2 = lambda j: (0, 0)
    return pl.pallas_call(
        functools.partial(_router_kernel, tr=tr),
        grid=(n // tr,),
        in_specs=[
            pl.BlockSpec((tr, D_MODEL), row),
            pl.BlockSpec((1, D_MODEL), const2),
            pl.BlockSpec((D_MODEL, LANES), const2),
            pl.BlockSpec((D_MODEL, LANES), const2),
            pl.BlockSpec((1, LANES), const2),
        ],
        out_specs=[
            pl.BlockSpec((tr, 1, D_MODEL), lambda j: (j, 0, 0)),
            pl.BlockSpec((tr, LANES), row),
            pl.BlockSpec((tr, LANES), row),
            pl.BlockSpec((1, LANES), const2),
        ],
        out_shape=[
            jax.ShapeDtypeStruct((n, 1, D_MODEL), F32),
            jax.ShapeDtypeStruct((n, LANES), jnp.int32),
            jax.ShapeDtypeStruct((n, LANES), F32),
            jax.ShapeDtypeStruct((1, LANES), F32),
        ],
        scratch_shapes=[pltpu.VMEM((1, LANES), F32)],
        compiler_params=pltpu.CompilerParams(
            dimension_semantics=("arbitrary",), vmem_limit_bytes=VMEM_LIMIT),
        name="router",
    )(x, p["norm_ffn"], p["router_w_hi"], p["router_w_lo"], p["router_b"])


def _moe_kernel(te_ref,
                tok0_ref, tokn_ref, dst_ref, hf_hbm, wup_ref, bup_ref, wdn_ref, bdn_ref,
                ys_hbm,
                xbuf, obuf, x2d, o2d, gsem, ssem, *, tm, nt):
    s = pl.program_id(0)
    slot = lax.rem(s, 2)
    other = 1 - slot
    o_fill = lax.rem(s + (OUT_SLOTS - 1), OUT_SLOTS)
    o_send = lax.rem(s + (OUT_SLOTS - 2), OUT_SLOTS)

    def gather_wait(b):
        pltpu.make_async_copy(hf_hbm.at[pl.ds(0, tm)], xbuf.at[pl.ds(b * tm, tm)], gsem.at[b]).wait()

    def scatter_wait(b):
        pltpu.make_async_copy(obuf.at[pl.ds(b * tm, tm)], ys_hbm.at[pl.ds(0, tm)], ssem.at[b]).wait()

    def gather_row(tok, b, r):
        pltpu.make_async_copy(hf_hbm.at[pl.ds(tok, 1)], xbuf.at[pl.ds(b * tm + r, 1)], gsem.at[b]).start(priority=0)

    def scatter_row(dst, b, r):
        pltpu.make_async_copy(obuf.at[pl.ds(b * tm + r, 1)], ys_hbm.at[pl.ds(dst, 1)], ssem.at[b]).start(priority=1)

    @pl.when(s == 0)
    def _():
        obuf[...] = jnp.zeros_like(obuf)
        o2d[...] = jnp.zeros_like(o2d)

        def body(r, c):
            gather_row(tok0_ref[0, 0, r], 0, r)
            return c
        lax.fori_loop(0, tm, body, 0)

    @pl.when(s >= OUT_SLOTS - 1)
    def _():
        scatter_wait(o_fill)

    gather_wait(slot)

    xrows = xbuf.at[pl.ds(pl.multiple_of(slot * tm, tm), tm)]
    orows = obuf.at[pl.ds(pl.multiple_of(o_fill * tm, tm), tm)]
    kw = D_MODEL // K_CHUNKS
    bounds = [(tm * g) // (K_CHUNKS - 1) for g in range(K_CHUNKS)] + [tm]
    hh = bup_ref[...]
    for kc in range(K_CHUNKS):
        cols = slice(kc * kw, (kc + 1) * kw)
        x2d[:, cols] = xrows[:, :, cols].reshape(tm, kw)
        orows[:, :, cols] = o2d[:, cols].reshape(tm, 1, kw)
        for r in range(bounds[kc], bounds[kc + 1]):
            gather_row(tokn_ref[0, 0, r], other, r)
            scatter_row(dst_ref[0, 0, r], o_send, r)
        hh = hh + jnp.dot(x2d[:, cols].astype(BF16), wup_ref[cols, :], preferred_element_type=F32)
    x_glu = jnp.minimum(hh[:, :D_FF], SWIGLU_LIMIT)
    x_lin = jnp.clip(hh[:, D_FF:], -SWIGLU_LIMIT, SWIGLU_LIMIT)
    act = (x_glu * jax.nn.sigmoid(SWIGLU_ALPHA * x_glu) * (x_lin + 1.0)).astype(BF16)
    o2d[...] = jnp.dot(act, wdn_ref[...], preferred_element_type=F32) + bdn_ref[...]

    @pl.when(s == nt - 1)
    def _():
        scatter_wait(o_send)
        scatter_wait(lax.rem(s + (OUT_SLOTS - 3), OUT_SLOTS))
        gather_wait(other)


def _moe(hf3, tile_expert, tok3, dst3, p, *, tm, n_out_rows):
    nt = tok3.shape[0]
    smem_blk = lambda fn: pl.BlockSpec((1, 1, tm), fn, memory_space=pltpu.SMEM)
    grid_spec = pltpu.PrefetchScalarGridSpec(
        num_scalar_prefetch=1,
        grid=(nt,),
        in_specs=[
            smem_blk(lambda s, te: (0, 0, 0)),
            smem_blk(lambda s, te: (jnp.minimum(s + 1, nt - 1), 0, 0)),
            smem_blk(lambda s, te: (s, 0, 0)),
            pl.BlockSpec(memory_space=pl.ANY),
            pl.BlockSpec((None, D_MODEL, 2 * D_FF), lambda s, te: (te[s], 0, 0)),
            pl.BlockSpec((None, 1, 2 * D_FF), lambda s, te: (te[s], 0, 0)),
            pl.BlockSpec((None, D_FF, D_MODEL), lambda s, te: (te[s], 0, 0)),
            pl.BlockSpec((None, 1, D_MODEL), lambda s, te: (te[s], 0, 0)),
        ],
        out_specs=pl.BlockSpec(memory_space=pl.ANY),
        scratch_shapes=[
            pltpu.VMEM((2 * tm, 1, D_MODEL), F32),
            pltpu.VMEM((OUT_SLOTS * tm, 1, D_MODEL), F32),
            pltpu.VMEM((tm, D_MODEL), F32),
            pltpu.VMEM((tm, D_MODEL), F32),
            pltpu.SemaphoreType.DMA((2,)),
            pltpu.SemaphoreType.DMA((OUT_SLOTS,)),
        ],
    )
    return pl.pallas_call(
        functools.partial(_moe_kernel, tm=tm, nt=nt),
        grid_spec=grid_spec,
        out_shape=jax.ShapeDtypeStruct((n_out_rows, 1, D_MODEL), F32),
        compiler_params=pltpu.CompilerParams(
            dimension_semantics=("arbitrary",), vmem_limit_bytes=VMEM_LIMIT),
        name="moe",
    )(tile_expert, tok3, tok3, dst3, hf3, p["w_up"], p["b_up"], p["w_down"], p["b_down"])


def _combine_kernel(x_ref, y0_ref, y1_ref, y2_ref, y3_ref, wts_ref, nf_ref, o_ref, y2d, *, tc, final):
    for k, y_ref in enumerate((y0_ref, y1_ref, y2_ref, y3_ref)):
        y2d[k] = y_ref[...].reshape(tc, D_MODEL)
    w = wts_ref[...]
    acc = w[:, 0:1] * y2d[0]
    for k in range(1, TOP_K):
        acc = acc + w[:, k:k + 1] * y2d[k]
    x = x_ref[...] + acc
    o_ref[...] = _rms(x, nf_ref[...]) if final else x


def _combine(x, ys3, wts, norm_final, *, tc, final, row0, n_rows):
    n = x.shape[0]
    b0 = row0 // tc
    kb = n // tc
    row = lambda j: (b0 + j, 0)
    y_spec = lambda k: pl.BlockSpec((tc, 1, D_MODEL), lambda j: (k * kb + b0 + j, 0, 0))
    return pl.pallas_call(
        functools.partial(_combine_kernel, tc=tc, final=final),
        grid=(n_rows // tc,),
        in_specs=[
            pl.BlockSpec((tc, D_MODEL), row),
            y_spec(0), y_spec(1), y_spec(2), y_spec(3),
            pl.BlockSpec((tc, LANES), row),
            pl.BlockSpec((1, D_MODEL), lambda j: (0, 0)),
        ],
        out_specs=pl.BlockSpec((tc, D_MODEL), lambda j: (j, 0)),
        out_shape=jax.ShapeDtypeStruct((n_rows, D_MODEL), F32),
        scratch_shapes=[pltpu.VMEM((TOP_K, tc, D_MODEL), F32)],
        compiler_params=pltpu.CompilerParams(
            dimension_semantics=("arbitrary",), vmem_limit_bytes=VMEM_LIMIT),
        name="combine_final" if final else "combine",
    )(x, ys3, ys3, ys3, ys3, wts, norm_final)


def _seq_tables(groups, ts):
    pos, length = [], []
    for n_seq, seq_len in groups:
        assert seq_len % ts == 0
        for _ in range(n_seq):
            for t in range(seq_len // ts):
                pos.append(t * ts)
                length.append(seq_len)
    return np.asarray(pos, np.int32), np.asarray(length, np.int32)


def _routing_plan(meta, counts, *, n, tm, nt):
    idx = meta[:, 0:TOP_K]
    pos = meta[:, TOP_K:2 * TOP_K]
    cnt = counts[0, :N_EXPERTS].astype(jnp.int32)
    tiles_e = (cnt + tm - 1) // tm
    tile_end = jnp.cumsum(tiles_e)
    base = (tile_end - tiles_e) * tm
    slot = (jnp.take(base, idx) + pos).T.reshape(-1)
    tile_ids = jnp.arange(nt, dtype=jnp.int32)
    tile_expert = jnp.minimum(jnp.sum(tile_end[None, :] <= tile_ids[:, None], axis=1), N_EXPERTS - 1).astype(jnp.int32)
    pairs = jnp.arange(TOP_K * n, dtype=jnp.int32)
    rows = jnp.arange((nt + 2) * tm, dtype=jnp.int32)
    spare = TOP_K * n + ((rows // tm) % 2) * tm + rows % tm
    dst = spare.at[slot + 2 * tm].set(pairs, unique_indices=True, mode="promise_in_bounds")
    src = dst[2 * tm:]
    tok = jnp.where(src < TOP_K * n, src % n, 0)
    return tile_expert, tok.reshape(nt, 1, tm), dst[:nt * tm].reshape(nt, 1, tm)


def _layer_params(l, norm_mix, w_in, pool_w, pool_scale, conv_w, conv_b, lru_wa, lru_ba, lru_wx, lru_bx,
                  lru_lambda, out_norm_pool, out_norm_lru, w_out, norm_ffn, router_w, router_b,
                  w_up, b_up, w_down, b_down):
    def gates(d):
        return dict(
            wg=jnp.concatenate([lru_wa[l, d], lru_wx[l, d]], axis=-1).astype(BF16),
            ba=lru_ba[l, d].reshape(1, D_LRU),
            bx=lru_bx[l, d].reshape(1, D_LRU),
            sp=jax.nn.softplus(-lru_lambda[l, d]).reshape(1, D_LRU),
        )
    gf, gb = gates(0), gates(1)
    pad = LANES - N_EXPERTS
    rw = jnp.pad(router_w[l], ((0, 0), (0, pad)))
    rw_hi = rw.astype(BF16)
    return dict(
        norm_mix=norm_mix[l].reshape(1, D_MODEL),
        w_in=w_in[l].astype(BF16),
        pool_w=pool_w[l].astype(BF16),
        pool_scale=pool_scale[l].reshape(1, D_POOL),
        conv_w=conv_w[l],
        conv_b=conv_b[l].reshape(1, D_LRU),
        wg_f=gf["wg"], ba_f=gf["ba"], bx_f=gf["bx"], sp_f=gf["sp"],
        wg_b=gb["wg"], ba_b=gb["ba"], bx_b=gb["bx"], sp_b=gb["sp"],
        out_norm_pool=out_norm_pool[l].reshape(1, D_POOL),
        out_norm_lru=out_norm_lru[l].reshape(1, D_LRU),
        w_out=w_out[l].astype(BF16),
        norm_ffn=norm_ffn[l].reshape(1, D_MODEL),
        router_w_hi=rw_hi,
        router_w_lo=(rw - rw_hi.astype(F32)).astype(BF16),
        router_b=jnp.pad(router_b[l].reshape(1, N_EXPERTS), ((0, 0), (0, pad)), constant_values=-jnp.inf),
        w_up=w_up[l].astype(BF16),
        b_up=b_up[l].reshape(N_EXPERTS, 1, 2 * D_FF),
        w_down=w_down[l].astype(BF16),
        b_down=b_down[l].reshape(N_EXPERTS, 1, D_MODEL),
    )


def _trunk(xs, weights, norm_final, *, ts, tr, tm, tc):
    groups = [(x.shape[0], x.shape[1]) for x in xs]
    sizes = [b * s for b, s in groups]
    n = sum(sizes)
    depth = weights[0].shape[0]
    x = jnp.concatenate([xi.reshape(-1, D_MODEL) for xi in xs], axis=0)
    tile_pos, tile_len = _seq_tables(groups, ts)
    tile_pos, tile_len = jnp.asarray(tile_pos), jnp.asarray(tile_len)
    nt = (TOP_K * n) // tm + N_EXPERTS + 2
    n_out_rows = TOP_K * n + 2 * tm
    nf = norm_final.reshape(1, D_MODEL)
    outs = None
    for l in range(depth):
        p = _layer_params(l, *weights)
        ypn, hf, gate, xc = _mix_a(x, tile_pos, tile_len, p, ts=ts)
        x = _mix_b(x, xc, hf, gate, ypn, tile_pos, tile_len, p, ts=ts)
        hf3, meta, wts, counts = _router(x, p, tr=tr)
        tile_expert, tok3, dst3 = _routing_plan(meta, counts, n=n, tm=tm, nt=nt)
        ys3 = _moe(hf3, tile_expert, tok3, dst3, p, tm=tm, n_out_rows=n_out_rows)
        if l + 1 < depth:
            x = _combine(x, ys3, wts, nf, tc=tc, final=False, row0=0, n_rows=n)
        else:
            outs, row0 = [], 0
            for (b, s), sz in zip(groups, sizes):
                y = _combine(x, ys3, wts, nf, tc=tc, final=True, row0=row0, n_rows=sz)
                outs.append(y.reshape(b, s, D_MODEL))
                row0 += sz
    return outs


def kernel(x_prompt, x_sample, norm_mix, w_in, pool_w, pool_scale, conv_w, conv_b, lru_wa, lru_ba, lru_wx, lru_bx,
           lru_lambda, out_norm_pool, out_norm_lru, w_out, norm_ffn, router_w, router_b, w_up, b_up, w_down,
           b_down, norm_final):
    weights = (norm_mix, w_in, pool_w, pool_scale, conv_w, conv_b, lru_wa, lru_ba, lru_wx, lru_bx, lru_lambda,
               out_norm_pool, out_norm_lru, w_out, norm_ffn, router_w, router_b, w_up, b_up, w_down, b_down)
    y_prompt, y_sample = _trunk([x_prompt, x_sample], weights, norm_final, ts=512, tr=512, tm=512, tc=256)
    return (y_prompt, y_sample)
```

```python
import functools

import numpy as np
import jax
import jax.numpy as jnp
from jax import lax
from jax.experimental import pallas as pl
from jax.experimental.pallas import tpu as pltpu

D_MODEL = 1024
D_POOL = 512
D_LRU = 512
D_IN = D_POOL + 2 * D_LRU
POOL_WINDOWS = (2, 4, 8, 16)
POOL_GROUP_W = D_POOL // len(POOL_WINDOWS)
N_LRU_HEADS = 4
LRU_BLOCK = D_LRU // N_LRU_HEADS
LRU_C = 8.0
CONV_W = 4
CONV_LEFT = 1
N_EXPERTS = 32
TOP_K = 4
D_FF = D_MODEL
SWIGLU_LIMIT = 7.0
SWIGLU_ALPHA = 1.702
EPS = 1e-6

LANES = 128
SUBLANES = 8
HALO = 16
VMEM_LIMIT = 48 * 1024 * 1024
K_CHUNKS = 2
HALF = D_MODEL // 2
SCAN_UNROLL = 4
OUT_SLOTS = 3

F32 = jnp.float32
BF16 = jnp.bfloat16


def _rms(x, g):
    return x * lax.rsqrt(jnp.mean(x * x, axis=-1, keepdims=True) + EPS) * g


def _pack_bf16_pairs(lo, hi):
    lo_bits = lax.bitcast_convert_type(lo.astype(BF16).astype(F32), jnp.uint32)
    hi_bits = lax.bitcast_convert_type(hi.astype(BF16).astype(F32), jnp.uint32)
    return lax.shift_right_logical(lo_bits, jnp.uint32(16)) | hi_bits


def _unpack_bf16_pairs(w):
    lo = lax.bitcast_convert_type(lax.shift_left(w, jnp.uint32(16)), F32)
    hi = lax.bitcast_convert_type(w & jnp.uint32(0xFFFF0000), F32)
    return lo, hi


def _shift_rows(v, s):
    n = v.shape[0]
    return pltpu.roll(v, s % n, axis=0)


def _lru_gates(xc, wg_ref, ba_ref, bx_ref, sp_ref, a_scr, b_scr):
    for h in range(N_LRU_HEADS):
        cs = slice(h * LRU_BLOCK, (h + 1) * LRU_BLOCK)
        xh = xc[:, cs]
        gr = jnp.dot(xh.astype(BF16), wg_ref[h], preferred_element_type=F32)
        r = jax.nn.sigmoid(gr[:, :LRU_BLOCK] + ba_ref[:, cs])
        i = jax.nn.sigmoid(gr[:, LRU_BLOCK:] + bx_ref[:, cs])
        log_a = -LRU_C * r * sp_ref[:, cs]
        a = jnp.exp(log_a)
        beta = jnp.sqrt(1.0 - a * a)
        a_scr[:, cs] = a
        b_scr[:, cs] = beta * i * xh


def _scan_rows(a_ref, b_ref, h_ref, carry0, n_rows, reverse):
    row = lax.broadcasted_iota(jnp.int32, (SUBLANES, D_LRU), 0)
    n_grp = n_rows // SUBLANES
    n_iter = n_grp // SCAN_UNROLL

    def prefix(r0):
        a = a_ref[pl.ds(r0, SUBLANES), :]
        b = b_ref[pl.ds(r0, SUBLANES), :]
        for s in (1, 2, 4):
            if reverse:
                ok = row < SUBLANES - s
                a_sh = jnp.where(ok, pltpu.roll(a, SUBLANES - s, axis=0), 1.0)
                b_sh = jnp.where(ok, pltpu.roll(b, SUBLANES - s, axis=0), 0.0)
            else:
                ok = row >= s
                a_sh = jnp.where(ok, pltpu.roll(a, s, axis=0), 1.0)
                b_sh = jnp.where(ok, pltpu.roll(b, s, axis=0), 0.0)
            b = a * b_sh + b
            a = a * a_sh
        return a, b

    def body(k, carry):
        first = (n_iter - 1 - k) if reverse else k
        order = range(SCAN_UNROLL - 1, -1, -1) if reverse else range(SCAN_UNROLL)
        starts = [pl.multiple_of((first * SCAN_UNROLL + u) * SUBLANES, SUBLANES) for u in order]
        parts = [prefix(r0) for r0 in starts]
        for r0, (a, b) in zip(starts, parts):
            h = a * carry + b
            h_ref[pl.ds(r0, SUBLANES), :] = h
            carry = h[0:1, :] if reverse else h[SUBLANES - 1:SUBLANES, :]
        return carry

    return lax.fori_loop(0, n_iter, body, carry0)


def _mix_a_kernel(pos_ref, len_ref,
                  x_ref, xp_ref, xn_ref, nm_ref, win_ref, pw_ref, ps_ref, cw_ref, cb_ref,
                  wg_ref, ba_ref, bx_ref, sp_ref, onp_ref,
                  ypn_ref, hf_ref, gate_ref, xc_ref,
                  h_scr, a_scr, b_scr, carry_ref, *, ts):
    g = pl.program_id(0)
    pos0 = pos_ref[g]
    slen = len_ref[g]
    keep_prev = jnp.where(pos0 == 0, 0.0, 1.0).astype(F32)
    keep_next = jnp.where(pos0 + ts == slen, 0.0, 1.0).astype(F32)
    nm = nm_ref[...]

    h_scr[0:HALO, :] = (_rms(xp_ref[...], nm) * keep_prev).astype(BF16)
    h_scr[HALO:HALO + ts, :] = _rms(x_ref[...], nm).astype(BF16)
    h_scr[HALO + ts:, :] = (_rms(xn_ref[...], nm) * keep_next).astype(BF16)
    z = jnp.dot(h_scr[...], win_ref[...], preferred_element_type=F32)

    trow = pos0 + lax.broadcasted_iota(jnp.int32, (ts, POOL_GROUP_W), 0)
    ys = []
    for gi, win in enumerate(POOL_WINDOWS):
        u = z[:, gi * POOL_GROUP_W:(gi + 1) * POOL_GROUP_W]
        acc = u + _shift_rows(u, 1)
        half = 1
        while 2 * half < win:
            acc = _shift_rows(acc, half) + _shift_rows(acc, -half)
            half *= 2
        half = win // 2
        cnt = (jnp.minimum(trow + half, slen) - jnp.maximum(trow - half, 0)).astype(F32)
        p = acc[HALO:HALO + ts] / cnt - u[HALO:HALO + ts]
        ys.append(jnp.dot(p.astype(BF16), pw_ref[gi], preferred_element_type=F32))
    y_pool = jnp.concatenate(ys, axis=-1) * ps_ref[...]
    ypn_ref[...] = _rms(y_pool, onp_ref[...]).astype(BF16)

    ul = z[:, D_POOL:D_POOL + D_LRU]
    xc_ext = cb_ref[...] + _shift_rows(ul, CONV_LEFT) * cw_ref[0:1, :]
    xc_ext = xc_ext + ul * cw_ref[1:2, :]
    xc_ext = xc_ext + _shift_rows(ul, -1) * cw_ref[2:3, :]
    xc_ext = xc_ext + _shift_rows(ul, -2) * cw_ref[3:4, :]
    xc = xc_ext[HALO:HALO + ts]
    xc_ref[...] = xc

    gate_ref[...] = jax.nn.gelu(z[HALO:HALO + ts, D_POOL + D_LRU:], approximate=True)

    _lru_gates(xc, wg_ref, ba_ref, bx_ref, sp_ref, a_scr, b_scr)
    carry0 = jnp.where(pos0 == 0, 0.0, carry_ref[...])
    carry_ref[...] = _scan_rows(a_scr, b_scr, hf_ref, carry0, ts, reverse=False)


def _mix_a(x, tile_pos, tile_len, p, *, ts):
    n = x.shape[0]
    g_tiles = n // ts
    hb = ts // HALO
    n_hblk = n // HALO
    row = lambda g, *_: (g, 0)
    const2 = lambda g, *_: (0, 0)
    const3 = lambda g, *_: (0, 0, 0)
    grid_spec = pltpu.PrefetchScalarGridSpec(
        num_scalar_prefetch=2,
        grid=(g_tiles,),
        in_specs=[
            pl.BlockSpec((ts, D_MODEL), row),
            pl.BlockSpec((HALO, D_MODEL), lambda g, *_: (jnp.maximum(g * hb - 1, 0), 0)),
            pl.BlockSpec((HALO, D_MODEL), lambda g, *_: (jnp.minimum((g + 1) * hb, n_hblk - 1), 0)),
            pl.BlockSpec((1, D_MODEL), const2),
            pl.BlockSpec((D_MODEL, D_IN), const2),
            pl.BlockSpec((len(POOL_WINDOWS), POOL_GROUP_W, POOL_GROUP_W), const3),
            pl.BlockSpec((1, D_POOL), const2),
            pl.BlockSpec((CONV_W, D_LRU), const2),
            pl.BlockSpec((1, D_LRU), const2),
            pl.BlockSpec((N_LRU_HEADS, LRU_BLOCK, 2 * LRU_BLOCK), const3),
            pl.BlockSpec((1, D_LRU), const2),
            pl.BlockSpec((1, D_LRU), const2),
            pl.BlockSpec((1, D_LRU), const2),
            pl.BlockSpec((1, D_POOL), const2),
        ],
        out_specs=[
            pl.BlockSpec((ts, D_POOL), row),
            pl.BlockSpec((ts, D_LRU), row),
            pl.BlockSpec((ts, D_LRU), row),
            pl.BlockSpec((ts, D_LRU), row),
        ],
        scratch_shapes=[
            pltpu.VMEM((ts + 2 * HALO, D_MODEL), BF16),
            pltpu.VMEM((ts, D_LRU), F32),
            pltpu.VMEM((ts, D_LRU), F32),
            pltpu.VMEM((1, D_LRU), F32),
        ],
    )
    return pl.pallas_call(
        functools.partial(_mix_a_kernel, ts=ts),
        grid_spec=grid_spec,
        out_shape=[
            jax.ShapeDtypeStruct((n, D_POOL), BF16),
            jax.ShapeDtypeStruct((n, D_LRU), F32),
            jax.ShapeDtypeStruct((n, D_LRU), F32),
            jax.ShapeDtypeStruct((n, D_LRU), F32),
        ],
        compiler_params=pltpu.CompilerParams(
            dimension_semantics=("arbitrary",), vmem_limit_bytes=VMEM_LIMIT),
        name="mix_a",
    )(tile_pos, tile_len, x, x, x, p["norm_mix"], p["w_in"], p["pool_w"], p["pool_scale"],
      p["conv_w"], p["conv_b"], p["wg_f"], p["ba_f"], p["bx_f"], p["sp_f"], p["out_norm_pool"])


def _mix_b_kernel(pos_ref, len_ref,
                  x_ref, xc_ref, hf_ref, gate_ref, ypn_ref,
                  wg_ref, ba_ref, bx_ref, sp_ref, onl_ref, wout_ref,
                  o_ref,
                  a_scr, b_scr, hb_scr, carry_ref, *, ts, g_tiles):
    g = g_tiles - 1 - pl.program_id(0)
    at_end = pos_ref[g] + ts == len_ref[g]
    _lru_gates(xc_ref[...], wg_ref, ba_ref, bx_ref, sp_ref, a_scr, b_scr)
    carry0 = jnp.where(at_end, 0.0, carry_ref[...])
    carry_ref[...] = _scan_rows(a_scr, b_scr, hb_scr, carry0, ts, reverse=True)
    y_lru = (hf_ref[...] + hb_scr[...]) * gate_ref[...]
    yln = _rms(y_lru, onl_ref[...]).astype(BF16)
    out = jnp.dot(ypn_ref[...], wout_ref[0:D_POOL, :], preferred_element_type=F32)
    out = out + jnp.dot(yln, wout_ref[D_POOL:, :], preferred_element_type=F32)
    o_ref[...] = x_ref[...] + out


def _mix_b(x, xc, hf, gate, ypn, tile_pos, tile_len, p, *, ts):
    n = x.shape[0]
    g_tiles = n // ts
    row = lambda g, *_: (g_tiles - 1 - g, 0)
    const2 = lambda g, *_: (0, 0)
    const3 = lambda g, *_: (0, 0, 0)
    grid_spec = pltpu.PrefetchScalarGridSpec(
        num_scalar_prefetch=2,
        grid=(g_tiles,),
        in_specs=[
            pl.BlockSpec((ts, D_MODEL), row),
            pl.BlockSpec((ts, D_LRU), row),
            pl.BlockSpec((ts, D_LRU), row),
            pl.BlockSpec((ts, D_LRU), row),
            pl.BlockSpec((ts, D_POOL), row),
            pl.BlockSpec((N_LRU_HEADS, LRU_BLOCK, 2 * LRU_BLOCK), const3),
            pl.BlockSpec((1, D_LRU), const2),
            pl.BlockSpec((1, D_LRU), const2),
            pl.BlockSpec((1, D_LRU), const2),
            pl.BlockSpec((1, D_LRU), const2),
            pl.BlockSpec((D_POOL + D_LRU, D_MODEL), const2),
        ],
        out_specs=pl.BlockSpec((ts, D_MODEL), row),
        scratch_shapes=[
            pltpu.VMEM((ts, D_LRU), F32),
            pltpu.VMEM((ts, D_LRU), F32),
            pltpu.VMEM((ts, D_LRU), F32),
            pltpu.VMEM((1, D_LRU), F32),
        ],
    )
    return pl.pallas_call(
        functools.partial(_mix_b_kernel, ts=ts, g_tiles=g_tiles),
        grid_spec=grid_spec,
        out_shape=jax.ShapeDtypeStruct((n, D_MODEL), F32),
        compiler_params=pltpu.CompilerParams(
            dimension_semantics=("arbitrary",), vmem_limit_bytes=VMEM_LIMIT),
        name="mix_b",
    )(tile_pos, tile_len, x, xc, hf, gate, ypn, p["wg_b"], p["ba_b"], p["bx_b"], p["sp_b"],
      p["out_norm_lru"], p["w_out"])


def _router_kernel(x_ref, nf_ref, rwh_ref, rwl_ref, rb_ref,
                   hf_ref, meta_ref, wts_ref, cnt_ref,
                   carry_ref, *, tr):
    @pl.when(pl.program_id(0) == 0)
    def _():
        carry_ref[...] = jnp.zeros_like(carry_ref)

    hf = _rms(x_ref[...], nf_ref[...])
    hf_ref[...] = hf.reshape(tr, 1, D_MODEL)
    h_hi = hf.astype(BF16)
    h_lo = (hf - h_hi.astype(F32)).astype(BF16)
    logits = (jnp.dot(h_hi, rwh_ref[...], preferred_element_type=F32)
              + jnp.dot(h_lo, rwh_ref[...], preferred_element_type=F32)
              + jnp.dot(h_hi, rwl_ref[...], preferred_element_type=F32)) + rb_ref[...]
    lane = lax.broadcasted_iota(jnp.int32, (tr, LANES), 1).astype(F32)

    vals, idxs = [], []
    l = logits
    for _ in range(TOP_K):
        m = jnp.max(l, axis=-1, keepdims=True)
        ik = jnp.min(jnp.where(l == m, lane, float(LANES)), axis=-1, keepdims=True)
        vals.append(m)
        idxs.append(ik)
        l = jnp.where(lane == ik, -jnp.inf, l)
    es = [jnp.exp(v - vals[0]) for v in vals]
    den = es[0] + es[1] + es[2] + es[3]

    hits = [lane == ik for ik in idxs]
    chosen = jnp.zeros((tr, LANES), F32)
    for hit in hits:
        chosen = chosen + jnp.where(hit, 1.0, 0.0)
    tri = jnp.where(lax.broadcasted_iota(jnp.int32, (tr, tr), 1) < lax.broadcasted_iota(jnp.int32, (tr, tr), 0),
                    1.0, 0.0).astype(BF16)
    before = jnp.dot(tri, chosen.astype(BF16), preferred_element_type=F32) + carry_ref[...]
    total = carry_ref[...] + jnp.sum(chosen, axis=0, keepdims=True)
    carry_ref[...] = total
    cnt_ref[...] = total

    meta = jnp.zeros((tr, LANES), F32)
    wts = jnp.zeros((tr, LANES), F32)
    for k in range(TOP_K):
        pos_k = jnp.sum(jnp.where(hits[k], before, 0.0), axis=-1, keepdims=True)
        meta = jnp.where(lane == float(k), idxs[k], meta)
        meta = jnp.where(lane == float(TOP_K + k), pos_k, meta)
        wts = jnp.where(lane == float(k), es[k] / den, wts)
    meta_ref[...] = meta.astype(jnp.int32)
    wts_ref[...] = wts


def _router(x, p, *, tr):
    n = x.shape[0]
    row = lambda j: (j, 0)
    const2 = lambda j: (0, 0)
    return pl.pallas_call(
        functools.partial(_router_kernel, tr=tr),
        grid=(n // tr,),
        in_specs=[
            pl.BlockSpec((tr, D_MODEL), row),
            pl.BlockSpec((1, D_MODEL), const2),
            pl.BlockSpec((D_MODEL, LANES), const2),
            pl.BlockSpec((D_MODEL, LANES), const2),
            pl.BlockSpec((1, LANES), const2),
        ],
        out_specs=[
            pl.BlockSpec((tr, 1, D_MODEL), lambda j: (j, 0, 0)),
            pl.BlockSpec((tr, LANES), row),
            pl.BlockSpec((tr, LANES), row),
            pl.BlockSpec((1, LANES), const2),
        ],
        out_shape=[
            jax.ShapeDtypeStruct((n, 1, D_MODEL), F32),
            jax.ShapeDtypeStruct((n, LANES), jnp.int32),
            jax.ShapeDtypeStruct((n, LANES), F32),
            jax.ShapeDtypeStruct((1, LANES), F32),
        ],
        scratch_shapes=[pltpu.VMEM((1, LANES), F32)],
        compiler_params=pltpu.CompilerParams(
            dimension_semantics=("arbitrary",), vmem_limit_bytes=VMEM_LIMIT),
        name="router",
    )(x, p["norm_ffn"], p["router_w_hi"], p["router_w_lo"], p["router_b"])


def _moe_kernel(te_ref,
                tok0_ref, tokn_ref, dst_ref, hf_hbm, wup_ref, bup_ref, wdn_ref, bdn_ref,
                ys_hbm,
                xbuf, obuf, x2d, o2d, gsem, ssem, *, tm, nt):
    s = pl.program_id(0)
    slot = lax.rem(s, 2)
    other = 1 - slot
    o_fill = lax.rem(s + (OUT_SLOTS - 1), OUT_SLOTS)
    o_send = lax.rem(s + (OUT_SLOTS - 2), OUT_SLOTS)

    def gather_wait(b):
        pltpu.make_async_copy(hf_hbm.at[pl.ds(0, tm)], xbuf.at[pl.ds(b * tm, tm)], gsem.at[b]).wait()

    def scatter_wait(b):
        pltpu.make_async_copy(obuf.at[pl.ds(b * tm, tm)], ys_hbm.at[pl.ds(0, tm)], ssem.at[b]).wait()

    def gather_row(tok, b, r):
        pltpu.make_async_copy(hf_hbm.at[pl.ds(tok, 1)], xbuf.at[pl.ds(b * tm + r, 1)], gsem.at[b]).start(priority=0)

    def scatter_row(dst, b, r):
        pltpu.make_async_copy(obuf.at[pl.ds(b * tm + r, 1)], ys_hbm.at[pl.ds(dst, 1)], ssem.at[b]).start(priority=1)

    @pl.when(s == 0)
    def _():
        obuf[...] = jnp.zeros_like(obuf)
        o2d[...] = jnp.zeros_like(o2d)

        def body(r, c):
            gather_row(tok0_ref[0, 0, r], 0, r)
            return c
        lax.fori_loop(0, tm, body, 0)

    @pl.when(s >= OUT_SLOTS - 1)
    def _():
        scatter_wait(o_fill)

    gather_wait(slot)

    xrows = xbuf.at[pl.ds(pl.multiple_of(slot * tm, tm), tm)]
    orows = obuf.at[pl.ds(pl.multiple_of(o_fill * tm, tm), tm)]
    orows[...] = _pack_bf16_pairs(o2d[:, :HALF], o2d[:, HALF:]).reshape(tm, 1, HALF)
    kw = D_MODEL // K_CHUNKS
    bounds = [(tm * g) // (K_CHUNKS - 1) for g in range(K_CHUNKS)] + [tm]
    hh = bup_ref[...]
    for kc in range(K_CHUNKS):
        cols = slice(kc * kw, (kc + 1) * kw)
        x2d[:, cols] = xrows[:, :, cols].reshape(tm, kw)
        for r in range(bounds[kc], bounds[kc + 1]):
            gather_row(tokn_ref[0, 0, r], other, r)
            scatter_row(dst_ref[0, 0, r], o_send, r)
        hh = hh + jnp.dot(x2d[:, cols].astype(BF16), wup_ref[cols, :], preferred_element_type=F32)
    x_glu = jnp.minimum(hh[:, :D_FF], SWIGLU_LIMIT)
    x_lin = jnp.clip(hh[:, D_FF:], -SWIGLU_LIMIT, SWIGLU_LIMIT)
    act = (x_glu * jax.nn.sigmoid(SWIGLU_ALPHA * x_glu) * (x_lin + 1.0)).astype(BF16)
    o2d[...] = jnp.dot(act, wdn_ref[...], preferred_element_type=F32) + bdn_ref[...]

    @pl.when(s == nt - 1)
    def _():
        scatter_wait(o_send)
        scatter_wait(lax.rem(s + (OUT_SLOTS - 3), OUT_SLOTS))
        gather_wait(other)


def _moe(hf3, tile_expert, tok3, dst3, p, *, tm, n_out_rows):
    nt = tok3.shape[0]
    smem_blk = lambda fn: pl.BlockSpec((1, 1, tm), fn, memory_space=pltpu.SMEM)
    grid_spec = pltpu.PrefetchScalarGridSpec(
        num_scalar_prefetch=1,
        grid=(nt,),
        in_specs=[
            smem_blk(lambda s, te: (0, 0, 0)),
            smem_blk(lambda s, te: (jnp.minimum(s + 1, nt - 1), 0, 0)),
            smem_blk(lambda s, te: (s, 0, 0)),
            pl.BlockSpec(memory_space=pl.ANY),
            pl.BlockSpec((None, D_MODEL, 2 * D_FF), lambda s, te: (te[s], 0, 0)),
            pl.BlockSpec((None, 1, 2 * D_FF), lambda s, te: (te[s], 0, 0)),
            pl.BlockSpec((None, D_FF, D_MODEL), lambda s, te: (te[s], 0, 0)),
            pl.BlockSpec((None, 1, D_MODEL), lambda s, te: (te[s], 0, 0)),
        ],
        out_specs=pl.BlockSpec(memory_space=pl.ANY),
        scratch_shapes=[
            pltpu.VMEM((2 * tm, 1, D_MODEL), F32),
            pltpu.VMEM((OUT_SLOTS * tm, 1, HALF), jnp.uint32),
            pltpu.VMEM((tm, D_MODEL), F32),
            pltpu.VMEM((tm, D_MODEL), F32),
            pltpu.SemaphoreType.DMA((2,)),
            pltpu.SemaphoreType.DMA((OUT_SLOTS,)),
        ],
    )
    return pl.pallas_call(
        functools.partial(_moe_kernel, tm=tm, nt=nt),
        grid_spec=grid_spec,
        out_shape=jax.ShapeDtypeStruct((n_out_rows, 1, HALF), jnp.uint32),
        compiler_params=pltpu.CompilerParams(
            dimension_semantics=("arbitrary",), vmem_limit_bytes=VMEM_LIMIT),
        name="moe",
    )(tile_expert, tok3, tok3, dst3, hf3, p["w_up"], p["b_up"], p["w_down"], p["b_down"])


def _combine_kernel(x_ref, y0_ref, y1_ref, y2_ref, y3_ref, wts_ref, nf_ref, o_ref, y2d, *, tc, final):
    for k, y_ref in enumerate((y0_ref, y1_ref, y2_ref, y3_ref)):
        y2d[k] = y_ref[...].reshape(tc, HALF)
    w = wts_ref[...]
    acc_lo = acc_hi = None
    for k in range(TOP_K):
        lo, hi = _unpack_bf16_pairs(y2d[k])
        acc_lo = w[:, k:k + 1] * lo if k == 0 else acc_lo + w[:, k:k + 1] * lo
        acc_hi = w[:, k:k + 1] * hi if k == 0 else acc_hi + w[:, k:k + 1] * hi
    x = x_ref[...] + jnp.concatenate([acc_lo, acc_hi], axis=-1)
    o_ref[...] = _rms(x, nf_ref[...]) if final else x


def _combine(x, ys3, wts, norm_final, *, tc, final, row0, n_rows):
    n = x.shape[0]
    b0 = row0 // tc
    kb = n // tc
    row = lambda j: (b0 + j, 0)
    y_spec = lambda k: pl.BlockSpec((tc, 1, HALF), lambda j: (k * kb + b0 + j, 0, 0))
    return pl.pallas_call(
        functools.partial(_combine_kernel, tc=tc, final=final),
        grid=(n_rows // tc,),
        in_specs=[
            pl.BlockSpec((tc, D_MODEL), row),
            y_spec(0), y_spec(1), y_spec(2), y_spec(3),
            pl.BlockSpec((tc, LANES), row),
            pl.BlockSpec((1, D_MODEL), lambda j: (0, 0)),
        ],
        out_specs=pl.BlockSpec((tc, D_MODEL), lambda j: (j, 0)),
        out_shape=jax.ShapeDtypeStruct((n_rows, D_MODEL), F32),
        scratch_shapes=[pltpu.VMEM((TOP_K, tc, HALF), jnp.uint32)],
        compiler_params=pltpu.CompilerParams(
            dimension_semantics=("arbitrary",), vmem_limit_bytes=VMEM_LIMIT),
        name="combine_final" if final else "combine",
    )(x, ys3, ys3, ys3, ys3, wts, norm_final)


def _seq_tables(groups, ts):
    pos, length = [], []
    for n_seq, seq_len in groups:
        assert seq_len % ts == 0
        for _ in range(n_seq):
            for t in range(seq_len // ts):
                pos.append(t * ts)
                length.append(seq_len)
    return np.asarray(pos, np.int32), np.asarray(length, np.int32)


def _routing_plan(meta, counts, *, n, tm, nt):
    idx = meta[:, 0:TOP_K]
    pos = meta[:, TOP_K:2 * TOP_K]
    cnt = counts[0, :N_EXPERTS].astype(jnp.int32)
    tiles_e = (cnt + tm - 1) // tm
    tile_end = jnp.cumsum(tiles_e)
    base = (tile_end - tiles_e) * tm
    slot = (jnp.take(base, idx) + pos).T.reshape(-1)
    tile_ids = jnp.arange(nt, dtype=jnp.int32)
    tile_expert = jnp.minimum(jnp.sum(tile_end[None, :] <= tile_ids[:, None], axis=1), N_EXPERTS - 1).astype(jnp.int32)
    pairs = jnp.arange(TOP_K * n, dtype=jnp.int32)
    rows = jnp.arange((nt + 2) * tm, dtype=jnp.int32)
    spare = TOP_K * n + ((rows // tm) % 2) * tm + rows % tm
    dst = spare.at[slot + 2 * tm].set(pairs, unique_indices=True, mode="promise_in_bounds")
    src = dst[2 * tm:]
    tok = jnp.where(src < TOP_K * n, src % n, 0)
    return tile_expert, tok.reshape(nt, 1, tm), dst[:nt * tm].reshape(nt, 1, tm)


def _layer_params(l, norm_mix, w_in, pool_w, pool_scale, conv_w, conv_b, lru_wa, lru_ba, lru_wx, lru_bx,
                  lru_lambda, out_norm_pool, out_norm_lru, w_out, norm_ffn, router_w, router_b,
                  w_up, b_up, w_down, b_down):
    def gates(d):
        return dict(
            wg=jnp.concatenate([lru_wa[l, d], lru_wx[l, d]], axis=-1).astype(BF16),
            ba=lru_ba[l, d].reshape(1, D_LRU),
            bx=lru_bx[l, d].reshape(1, D_LRU),
            sp=jax.nn.softplus(-lru_lambda[l, d]).reshape(1, D_LRU),
        )
    gf, gb = gates(0), gates(1)
    pad = LANES - N_EXPERTS
    rw = jnp.pad(router_w[l], ((0, 0), (0, pad)))
    rw_hi = rw.astype(BF16)
    return dict(
        norm_mix=norm_mix[l].reshape(1, D_MODEL),
        w_in=w_in[l].astype(BF16),
        pool_w=pool_w[l].astype(BF16),
        pool_scale=pool_scale[l].reshape(1, D_POOL),
        conv_w=conv_w[l],
        conv_b=conv_b[l].reshape(1, D_LRU),
        wg_f=gf["wg"], ba_f=gf["ba"], bx_f=gf["bx"], sp_f=gf["sp"],
        wg_b=gb["wg"], ba_b=gb["ba"], bx_b=gb["bx"], sp_b=gb["sp"],
        out_norm_pool=out_norm_pool[l].reshape(1, D_POOL),
        out_norm_lru=out_norm_lru[l].reshape(1, D_LRU),
        w_out=w_out[l].astype(BF16),
        norm_ffn=norm_ffn[l].reshape(1, D_MODEL),
        router_w_hi=rw_hi,
        router_w_lo=(rw - rw_hi.astype(F32)).astype(BF16),
        router_b=jnp.pad(router_b[l].reshape(1, N_EXPERTS), ((0, 0), (0, pad)), constant_values=-jnp.inf),
        w_up=w_up[l].astype(BF16),
        b_up=b_up[l].reshape(N_EXPERTS, 1, 2 * D_FF),
        w_down=w_down[l].astype(BF16),
        b_down=b_down[l].reshape(N_EXPERTS, 1, D_MODEL),
    )


def _trunk(xs, weights, norm_final, *, ts, tr, tm, tc):
    groups = [(x.shape[0], x.shape[1]) for x in xs]
    sizes = [b * s for b, s in groups]
    n = sum(sizes)
    depth = weights[0].shape[0]
    x = jnp.concatenate([xi.reshape(-1, D_MODEL) for xi in xs], axis=0)
    tile_pos, tile_len = _seq_tables(groups, ts)
    tile_pos, tile_len = jnp.asarray(tile_pos), jnp.asarray(tile_len)
    nt = (TOP_K * n) // tm + N_EXPERTS + 2
    n_out_rows = TOP_K * n + 2 * tm
    nf = norm_final.reshape(1, D_MODEL)
    outs = None
    for l in range(depth):
        p = _layer_params(l, *weights)
        ypn, hf, gate, xc = _mix_a(x, tile_pos, tile_len, p, ts=ts)
        x = _mix_b(x, xc, hf, gate, ypn, tile_pos, tile_len, p, ts=ts)
        hf3, meta, wts, counts = _router(x, p, tr=tr)
        tile_expert, tok3, dst3 = _routing_plan(meta, counts, n=n, tm=tm, nt=nt)
        ys3 = _moe(hf3, tile_expert, tok3, dst3, p, tm=tm, n_out_rows=n_out_rows)
        if l + 1 < depth:
            x = _combine(x, ys3, wts, nf, tc=tc, final=False, row0=0, n_rows=n)
        else:
            outs, row0 = [], 0
            for (b, s), sz in zip(groups, sizes):
                y = _combine(x, ys3, wts, nf, tc=tc, final=True, row0=row0, n_rows=sz)
                outs.append(y.reshape(b, s, D_MODEL))
                row0 += sz
    return outs


def kernel(x_prompt, x_sample, norm_mix, w_in, pool_w, pool_scale, conv_w, conv_b, lru_wa, lru_ba, lru_wx, lru_bx,
           lru_lambda, out_norm_pool, out_norm_lru, w_out, norm_ffn, router_w, router_b, w_up, b_up, w_down,
           b_down, norm_final):
    weights = (norm_mix, w_in, pool_w, pool_scale, conv_w, conv_b, lru_wa, lru_ba, lru_wx, lru_bx, lru_lambda,
               out_norm_pool, out_norm_lru, w_out, norm_ffn, router_w, router_b, w_up, b_up, w_down, b_down)
    y_prompt, y_sample = _trunk([x_prompt, x_sample], weights, norm_final, ts=512, tr=512, tm=512, tc=256)
    return (y_prompt, y_sample)
```

```python
import functools

import numpy as np
import jax
import jax.numpy as jnp
from jax import lax
from jax.experimental import pallas as pl
from jax.experimental.pallas import tpu as pltpu

D_MODEL = 1024
D_POOL = 512
D_LRU = 512
D_IN = D_POOL + 2 * D_LRU
POOL_WINDOWS = (2, 4, 8, 16)
POOL_GROUP_W = D_POOL // len(POOL_WINDOWS)
N_LRU_HEADS = 4
LRU_BLOCK = D_LRU // N_LRU_HEADS
LRU_C = 8.0
CONV_W = 4
CONV_LEFT = 1
N_EXPERTS = 32
TOP_K = 4
D_FF = D_MODEL
SWIGLU_LIMIT = 7.0
SWIGLU_ALPHA = 1.702
EPS = 1e-6

LANES = 128
SUBLANES = 8
HALO = 16
VMEM_LIMIT = 48 * 1024 * 1024
K_CHUNKS = 2
HALF = D_MODEL // 2
SCAN_UNROLL = 4
OUT_SLOTS = 3

F32 = jnp.float32
BF16 = jnp.bfloat16


def _rms(x, g):
    return x * lax.rsqrt(jnp.mean(x * x, axis=-1, keepdims=True) + EPS) * g


def _pack_bf16_pairs(lo, hi):
    lo_bits = lax.bitcast_convert_type(lo.astype(BF16).astype(F32), jnp.uint32)
    hi_bits = lax.bitcast_convert_type(hi.astype(BF16).astype(F32), jnp.uint32)
    return lax.shift_right_logical(lo_bits, jnp.uint32(16)) | hi_bits


def _unpack_bf16_pairs(w):
    lo = lax.bitcast_convert_type(lax.shift_left(w, jnp.uint32(16)), F32)
    hi = lax.bitcast_convert_type(w & jnp.uint32(0xFFFF0000), F32)
    return lo, hi


def _shift_rows(v, s):
    n = v.shape[0]
    return pltpu.roll(v, s % n, axis=0)


def _lru_gates(xc, wg_ref, ba_ref, bx_ref, sp_ref, a_scr, b_scr):
    for h in range(N_LRU_HEADS):
        cs = slice(h * LRU_BLOCK, (h + 1) * LRU_BLOCK)
        xh = xc[:, cs]
        gr = jnp.dot(xh.astype(BF16), wg_ref[h], preferred_element_type=F32)
        r = jax.nn.sigmoid(gr[:, :LRU_BLOCK] + ba_ref[:, cs])
        i = jax.nn.sigmoid(gr[:, LRU_BLOCK:] + bx_ref[:, cs])
        log_a = -LRU_C * r * sp_ref[:, cs]
        a = jnp.exp(log_a)
        beta = jnp.sqrt(1.0 - a * a)
        a_scr[:, cs] = a
        b_scr[:, cs] = beta * i * xh


def _scan_rows(a_ref, b_ref, h_ref, carry0, n_rows, reverse):
    row = lax.broadcasted_iota(jnp.int32, (SUBLANES, D_LRU), 0)
    n_grp = n_rows // SUBLANES
    n_iter = n_grp // SCAN_UNROLL

    def prefix(r0):
        a = a_ref[pl.ds(r0, SUBLANES), :]
        b = b_ref[pl.ds(r0, SUBLANES), :]
        for s in (1, 2, 4):
            if reverse:
                ok = row < SUBLANES - s
                a_sh = jnp.where(ok, pltpu.roll(a, SUBLANES - s, axis=0), 1.0)
                b_sh = jnp.where(ok, pltpu.roll(b, SUBLANES - s, axis=0), 0.0)
            else:
                ok = row >= s
                a_sh = jnp.where(ok, pltpu.roll(a, s, axis=0), 1.0)
                b_sh = jnp.where(ok, pltpu.roll(b, s, axis=0), 0.0)
            b = a * b_sh + b
            a = a * a_sh
        return a, b

    def body(k, carry):
        first = (n_iter - 1 - k) if reverse else k
        order = range(SCAN_UNROLL - 1, -1, -1) if reverse else range(SCAN_UNROLL)
        starts = [pl.multiple_of((first * SCAN_UNROLL + u) * SUBLANES, SUBLANES) for u in order]
        parts = [prefix(r0) for r0 in starts]
        for r0, (a, b) in zip(starts, parts):
            h = a * carry + b
            h_ref[pl.ds(r0, SUBLANES), :] = h
            carry = h[0:1, :] if reverse else h[SUBLANES - 1:SUBLANES, :]
        return carry

    return lax.fori_loop(0, n_iter, body, carry0)


def _mix_a_kernel(pos_ref, len_ref,
                  x_ref, xp_ref, xn_ref, nm_ref, win_ref, pw_ref, ps_ref, cw_ref, cb_ref,
                  wg_ref, ba_ref, bx_ref, sp_ref, onp_ref,
                  ypn_ref, hf_ref, gate_ref, xc_ref,
                  h_scr, a_scr, b_scr, carry_ref, *, ts):
    g = pl.program_id(0)
    pos0 = pos_ref[g]
    slen = len_ref[g]
    keep_prev = jnp.where(pos0 == 0, 0.0, 1.0).astype(F32)
    keep_next = jnp.where(pos0 + ts == slen, 0.0, 1.0).astype(F32)
    nm = nm_ref[...]

    h_scr[0:HALO, :] = (_rms(xp_ref[...], nm) * keep_prev).astype(BF16)
    h_scr[HALO:HALO + ts, :] = _rms(x_ref[...], nm).astype(BF16)
    h_scr[HALO + ts:, :] = (_rms(xn_ref[...], nm) * keep_next).astype(BF16)
    z = jnp.dot(h_scr[...], win_ref[...], preferred_element_type=F32)

    trow = pos0 + lax.broadcasted_iota(jnp.int32, (ts, POOL_GROUP_W), 0)
    ys = []
    for gi, win in enumerate(POOL_WINDOWS):
        u = z[:, gi * POOL_GROUP_W:(gi + 1) * POOL_GROUP_W]
        acc = u + _shift_rows(u, 1)
        half = 1
        while 2 * half < win:
            acc = _shift_rows(acc, half) + _shift_rows(acc, -half)
            half *= 2
        half = win // 2
        cnt = (jnp.minimum(trow + half, slen) - jnp.maximum(trow - half, 0)).astype(F32)
        p = acc[HALO:HALO + ts] / cnt - u[HALO:HALO + ts]
        ys.append(jnp.dot(p.astype(BF16), pw_ref[gi], preferred_element_type=F32))
    y_pool = jnp.concatenate(ys, axis=-1) * ps_ref[...]
    ypn_ref[...] = _rms(y_pool, onp_ref[...]).astype(BF16)

    ul = z[:, D_POOL:D_POOL + D_LRU]
    xc_ext = cb_ref[...] + _shift_rows(ul, CONV_LEFT) * cw_ref[0:1, :]
    xc_ext = xc_ext + ul * cw_ref[1:2, :]
    xc_ext = xc_ext + _shift_rows(ul, -1) * cw_ref[2:3, :]
    xc_ext = xc_ext + _shift_rows(ul, -2) * cw_ref[3:4, :]
    xc = xc_ext[HALO:HALO + ts]
    xc_ref[...] = xc

    gate_ref[...] = jax.nn.gelu(z[HALO:HALO + ts, D_POOL + D_LRU:], approximate=True)

    _lru_gates(xc, wg_ref, ba_ref, bx_ref, sp_ref, a_scr, b_scr)
    carry0 = jnp.where(pos0 == 0, 0.0, carry_ref[...])
    carry_ref[...] = _scan_rows(a_scr, b_scr, hf_ref, carry0, ts, reverse=False)


def _mix_a(x, tile_pos, tile_len, p, *, ts):
    n = x.shape[0]
    g_tiles = n // ts
    hb = ts // HALO
    n_hblk = n // HALO
    row = lambda g, *_: (g, 0)
    const2 = lambda g, *_: (0, 0)
    const3 = lambda g, *_: (0, 0, 0)
    grid_spec = pltpu.PrefetchScalarGridSpec(
        num_scalar_prefetch=2,
        grid=(g_tiles,),
        in_specs=[
            pl.BlockSpec((ts, D_MODEL), row),
            pl.BlockSpec((HALO, D_MODEL), lambda g, *_: (jnp.maximum(g * hb - 1, 0), 0)),
            pl.BlockSpec((HALO, D_MODEL), lambda g, *_: (jnp.minimum((g + 1) * hb, n_hblk - 1), 0)),
            pl.BlockSpec((1, D_MODEL), const2),
            pl.BlockSpec((D_MODEL, D_IN), const2),
            pl.BlockSpec((len(POOL_WINDOWS), POOL_GROUP_W, POOL_GROUP_W), const3),
            pl.BlockSpec((1, D_POOL), const2),
            pl.BlockSpec((CONV_W, D_LRU), const2),
            pl.BlockSpec((1, D_LRU), const2),
            pl.BlockSpec((N_LRU_HEADS, LRU_BLOCK, 2 * LRU_BLOCK), const3),
            pl.BlockSpec((1, D_LRU), const2),
            pl.BlockSpec((1, D_LRU), const2),
            pl.BlockSpec((1, D_LRU), const2),
            pl.BlockSpec((1, D_POOL), const2),
        ],
        out_specs=[
            pl.BlockSpec((ts, D_POOL), row),
            pl.BlockSpec((ts, D_LRU), row),
            pl.BlockSpec((ts, D_LRU), row),
            pl.BlockSpec((ts, D_LRU), row),
        ],
        scratch_shapes=[
            pltpu.VMEM((ts + 2 * HALO, D_MODEL), BF16),
            pltpu.VMEM((ts, D_LRU), F32),
            pltpu.VMEM((ts, D_LRU), F32),
            pltpu.VMEM((1, D_LRU), F32),
        ],
    )
    return pl.pallas_call(
        functools.partial(_mix_a_kernel, ts=ts),
        grid_spec=grid_spec,
        out_shape=[
            jax.ShapeDtypeStruct((n, D_POOL), BF16),
            jax.ShapeDtypeStruct((n, D_LRU), F32),
            jax.ShapeDtypeStruct((n, D_LRU), F32),
            jax.ShapeDtypeStruct((n, D_LRU), F32),
        ],
        compiler_params=pltpu.CompilerParams(
            dimension_semantics=("arbitrary",), vmem_limit_bytes=VMEM_LIMIT),
        name="mix_a",
    )(tile_pos, tile_len, x, x, x, p["norm_mix"], p["w_in"], p["pool_w"], p["pool_scale"],
      p["conv_w"], p["conv_b"], p["wg_f"], p["ba_f"], p["bx_f"], p["sp_f"], p["out_norm_pool"])


def _mix_b_kernel(pos_ref, len_ref,
                  x_ref, xc_ref, hf_ref, gate_ref, ypn_ref,
                  wg_ref, ba_ref, bx_ref, sp_ref, onl_ref, wout_ref,
                  o_ref,
                  a_scr, b_scr, hb_scr, carry_ref, *, ts, g_tiles):
    g = g_tiles - 1 - pl.program_id(0)
    at_end = pos_ref[g] + ts == len_ref[g]
    _lru_gates(xc_ref[...], wg_ref, ba_ref, bx_ref, sp_ref, a_scr, b_scr)
    carry0 = jnp.where(at_end, 0.0, carry_ref[...])
    carry_ref[...] = _scan_rows(a_scr, b_scr, hb_scr, carry0, ts, reverse=True)
    y_lru = (hf_ref[...] + hb_scr[...]) * gate_ref[...]
    yln = _rms(y_lru, onl_ref[...]).astype(BF16)
    out = jnp.dot(ypn_ref[...], wout_ref[0:D_POOL, :], preferred_element_type=F32)
    out = out + jnp.dot(yln, wout_ref[D_POOL:, :], preferred_element_type=F32)
    o_ref[...] = x_ref[...] + out


def _mix_b(x, xc, hf, gate, ypn, tile_pos, tile_len, p, *, ts):
    n = x.shape[0]
    g_tiles = n // ts
    row = lambda g, *_: (g_tiles - 1 - g, 0)
    const2 = lambda g, *_: (0, 0)
    const3 = lambda g, *_: (0, 0, 0)
    grid_spec = pltpu.PrefetchScalarGridSpec(
        num_scalar_prefetch=2,
        grid=(g_tiles,),
        in_specs=[
            pl.BlockSpec((ts, D_MODEL), row),
            pl.BlockSpec((ts, D_LRU), row),
            pl.BlockSpec((ts, D_LRU), row),
            pl.BlockSpec((ts, D_LRU), row),
            pl.BlockSpec((ts, D_POOL), row),
            pl.BlockSpec((N_LRU_HEADS, LRU_BLOCK, 2 * LRU_BLOCK), const3),
            pl.BlockSpec((1, D_LRU), const2),
            pl.BlockSpec((1, D_LRU), const2),
            pl.BlockSpec((1, D_LRU), const2),
            pl.BlockSpec((1, D_LRU), const2),
            pl.BlockSpec((D_POOL + D_LRU, D_MODEL), const2),
        ],
        out_specs=pl.BlockSpec((ts, D_MODEL), row),
        scratch_shapes=[
            pltpu.VMEM((ts, D_LRU), F32),
            pltpu.VMEM((ts, D_LRU), F32),
            pltpu.VMEM((ts, D_LRU), F32),
            pltpu.VMEM((1, D_LRU), F32),
        ],
    )
    return pl.pallas_call(
        functools.partial(_mix_b_kernel, ts=ts, g_tiles=g_tiles),
        grid_spec=grid_spec,
        out_shape=jax.ShapeDtypeStruct((n, D_MODEL), F32),
        compiler_params=pltpu.CompilerParams(
            dimension_semantics=("arbitrary",), vmem_limit_bytes=VMEM_LIMIT),
        name="mix_b",
    )(tile_pos, tile_len, x, xc, hf, gate, ypn, p["wg_b"], p["ba_b"], p["bx_b"], p["sp_b"],
      p["out_norm_lru"], p["w_out"])


def _router_kernel(x_ref, nf_ref, rwh_ref, rwl_ref, rb_ref,
                   hf_ref, meta_ref, wts_ref, cnt_ref,
                   carry_ref, *, tr):
    @pl.when(pl.program_id(0) == 0)
    def _():
        carry_ref[...] = jnp.zeros_like(carry_ref)

    hf = _rms(x_ref[...], nf_ref[...])
    hf_ref[...] = hf.reshape(tr, 1, D_MODEL)
    h_hi = hf.astype(BF16)
    h_lo = (hf - h_hi.astype(F32)).astype(BF16)
    logits = (jnp.dot(h_hi, rwh_ref[...], preferred_element_type=F32)
              + jnp.dot(h_lo, rwh_ref[...], preferred_element_type=F32)
              + jnp.dot(h_hi, rwl_ref[...], preferred_element_type=F32)) + rb_ref[...]
    lane = lax.broadcasted_iota(jnp.int32, (tr, LANES), 1).astype(F32)

    vals, idxs = [], []
    l = logits
    for _ in range(TOP_K):
        m = jnp.max(l, axis=-1, keepdims=True)
        ik = jnp.min(jnp.where(l == m, lane, float(LANES)), axis=-1, keepdims=True)
        vals.append(m)
        idxs.append(ik)
        l = jnp.where(lane == ik, -jnp.inf, l)
    es = [jnp.exp(v - vals[0]) for v in vals]
    den = es[0] + es[1] + es[2] + es[3]

    hits = [lane == ik for ik in idxs]
    chosen = jnp.zeros((tr, LANES), F32)
    for hit in hits:
        chosen = chosen + jnp.where(hit, 1.0, 0.0)
    tri = jnp.where(lax.broadcasted_iota(jnp.int32, (tr, tr), 1) < lax.broadcasted_iota(jnp.int32, (tr, tr), 0),
                    1.0, 0.0).astype(BF16)
    before = jnp.dot(tri, chosen.astype(BF16), preferred_element_type=F32) + carry_ref[...]
    total = carry_ref[...] + jnp.sum(chosen, axis=0, keepdims=True)
    carry_ref[...] = total
    cnt_ref[...] = total

    meta = jnp.zeros((tr, LANES), F32)
    wts = jnp.zeros((tr, LANES), F32)
    for k in range(TOP_K):
        pos_k = jnp.sum(jnp.where(hits[k], before, 0.0), axis=-1, keepdims=True)
        meta = jnp.where(lane == float(k), idxs[k], meta)
        meta = jnp.where(lane == float(TOP_K + k), pos_k, meta)
        wts = jnp.where(lane == float(k), es[k] / den, wts)
    meta_ref[...] = jnp.transpose(meta)[0:2 * TOP_K, :].astype(jnp.int32)
    wts_ref[...] = wts


def _router(x, p, *, tr):
    n = x.shape[0]
    row = lambda j: (j, 0)
    const2 = lambda j: (0, 0)
    return pl.pallas_call(
        functools.partial(_router_kernel, tr=tr),
        grid=(n // tr,),
        in_specs=[
            pl.BlockSpec((tr, D_MODEL), row),
            pl.BlockSpec((1, D_MODEL), const2),
            pl.BlockSpec((D_MODEL, LANES), const2),
            pl.BlockSpec((D_MODEL, LANES), const2),
            pl.BlockSpec((1, LANES), const2),
        ],
        out_specs=[
            pl.BlockSpec((tr, 1, D_MODEL), lambda j: (j, 0, 0)),
            pl.BlockSpec((2 * TOP_K, tr), lambda j: (0, j)),
            pl.BlockSpec((tr, LANES), row),
            pl.BlockSpec((1, LANES), const2),
        ],
        out_shape=[
            jax.ShapeDtypeStruct((n, 1, D_MODEL), F32),
            jax.ShapeDtypeStruct((2 * TOP_K, n), jnp.int32),
            jax.ShapeDtypeStruct((n, LANES), F32),
            jax.ShapeDtypeStruct((1, LANES), F32),
        ],
        scratch_shapes=[pltpu.VMEM((1, LANES), F32)],
        compiler_params=pltpu.CompilerParams(
            dimension_semantics=("arbitrary",), vmem_limit_bytes=VMEM_LIMIT),
        name="router",
    )(x, p["norm_ffn"], p["router_w_hi"], p["router_w_lo"], p["router_b"])


def _moe_kernel(te_ref,
                tok0_ref, tokn_ref, dst_ref, hf_hbm, wup_ref, bup_ref, wdn_ref, bdn_ref,
                ys_hbm,
                xbuf, obuf, x2d, o2d, gsem, ssem, *, tm, nt):
    s = pl.program_id(0)
    slot = lax.rem(s, 2)
    other = 1 - slot
    o_fill = lax.rem(s + (OUT_SLOTS - 1), OUT_SLOTS)
    o_send = lax.rem(s + (OUT_SLOTS - 2), OUT_SLOTS)

    def gather_wait(b):
        pltpu.make_async_copy(hf_hbm.at[pl.ds(0, tm)], xbuf.at[pl.ds(b * tm, tm)], gsem.at[b]).wait()

    def scatter_wait(b):
        pltpu.make_async_copy(obuf.at[pl.ds(b * tm, tm)], ys_hbm.at[pl.ds(0, tm)], ssem.at[b]).wait()

    def gather_row(tok, b, r, prio=0):
        pltpu.make_async_copy(hf_hbm.at[pl.ds(tok, 1)], xbuf.at[pl.ds(b * tm + r, 1)], gsem.at[b]).start(priority=prio)

    def scatter_row(dst, b, r, prio=0):
        pltpu.make_async_copy(obuf.at[pl.ds(b * tm + r, 1)], ys_hbm.at[pl.ds(dst, 1)], ssem.at[b]).start(priority=prio)

    @pl.when(s == 0)
    def _():
        obuf[...] = jnp.zeros_like(obuf)
        o2d[...] = jnp.zeros_like(o2d)

        def body(r, c):
            gather_row(tok0_ref[0, 0, r], 0, r)
            return c
        lax.fori_loop(0, tm, body, 0)

    @pl.when(s >= OUT_SLOTS - 1)
    def _():
        scatter_wait(o_fill)

    gather_wait(slot)

    xrows = xbuf.at[pl.ds(pl.multiple_of(slot * tm, tm), tm)]
    orows = obuf.at[pl.ds(pl.multiple_of(o_fill * tm, tm), tm)]
    orows[...] = _pack_bf16_pairs(o2d[:, :HALF], o2d[:, HALF:]).reshape(tm, 1, HALF)
    kw = D_MODEL // K_CHUNKS
    bounds = [(tm * g) // (K_CHUNKS - 1) for g in range(K_CHUNKS)] + [tm]
    hh = bup_ref[...]
    for kc in range(K_CHUNKS):
        cols = slice(kc * kw, (kc + 1) * kw)
        x2d[:, cols] = xrows[:, :, cols].reshape(tm, kw)
        for r in range(bounds[kc], bounds[kc + 1]):
            gather_row(tokn_ref[0, 0, r], other, r, r % 2)
            scatter_row(dst_ref[0, 0, r], o_send, r, (r + 1) % 2)
        hh = hh + jnp.dot(x2d[:, cols].astype(BF16), wup_ref[cols, :], preferred_element_type=F32)
    x_glu = jnp.minimum(hh[:, :D_FF], SWIGLU_LIMIT)
    x_lin = jnp.clip(hh[:, D_FF:], -SWIGLU_LIMIT, SWIGLU_LIMIT)
    act = (x_glu * jax.nn.sigmoid(SWIGLU_ALPHA * x_glu) * (x_lin + 1.0)).astype(BF16)
    o2d[...] = jnp.dot(act, wdn_ref[...], preferred_element_type=F32) + bdn_ref[...]

    @pl.when(s == nt - 1)
    def _():
        scatter_wait(o_send)
        scatter_wait(lax.rem(s + (OUT_SLOTS - 3), OUT_SLOTS))
        gather_wait(other)


def _moe(hf3, tile_expert, tok3, dst3, p, *, tm, n_out_rows):
    nt = tok3.shape[0]
    smem_blk = lambda fn: pl.BlockSpec((1, 1, tm), fn, memory_space=pltpu.SMEM)
    grid_spec = pltpu.PrefetchScalarGridSpec(
        num_scalar_prefetch=1,
        grid=(nt,),
        in_specs=[
            smem_blk(lambda s, te: (0, 0, 0)),
            smem_blk(lambda s, te: (jnp.minimum(s + 1, nt - 1), 0, 0)),
            smem_blk(lambda s, te: (s, 0, 0)),
            pl.BlockSpec(memory_space=pl.ANY),
            pl.BlockSpec((None, D_MODEL, 2 * D_FF), lambda s, te: (te[s], 0, 0)),
            pl.BlockSpec((None, 1, 2 * D_FF), lambda s, te: (te[s], 0, 0)),
            pl.BlockSpec((None, D_FF, D_MODEL), lambda s, te: (te[s], 0, 0)),
            pl.BlockSpec((None, 1, D_MODEL), lambda s, te: (te[s], 0, 0)),
        ],
        out_specs=pl.BlockSpec(memory_space=pl.ANY),
        scratch_shapes=[
            pltpu.VMEM((2 * tm, 1, D_MODEL), F32),
            pltpu.VMEM((OUT_SLOTS * tm, 1, HALF), jnp.uint32),
            pltpu.VMEM((tm, D_MODEL), F32),
            pltpu.VMEM((tm, D_MODEL), F32),
            pltpu.SemaphoreType.DMA((2,)),
            pltpu.SemaphoreType.DMA((OUT_SLOTS,)),
        ],
    )
    return pl.pallas_call(
        functools.partial(_moe_kernel, tm=tm, nt=nt),
        grid_spec=grid_spec,
        out_shape=jax.ShapeDtypeStruct((n_out_rows, 1, HALF), jnp.uint32),
        compiler_params=pltpu.CompilerParams(
            dimension_semantics=("arbitrary",), vmem_limit_bytes=VMEM_LIMIT),
        name="moe",
    )(tile_expert, tok3, tok3, dst3, hf3, p["w_up"], p["b_up"], p["w_down"], p["b_down"])


def _combine_kernel(x_ref, y0_ref, y1_ref, y2_ref, y3_ref, wts_ref, nf_ref, o_ref, y2d, *, tc, final):
    for k, y_ref in enumerate((y0_ref, y1_ref, y2_ref, y3_ref)):
        y2d[k] = y_ref[...].reshape(tc, HALF)
    w = wts_ref[...]
    acc_lo = acc_hi = None
    for k in range(TOP_K):
        lo, hi = _unpack_bf16_pairs(y2d[k])
        acc_lo = w[:, k:k + 1] * lo if k == 0 else acc_lo + w[:, k:k + 1] * lo
        acc_hi = w[:, k:k + 1] * hi if k == 0 else acc_hi + w[:, k:k + 1] * hi
    x = x_ref[...] + jnp.concatenate([acc_lo, acc_hi], axis=-1)
    o_ref[...] = _rms(x, nf_ref[...]) if final else x


def _combine(x, ys3, wts, norm_final, *, tc, final, row0, n_rows):
    n = x.shape[0]
    b0 = row0 // tc
    kb = n // tc
    row = lambda j: (b0 + j, 0)
    y_spec = lambda k: pl.BlockSpec((tc, 1, HALF), lambda j: (k * kb + b0 + j, 0, 0))
    return pl.pallas_call(
        functools.partial(_combine_kernel, tc=tc, final=final),
        grid=(n_rows // tc,),
        in_specs=[
            pl.BlockSpec((tc, D_MODEL), row),
            y_spec(0), y_spec(1), y_spec(2), y_spec(3),
            pl.BlockSpec((tc, LANES), row),
            pl.BlockSpec((1, D_MODEL), lambda j: (0, 0)),
        ],
        out_specs=pl.BlockSpec((tc, D_MODEL), lambda j: (j, 0)),
        out_shape=jax.ShapeDtypeStruct((n_rows, D_MODEL), F32),
        scratch_shapes=[pltpu.VMEM((TOP_K, tc, HALF), jnp.uint32)],
        compiler_params=pltpu.CompilerParams(
            dimension_semantics=("arbitrary",), vmem_limit_bytes=VMEM_LIMIT),
        name="combine_final" if final else "combine",
    )(x, ys3, ys3, ys3, ys3, wts, norm_final)


def _seq_tables(groups, ts):
    pos, length = [], []
    for n_seq, seq_len in groups:
        assert seq_len % ts == 0
        for _ in range(n_seq):
            for t in range(seq_len // ts):
                pos.append(t * ts)
                length.append(seq_len)
    return np.asarray(pos, np.int32), np.asarray(length, np.int32)


def _routing_plan(meta, counts, *, n, tm, nt):
    idx = meta[0:TOP_K]
    pos = meta[TOP_K:2 * TOP_K]
    cnt = counts[0, :N_EXPERTS].astype(jnp.int32)
    tiles_e = (cnt + tm - 1) // tm
    tile_end = jnp.cumsum(tiles_e)
    base = (tile_end - tiles_e) * tm
    base_of = jnp.zeros_like(idx)
    for e in range(N_EXPERTS):
        base_of = jnp.where(idx == e, base[e], base_of)
    slot = (base_of + pos).reshape(-1)
    tile_ids = jnp.arange(nt, dtype=jnp.int32)
    tile_expert = jnp.minimum(jnp.sum(tile_end[None, :] <= tile_ids[:, None], axis=1), N_EXPERTS - 1).astype(jnp.int32)
    pairs = jnp.arange(TOP_K * n, dtype=jnp.int32)
    rows = jnp.arange((nt + 2) * tm, dtype=jnp.int32)
    spare = TOP_K * n + ((rows // tm) % 2) * tm + rows % tm
    dst = spare.at[slot + 2 * tm].set(pairs, unique_indices=True, mode="promise_in_bounds")
    src = dst[2 * tm:]
    tok = jnp.where(src < TOP_K * n, src % n, 0)
    return tile_expert, tok.reshape(nt, 1, tm), dst[:nt * tm].reshape(nt, 1, tm)


def _layer_params(l, norm_mix, w_in, pool_w, pool_scale, conv_w, conv_b, lru_wa, lru_ba, lru_wx, lru_bx,
                  lru_lambda, out_norm_pool, out_norm_lru, w_out, norm_ffn, router_w, router_b,
                  w_up, b_up, w_down, b_down):
    def gates(d):
        return dict(
            wg=jnp.concatenate([lru_wa[l, d], lru_wx[l, d]], axis=-1).astype(BF16),
            ba=lru_ba[l, d].reshape(1, D_LRU),
            bx=lru_bx[l, d].reshape(1, D_LRU),
            sp=jax.nn.softplus(-lru_lambda[l, d]).reshape(1, D_LRU),
        )
    gf, gb = gates(0), gates(1)
    pad = LANES - N_EXPERTS
    rw = jnp.pad(router_w[l], ((0, 0), (0, pad)))
    rw_hi = rw.astype(BF16)
    return dict(
        norm_mix=norm_mix[l].reshape(1, D_MODEL),
        w_in=w_in[l].astype(BF16),
        pool_w=pool_w[l].astype(BF16),
        pool_scale=pool_scale[l].reshape(1, D_POOL),
        conv_w=conv_w[l],
        conv_b=conv_b[l].reshape(1, D_LRU),
        wg_f=gf["wg"], ba_f=gf["ba"], bx_f=gf["bx"], sp_f=gf["sp"],
        wg_b=gb["wg"], ba_b=gb["ba"], bx_b=gb["bx"], sp_b=gb["sp"],
        out_norm_pool=out_norm_pool[l].reshape(1, D_POOL),
        out_norm_lru=out_norm_lru[l].reshape(1, D_LRU),
        w_out=w_out[l].astype(BF16),
        norm_ffn=norm_ffn[l].reshape(1, D_MODEL),
        router_w_hi=rw_hi,
        router_w_lo=(rw - rw_hi.astype(F32)).astype(BF16),
        router_b=jnp.pad(router_b[l].reshape(1, N_EXPERTS), ((0, 0), (0, pad)), constant_values=-jnp.inf),
        w_up=w_up[l].astype(BF16),
        b_up=b_up[l].reshape(N_EXPERTS, 1, 2 * D_FF),
        w_down=w_down[l].astype(BF16),
        b_down=b_down[l].reshape(N_EXPERTS, 1, D_MODEL),
    )


def _trunk(xs, weights, norm_final, *, ts, tr, tm, tc):
    groups = [(x.shape[0], x.shape[1]) for x in xs]
    sizes = [b * s for b, s in groups]
    n = sum(sizes)
    depth = weights[0].shape[0]
    x = jnp.concatenate([xi.reshape(-1, D_MODEL) for xi in xs], axis=0)
    tile_pos, tile_len = _seq_tables(groups, ts)
    tile_pos, tile_len = jnp.asarray(tile_pos), jnp.asarray(tile_len)
    nt = (TOP_K * n) // tm + N_EXPERTS + 2
    n_out_rows = TOP_K * n + 2 * tm
    nf = norm_final.reshape(1, D_MODEL)
    outs = None
    for l in range(depth):
        p = _layer_params(l, *weights)
        ypn, hf, gate, xc = _mix_a(x, tile_pos, tile_len, p, ts=ts)
        x = _mix_b(x, xc, hf, gate, ypn, tile_pos, tile_len, p, ts=ts)
        hf3, meta, wts, counts = _router(x, p, tr=tr)
        tile_expert, tok3, dst3 = _routing_plan(meta, counts, n=n, tm=tm, nt=nt)
        ys3 = _moe(hf3, tile_expert, tok3, dst3, p, tm=tm, n_out_rows=n_out_rows)
        if l + 1 < depth:
            x = _combine(x, ys3, wts, nf, tc=tc, final=False, row0=0, n_rows=n)
        else:
            outs, row0 = [], 0
            for (b, s), sz in zip(groups, sizes):
                y = _combine(x, ys3, wts, nf, tc=tc, final=True, row0=row0, n_rows=sz)
                outs.append(y.reshape(b, s, D_MODEL))
                row0 += sz
    return outs


def kernel(x_prompt, x_sample, norm_mix, w_in, pool_w, pool_scale, conv_w, conv_b, lru_wa, lru_ba, lru_wx, lru_bx,
           lru_lambda, out_norm_pool, out_norm_lru, w_out, norm_ffn, router_w, router_b, w_up, b_up, w_down,
           b_down, norm_final):
    weights = (norm_mix, w_in, pool_w, pool_scale, conv_w, conv_b, lru_wa, lru_ba, lru_wx, lru_bx, lru_lambda,
               out_norm_pool, out_norm_lru, w_out, norm_ffn, router_w, router_b, w_up, b_up, w_down, b_down)
    y_prompt, y_sample = _trunk([x_prompt, x_sample], weights, norm_final, ts=512, tr=512, tm=512, tc=256)
    return (y_prompt, y_sample)
```

```python
import functools

import numpy as np
import jax
import jax.numpy as jnp
from jax import lax
from jax.experimental import pallas as pl
from jax.experimental.pallas import tpu as pltpu

D_MODEL = 1024
D_POOL = 512
D_LRU = 512
D_IN = D_POOL + 2 * D_LRU
POOL_WINDOWS = (2, 4, 8, 16)
POOL_GROUP_W = D_POOL // len(POOL_WINDOWS)
N_LRU_HEADS = 4
LRU_BLOCK = D_LRU // N_LRU_HEADS
LRU_C = 8.0
CONV_W = 4
CONV_LEFT = 1
N_EXPERTS = 32
TOP_K = 4
D_FF = D_MODEL
SWIGLU_LIMIT = 7.0
SWIGLU_ALPHA = 1.702
EPS = 1e-6

LANES = 128
SUBLANES = 8
HALO = 16
VMEM_LIMIT = 48 * 1024 * 1024
K_CHUNKS = 2
HALF = D_MODEL // 2
SCAN_UNROLL = 4
OUT_SLOTS = 3

F32 = jnp.float32
BF16 = jnp.bfloat16


def _rms(x, g):
    return x * lax.rsqrt(jnp.mean(x * x, axis=-1, keepdims=True) + EPS) * g


def _pack_bf16_pairs(lo, hi):
    lo_bits = lax.bitcast_convert_type(lo.astype(BF16).astype(F32), jnp.uint32)
    hi_bits = lax.bitcast_convert_type(hi.astype(BF16).astype(F32), jnp.uint32)
    return lax.shift_right_logical(lo_bits, jnp.uint32(16)) | hi_bits


def _unpack_bf16_pairs(w):
    lo = lax.bitcast_convert_type(lax.shift_left(w, jnp.uint32(16)), F32)
    hi = lax.bitcast_convert_type(w & jnp.uint32(0xFFFF0000), F32)
    return lo, hi


def _shift_rows(v, s):
    n = v.shape[0]
    return pltpu.roll(v, s % n, axis=0)


def _lru_gates(xc, wg_ref, ba_ref, bx_ref, sp_ref, a_scr, b_scr):
    for h in range(N_LRU_HEADS):
        cs = slice(h * LRU_BLOCK, (h + 1) * LRU_BLOCK)
        xh = xc[:, cs]
        gr = jnp.dot(xh.astype(BF16), wg_ref[h], preferred_element_type=F32)
        r = jax.nn.sigmoid(gr[:, :LRU_BLOCK] + ba_ref[:, cs])
        i = jax.nn.sigmoid(gr[:, LRU_BLOCK:] + bx_ref[:, cs])
        log_a = -LRU_C * r * sp_ref[:, cs]
        a = jnp.exp(log_a)
        beta = jnp.sqrt(1.0 - a * a)
        a_scr[:, cs] = a
        b_scr[:, cs] = beta * i * xh


def _scan_rows(a_ref, b_ref, h_ref, carry0, n_rows, reverse):
    row = lax.broadcasted_iota(jnp.int32, (SUBLANES, D_LRU), 0)
    n_grp = n_rows // SUBLANES
    n_iter = n_grp // SCAN_UNROLL

    def prefix(r0):
        a = a_ref[pl.ds(r0, SUBLANES), :]
        b = b_ref[pl.ds(r0, SUBLANES), :]
        for s in (1, 2, 4):
            if reverse:
                ok = row < SUBLANES - s
                a_sh = jnp.where(ok, pltpu.roll(a, SUBLANES - s, axis=0), 1.0)
                b_sh = jnp.where(ok, pltpu.roll(b, SUBLANES - s, axis=0), 0.0)
            else:
                ok = row >= s
                a_sh = jnp.where(ok, pltpu.roll(a, s, axis=0), 1.0)
                b_sh = jnp.where(ok, pltpu.roll(b, s, axis=0), 0.0)
            b = a * b_sh + b
            a = a * a_sh
        return a, b

    def body(k, carry):
        first = (n_iter - 1 - k) if reverse else k
        order = range(SCAN_UNROLL - 1, -1, -1) if reverse else range(SCAN_UNROLL)
        starts = [pl.multiple_of((first * SCAN_UNROLL + u) * SUBLANES, SUBLANES) for u in order]
        parts = [prefix(r0) for r0 in starts]
        for r0, (a, b) in zip(starts, parts):
            h = a * carry + b
            h_ref[pl.ds(r0, SUBLANES), :] = h
            carry = h[0:1, :] if reverse else h[SUBLANES - 1:SUBLANES, :]
        return carry

    return lax.fori_loop(0, n_iter, body, carry0)


def _pick_source(g, starts, refs):
    v = refs[0][...]
    for start, ref in zip(starts[1:], refs[1:]):
        v = jnp.where(g >= start, ref[...], v)
    return v


def _mix_a_kernel(pos_ref, len_ref, *refs, ts, starts):
    n_src = len(starts)
    x_refs, xp_refs, xn_refs = refs[0:n_src], refs[n_src:2 * n_src], refs[2 * n_src:3 * n_src]
    (nm_ref, win_ref, pw_ref, ps_ref, cw_ref, cb_ref, wg_ref, ba_ref, bx_ref, sp_ref, onp_ref,
     ypn_ref, hf_ref, gate_ref, xc_ref, h_scr, a_scr, b_scr, carry_ref) = refs[3 * n_src:]
    g = pl.program_id(0)
    pos0 = pos_ref[g]
    slen = len_ref[g]
    keep_prev = jnp.where(pos0 == 0, 0.0, 1.0).astype(F32)
    keep_next = jnp.where(pos0 + ts == slen, 0.0, 1.0).astype(F32)
    nm = nm_ref[...]

    h_scr[0:HALO, :] = (_rms(_pick_source(g, starts, xp_refs), nm) * keep_prev).astype(BF16)
    h_scr[HALO:HALO + ts, :] = _rms(_pick_source(g, starts, x_refs), nm).astype(BF16)
    h_scr[HALO + ts:, :] = (_rms(_pick_source(g, starts, xn_refs), nm) * keep_next).astype(BF16)
    z = jnp.dot(h_scr[...], win_ref[...], preferred_element_type=F32)

    trow = pos0 + lax.broadcasted_iota(jnp.int32, (ts, POOL_GROUP_W), 0)
    ys = []
    for gi, win in enumerate(POOL_WINDOWS):
        u = z[:, gi * POOL_GROUP_W:(gi + 1) * POOL_GROUP_W]
        acc = u + _shift_rows(u, 1)
        half = 1
        while 2 * half < win:
            acc = _shift_rows(acc, half) + _shift_rows(acc, -half)
            half *= 2
        half = win // 2
        cnt = (jnp.minimum(trow + half, slen) - jnp.maximum(trow - half, 0)).astype(F32)
        p = acc[HALO:HALO + ts] / cnt - u[HALO:HALO + ts]
        ys.append(jnp.dot(p.astype(BF16), pw_ref[gi], preferred_element_type=F32))
    y_pool = jnp.concatenate(ys, axis=-1) * ps_ref[...]
    ypn_ref[...] = _rms(y_pool, onp_ref[...]).astype(BF16)

    ul = z[:, D_POOL:D_POOL + D_LRU]
    xc_ext = cb_ref[...] + _shift_rows(ul, CONV_LEFT) * cw_ref[0:1, :]
    xc_ext = xc_ext + ul * cw_ref[1:2, :]
    xc_ext = xc_ext + _shift_rows(ul, -1) * cw_ref[2:3, :]
    xc_ext = xc_ext + _shift_rows(ul, -2) * cw_ref[3:4, :]
    xc = xc_ext[HALO:HALO + ts]
    xc_ref[...] = xc

    gate_ref[...] = jax.nn.gelu(z[HALO:HALO + ts, D_POOL + D_LRU:], approximate=True)

    _lru_gates(xc, wg_ref, ba_ref, bx_ref, sp_ref, a_scr, b_scr)
    carry0 = jnp.where(pos0 == 0, 0.0, carry_ref[...])
    carry_ref[...] = _scan_rows(a_scr, b_scr, hf_ref, carry0, ts, reverse=False)


def _source_starts(xs, ts):
    starts, at = [], 0
    for x in xs:
        starts.append(at)
        at += x.shape[0] // ts
    return tuple(starts), at


def _mix_a(xs, tile_pos, tile_len, p, *, ts):
    starts, g_tiles = _source_starts(xs, ts)
    n = g_tiles * ts
    hb = ts // HALO
    row = lambda g, *_: (g, 0)
    const2 = lambda g, *_: (0, 0)
    const3 = lambda g, *_: (0, 0, 0)

    def cur(start, x):
        return pl.BlockSpec((ts, D_MODEL), lambda g, *_: (jnp.clip(g - start, 0, x.shape[0] // ts - 1), 0))

    def prev(start, x):
        return pl.BlockSpec((HALO, D_MODEL),
                            lambda g, *_: (jnp.clip((g - start) * hb - 1, 0, x.shape[0] // HALO - 1), 0))

    def nxt(start, x):
        return pl.BlockSpec((HALO, D_MODEL),
                            lambda g, *_: (jnp.clip((g - start + 1) * hb, 0, x.shape[0] // HALO - 1), 0))

    grid_spec = pltpu.PrefetchScalarGridSpec(
        num_scalar_prefetch=2,
        grid=(g_tiles,),
        in_specs=[
            *[cur(st, x) for st, x in zip(starts, xs)],
            *[prev(st, x) for st, x in zip(starts, xs)],
            *[nxt(st, x) for st, x in zip(starts, xs)],
            pl.BlockSpec((1, D_MODEL), const2),
            pl.BlockSpec((D_MODEL, D_IN), const2),
            pl.BlockSpec((len(POOL_WINDOWS), POOL_GROUP_W, POOL_GROUP_W), const3),
            pl.BlockSpec((1, D_POOL), const2),
            pl.BlockSpec((CONV_W, D_LRU), const2),
            pl.BlockSpec((1, D_LRU), const2),
            pl.BlockSpec((N_LRU_HEADS, LRU_BLOCK, 2 * LRU_BLOCK), const3),
            pl.BlockSpec((1, D_LRU), const2),
            pl.BlockSpec((1, D_LRU), const2),
            pl.BlockSpec((1, D_LRU), const2),
            pl.BlockSpec((1, D_POOL), const2),
        ],
        out_specs=[
            pl.BlockSpec((ts, D_POOL), row),
            pl.BlockSpec((ts, D_LRU), row),
            pl.BlockSpec((ts, D_LRU), row),
            pl.BlockSpec((ts, D_LRU), row),
        ],
        scratch_shapes=[
            pltpu.VMEM((ts + 2 * HALO, D_MODEL), BF16),
            pltpu.VMEM((ts, D_LRU), F32),
            pltpu.VMEM((ts, D_LRU), F32),
            pltpu.VMEM((1, D_LRU), F32),
        ],
    )
    return pl.pallas_call(
        functools.partial(_mix_a_kernel, ts=ts, starts=starts),
        grid_spec=grid_spec,
        out_shape=[
            jax.ShapeDtypeStruct((n, D_POOL), BF16),
            jax.ShapeDtypeStruct((n, D_LRU), F32),
            jax.ShapeDtypeStruct((n, D_LRU), F32),
            jax.ShapeDtypeStruct((n, D_LRU), F32),
        ],
        compiler_params=pltpu.CompilerParams(
            dimension_semantics=("arbitrary",), vmem_limit_bytes=VMEM_LIMIT),
        name="mix_a",
    )(tile_pos, tile_len, *xs, *xs, *xs, p["norm_mix"], p["w_in"], p["pool_w"], p["pool_scale"],
      p["conv_w"], p["conv_b"], p["wg_f"], p["ba_f"], p["bx_f"], p["sp_f"], p["out_norm_pool"])


def _mix_b_kernel(pos_ref, len_ref, *refs, ts, g_tiles, starts):
    n_src = len(starts)
    x_refs = refs[0:n_src]
    (xc_ref, hf_ref, gate_ref, ypn_ref, wg_ref, ba_ref, bx_ref, sp_ref, onl_ref, wout_ref,
     o_ref, a_scr, b_scr, hb_scr, carry_ref) = refs[n_src:]
    g = g_tiles - 1 - pl.program_id(0)
    at_end = pos_ref[g] + ts == len_ref[g]
    _lru_gates(xc_ref[...], wg_ref, ba_ref, bx_ref, sp_ref, a_scr, b_scr)
    carry0 = jnp.where(at_end, 0.0, carry_ref[...])
    carry_ref[...] = _scan_rows(a_scr, b_scr, hb_scr, carry0, ts, reverse=True)
    y_lru = (hf_ref[...] + hb_scr[...]) * gate_ref[...]
    yln = _rms(y_lru, onl_ref[...]).astype(BF16)
    out = jnp.dot(ypn_ref[...], wout_ref[0:D_POOL, :], preferred_element_type=F32)
    out = out + jnp.dot(yln, wout_ref[D_POOL:, :], preferred_element_type=F32)
    o_ref[...] = _pick_source(g, starts, x_refs) + out


def _mix_b(xs, xc, hf, gate, ypn, tile_pos, tile_len, p, *, ts):
    starts, g_tiles = _source_starts(xs, ts)
    n = g_tiles * ts
    row = lambda g, *_: (g_tiles - 1 - g, 0)

    def cur(start, x):
        return pl.BlockSpec((ts, D_MODEL),
                            lambda g, *_: (jnp.clip(g_tiles - 1 - g - start, 0, x.shape[0] // ts - 1), 0))

    const2 = lambda g, *_: (0, 0)
    const3 = lambda g, *_: (0, 0, 0)
    grid_spec = pltpu.PrefetchScalarGridSpec(
        num_scalar_prefetch=2,
        grid=(g_tiles,),
        in_specs=[
            *[cur(st, x) for st, x in zip(starts, xs)],
            pl.BlockSpec((ts, D_LRU), row),
            pl.BlockSpec((ts, D_LRU), row),
            pl.BlockSpec((ts, D_LRU), row),
            pl.BlockSpec((ts, D_POOL), row),
            pl.BlockSpec((N_LRU_HEADS, LRU_BLOCK, 2 * LRU_BLOCK), const3),
            pl.BlockSpec((1, D_LRU), const2),
            pl.BlockSpec((1, D_LRU), const2),
            pl.BlockSpec((1, D_LRU), const2),
            pl.BlockSpec((1, D_LRU), const2),
            pl.BlockSpec((D_POOL + D_LRU, D_MODEL), const2),
        ],
        out_specs=pl.BlockSpec((ts, D_MODEL), row),
        scratch_shapes=[
            pltpu.VMEM((ts, D_LRU), F32),
            pltpu.VMEM((ts, D_LRU), F32),
            pltpu.VMEM((ts, D_LRU), F32),
            pltpu.VMEM((1, D_LRU), F32),
        ],
    )
    return pl.pallas_call(
        functools.partial(_mix_b_kernel, ts=ts, g_tiles=g_tiles, starts=starts),
        grid_spec=grid_spec,
        out_shape=jax.ShapeDtypeStruct((n, D_MODEL), F32),
        compiler_params=pltpu.CompilerParams(
            dimension_semantics=("arbitrary",), vmem_limit_bytes=VMEM_LIMIT),
        name="mix_b",
    )(tile_pos, tile_len, *xs, xc, hf, gate, ypn, p["wg_b"], p["ba_b"], p["bx_b"], p["sp_b"],
      p["out_norm_lru"], p["w_out"])


def _router_kernel(x_ref, nf_ref, rwh_ref, rwl_ref, rb_ref,
                   hf_ref, meta_ref, wts_ref, cnt_ref,
                   carry_ref, *, tr):
    @pl.when(pl.program_id(0) == 0)
    def _():
        carry_ref[...] = jnp.zeros_like(carry_ref)

    hf = _rms(x_ref[...], nf_ref[...])
    hf_ref[...] = hf.reshape(tr, 1, D_MODEL)
    h_hi = hf.astype(BF16)
    h_lo = (hf - h_hi.astype(F32)).astype(BF16)
    logits = (jnp.dot(h_hi, rwh_ref[...], preferred_element_type=F32)
              + jnp.dot(h_lo, rwh_ref[...], preferred_element_type=F32)
              + jnp.dot(h_hi, rwl_ref[...], preferred_element_type=F32)) + rb_ref[...]
    lane = lax.broadcasted_iota(jnp.int32, (tr, LANES), 1).astype(F32)

    vals, idxs = [], []
    l = logits
    for _ in range(TOP_K):
        m = jnp.max(l, axis=-1, keepdims=True)
        ik = jnp.min(jnp.where(l == m, lane, float(LANES)), axis=-1, keepdims=True)
        vals.append(m)
        idxs.append(ik)
        l = jnp.where(lane == ik, -jnp.inf, l)
    es = [jnp.exp(v - vals[0]) for v in vals]
    den = es[0] + es[1] + es[2] + es[3]

    hits = [lane == ik for ik in idxs]
    chosen = jnp.zeros((tr, LANES), F32)
    for hit in hits:
        chosen = chosen + jnp.where(hit, 1.0, 0.0)
    tri = jnp.where(lax.broadcasted_iota(jnp.int32, (tr, tr), 1) < lax.broadcasted_iota(jnp.int32, (tr, tr), 0),
                    1.0, 0.0).astype(BF16)
    before = jnp.dot(tri, chosen.astype(BF16), preferred_element_type=F32) + carry_ref[...]
    total = carry_ref[...] + jnp.sum(chosen, axis=0, keepdims=True)
    carry_ref[...] = total
    cnt_ref[...] = total

    meta = jnp.zeros((tr, LANES), F32)
    wts = jnp.zeros((tr, LANES), F32)
    for k in range(TOP_K):
        pos_k = jnp.sum(jnp.where(hits[k], before, 0.0), axis=-1, keepdims=True)
        meta = jnp.where(lane == float(k), idxs[k], meta)
        meta = jnp.where(lane == float(TOP_K + k), pos_k, meta)
        wts = jnp.where(lane == float(k), es[k] / den, wts)
    meta_ref[...] = jnp.transpose(meta)[0:2 * TOP_K, :].astype(jnp.int32)
    wts_ref[...] = wts


def _router(x, p, *, tr):
    n = x.shape[0]
    row = lambda j: (j, 0)
    const2 = lambda j: (0, 0)
    return pl.pallas_call(
        functools.partial(_router_kernel, tr=tr),
        grid=(n // tr,),
        in_specs=[
            pl.BlockSpec((tr, D_MODEL), row),
            pl.BlockSpec((1, D_MODEL), const2),
            pl.BlockSpec((D_MODEL, LANES), const2),
            pl.BlockSpec((D_MODEL, LANES), const2),
            pl.BlockSpec((1, LANES), const2),
        ],
        out_specs=[
            pl.BlockSpec((tr, 1, D_MODEL), lambda j: (j, 0, 0)),
            pl.BlockSpec((2 * TOP_K, tr), lambda j: (0, j)),
            pl.BlockSpec((tr, LANES), row),
            pl.BlockSpec((1, LANES), const2),
        ],
        out_shape=[
            jax.ShapeDtypeStruct((n, 1, D_MODEL), F32),
            jax.ShapeDtypeStruct((2 * TOP_K, n), jnp.int32),
            jax.ShapeDtypeStruct((n, LANES), F32),
            jax.ShapeDtypeStruct((1, LANES), F32),
        ],
        scratch_shapes=[pltpu.VMEM((1, LANES), F32)],
        compiler_params=pltpu.CompilerParams(
            dimension_semantics=("arbitrary",), vmem_limit_bytes=VMEM_LIMIT),
        name="router",
    )(x, p["norm_ffn"], p["router_w_hi"], p["router_w_lo"], p["router_b"])


def _moe_kernel(te_ref, used_ref,
                tok0_ref, tokn_ref, dst_ref, hf_hbm, wup_ref, bup_ref, wdn_ref, bdn_ref,
                ys_hbm,
                xbuf, obuf, x2d, o2d, gsem, ssem, *, tm, nt):
    s = pl.program_id(0)
    slot = lax.rem(s, 2)
    other = 1 - slot
    o_fill = lax.rem(s + (OUT_SLOTS - 1), OUT_SLOTS)
    o_send = lax.rem(s + (OUT_SLOTS - 2), OUT_SLOTS)

    def gather_wait(b):
        pltpu.make_async_copy(hf_hbm.at[pl.ds(0, tm)], xbuf.at[pl.ds(b * tm, tm)], gsem.at[b]).wait()

    def scatter_wait(b):
        pltpu.make_async_copy(obuf.at[pl.ds(b * tm, tm)], ys_hbm.at[pl.ds(0, tm)], ssem.at[b]).wait()

    def gather_row(tok, b, r, prio=0):
        pltpu.make_async_copy(hf_hbm.at[pl.ds(tok, 1)], xbuf.at[pl.ds(b * tm + r, 1)], gsem.at[b]).start(priority=prio)

    def scatter_row(dst, b, r, prio=0):
        pltpu.make_async_copy(obuf.at[pl.ds(b * tm + r, 1)], ys_hbm.at[pl.ds(dst, 1)], ssem.at[b]).start(priority=prio)

    @pl.when(s == 0)
    def _():
        obuf[...] = jnp.zeros_like(obuf)
        o2d[...] = jnp.zeros_like(o2d)

        def body(r, c):
            gather_row(tok0_ref[0, 0, r], 0, r)
            return c
        lax.fori_loop(0, tm, body, 0)

    last = used_ref[0] + 1

    @pl.when(s <= last)
    def _():
        @pl.when(s >= OUT_SLOTS - 1)
        def _():
            scatter_wait(o_fill)

        gather_wait(slot)

        xrows = xbuf.at[pl.ds(pl.multiple_of(slot * tm, tm), tm)]
        orows = obuf.at[pl.ds(pl.multiple_of(o_fill * tm, tm), tm)]
        orows[...] = _pack_bf16_pairs(o2d[:, :HALF], o2d[:, HALF:]).reshape(tm, 1, HALF)
        kw = D_MODEL // K_CHUNKS
        bounds = [(tm * g) // (K_CHUNKS - 1) for g in range(K_CHUNKS)] + [tm]
        hh = bup_ref[...]
        for kc in range(K_CHUNKS):
            cols = slice(kc * kw, (kc + 1) * kw)
            x2d[:, cols] = xrows[:, :, cols].reshape(tm, kw)
            for r in range(bounds[kc], bounds[kc + 1]):
                gather_row(tokn_ref[0, 0, r], other, r, r % 2)
                scatter_row(dst_ref[0, 0, r], o_send, r, (r + 1) % 2)
            hh = hh + jnp.dot(x2d[:, cols].astype(BF16), wup_ref[cols, :], preferred_element_type=F32)
        x_glu = jnp.minimum(hh[:, :D_FF], SWIGLU_LIMIT)
        x_lin = jnp.clip(hh[:, D_FF:], -SWIGLU_LIMIT, SWIGLU_LIMIT)
        act = (x_glu * jax.nn.sigmoid(SWIGLU_ALPHA * x_glu) * (x_lin + 1.0)).astype(BF16)
        o2d[...] = jnp.dot(act, wdn_ref[...], preferred_element_type=F32) + bdn_ref[...]

        @pl.when(s == last)
        def _():
            scatter_wait(o_send)
            scatter_wait(lax.rem(s + (OUT_SLOTS - 3), OUT_SLOTS))
            gather_wait(other)


def _moe(hf3, tile_expert, used, tok3, dst3, w_up, b_up, w_down, b_down, *, layer, tm, n_out_rows):
    nt = tok3.shape[0]
    smem_blk = lambda fn: pl.BlockSpec((1, 1, tm), fn, memory_space=pltpu.SMEM)
    grid_spec = pltpu.PrefetchScalarGridSpec(
        num_scalar_prefetch=2,
        grid=(nt,),
        in_specs=[
            smem_blk(lambda s, te, u: (0, 0, 0)),
            smem_blk(lambda s, te, u: (jnp.minimum(s + 1, nt - 1), 0, 0)),
            smem_blk(lambda s, te, u: (s, 0, 0)),
            pl.BlockSpec(memory_space=pl.ANY),
            pl.BlockSpec((None, None, D_MODEL, 2 * D_FF), lambda s, te, u: (layer, te[s], 0, 0)),
            pl.BlockSpec((None, None, 1, 2 * D_FF), lambda s, te, u: (layer, te[s], 0, 0)),
            pl.BlockSpec((None, None, D_FF, D_MODEL), lambda s, te, u: (layer, te[s], 0, 0)),
            pl.BlockSpec((None, None, 1, D_MODEL), lambda s, te, u: (layer, te[s], 0, 0)),
        ],
        out_specs=pl.BlockSpec(memory_space=pl.ANY),
        scratch_shapes=[
            pltpu.VMEM((2 * tm, 1, D_MODEL), F32),
            pltpu.VMEM((OUT_SLOTS * tm, 1, HALF), jnp.uint32),
            pltpu.VMEM((tm, D_MODEL), F32),
            pltpu.VMEM((tm, D_MODEL), F32),
            pltpu.SemaphoreType.DMA((2,)),
            pltpu.SemaphoreType.DMA((OUT_SLOTS,)),
        ],
    )
    return pl.pallas_call(
        functools.partial(_moe_kernel, tm=tm, nt=nt),
        grid_spec=grid_spec,
        out_shape=jax.ShapeDtypeStruct((n_out_rows, 1, HALF), jnp.uint32),
        compiler_params=pltpu.CompilerParams(
            dimension_semantics=("arbitrary",), vmem_limit_bytes=VMEM_LIMIT),
        name="moe",
    )(tile_expert, used, tok3, tok3, dst3, hf3, w_up, b_up, w_down, b_down)


def _combine_kernel(x_ref, y0_ref, y1_ref, y2_ref, y3_ref, wts_ref, nf_ref, o_ref, y2d, *, tc, final):
    for k, y_ref in enumerate((y0_ref, y1_ref, y2_ref, y3_ref)):
        y2d[k] = y_ref[...].reshape(tc, HALF)
    w = wts_ref[...]
    acc_lo = acc_hi = None
    for k in range(TOP_K):
        lo, hi = _unpack_bf16_pairs(y2d[k])
        acc_lo = w[:, k:k + 1] * lo if k == 0 else acc_lo + w[:, k:k + 1] * lo
        acc_hi = w[:, k:k + 1] * hi if k == 0 else acc_hi + w[:, k:k + 1] * hi
    x = x_ref[...] + jnp.concatenate([acc_lo, acc_hi], axis=-1)
    o_ref[...] = _rms(x, nf_ref[...]) if final else x


def _combine(x, ys3, wts, norm_final, *, tc, final, row0, n_rows):
    n = x.shape[0]
    b0 = row0 // tc
    kb = n // tc
    row = lambda j: (b0 + j, 0)
    y_spec = lambda k: pl.BlockSpec((tc, 1, HALF), lambda j: (k * kb + b0 + j, 0, 0))
    return pl.pallas_call(
        functools.partial(_combine_kernel, tc=tc, final=final),
        grid=(n_rows // tc,),
        in_specs=[
            pl.BlockSpec((tc, D_MODEL), row),
            y_spec(0), y_spec(1), y_spec(2), y_spec(3),
            pl.BlockSpec((tc, LANES), row),
            pl.BlockSpec((1, D_MODEL), lambda j: (0, 0)),
        ],
        out_specs=pl.BlockSpec((tc, D_MODEL), lambda j: (j, 0)),
        out_shape=jax.ShapeDtypeStruct((n_rows, D_MODEL), F32),
        scratch_shapes=[pltpu.VMEM((TOP_K, tc, HALF), jnp.uint32)],
        compiler_params=pltpu.CompilerParams(
            dimension_semantics=("arbitrary",), vmem_limit_bytes=VMEM_LIMIT),
        name="combine_final" if final else "combine",
    )(x, ys3, ys3, ys3, ys3, wts, norm_final)


def _seq_tables(groups, ts):
    pos, length = [], []
    for n_seq, seq_len in groups:
        assert seq_len % ts == 0
        for _ in range(n_seq):
            for t in range(seq_len // ts):
                pos.append(t * ts)
                length.append(seq_len)
    return np.asarray(pos, np.int32), np.asarray(length, np.int32)


def _routing_plan(meta, counts, *, n, tm, nt):
    idx = meta[0:TOP_K]
    pos = meta[TOP_K:2 * TOP_K]
    cnt = counts[0, :N_EXPERTS].astype(jnp.int32)
    tiles_e = (cnt + tm - 1) // tm
    tile_end = jnp.cumsum(tiles_e)
    base = (tile_end - tiles_e) * tm
    base_of = jnp.zeros_like(idx)
    for e in range(N_EXPERTS):
        base_of = jnp.where(idx == e, base[e], base_of)
    slot = (base_of + pos).reshape(-1)
    tile_ids = jnp.arange(nt, dtype=jnp.int32)
    tile_expert = jnp.minimum(jnp.sum(tile_end[None, :] <= tile_ids[:, None], axis=1), N_EXPERTS - 1).astype(jnp.int32)
    pairs = jnp.arange(TOP_K * n, dtype=jnp.int32)
    rows = jnp.arange((nt + 2) * tm, dtype=jnp.int32)
    spare = TOP_K * n + ((rows // tm) % 2) * tm + rows % tm
    dst = spare.at[slot + 2 * tm].set(pairs, unique_indices=True, mode="promise_in_bounds")
    src = dst[2 * tm:]
    tok = jnp.where(src < TOP_K * n, src % n, 0)
    return tile_expert, tile_end[-1:], tok.reshape(nt, 1, tm), dst[:nt * tm].reshape(nt, 1, tm)


def _layer_params(l, norm_mix, w_in, pool_w, pool_scale, conv_w, conv_b, lru_wa, lru_ba, lru_wx, lru_bx,
                  lru_lambda, out_norm_pool, out_norm_lru, w_out, norm_ffn, router_w, router_b,
                  w_up, b_up, w_down, b_down):
    def gates(d):
        return dict(
            wg=jnp.concatenate([lru_wa[l, d], lru_wx[l, d]], axis=-1).astype(BF16),
            ba=lru_ba[l, d].reshape(1, D_LRU),
            bx=lru_bx[l, d].reshape(1, D_LRU),
            sp=jax.nn.softplus(-lru_lambda[l, d]).reshape(1, D_LRU),
        )
    gf, gb = gates(0), gates(1)
    pad = LANES - N_EXPERTS
    rw = jnp.pad(router_w[l], ((0, 0), (0, pad)))
    rw_hi = rw.astype(BF16)
    return dict(
        norm_mix=norm_mix[l].reshape(1, D_MODEL),
        w_in=w_in[l].astype(BF16),
        pool_w=pool_w[l].astype(BF16),
        pool_scale=pool_scale[l].reshape(1, D_POOL),
        conv_w=conv_w[l],
        conv_b=conv_b[l].reshape(1, D_LRU),
        wg_f=gf["wg"], ba_f=gf["ba"], bx_f=gf["bx"], sp_f=gf["sp"],
        wg_b=gb["wg"], ba_b=gb["ba"], bx_b=gb["bx"], sp_b=gb["sp"],
        out_norm_pool=out_norm_pool[l].reshape(1, D_POOL),
        out_norm_lru=out_norm_lru[l].reshape(1, D_LRU),
        w_out=w_out[l].astype(BF16),
        norm_ffn=norm_ffn[l].reshape(1, D_MODEL),
        router_w_hi=rw_hi,
        router_w_lo=(rw - rw_hi.astype(F32)).astype(BF16),
        router_b=jnp.pad(router_b[l].reshape(1, N_EXPERTS), ((0, 0), (0, pad)), constant_values=-jnp.inf),
    )


def _trunk(xs, weights, norm_final, *, ts, tr, tm, tc):
    groups = [(x.shape[0], x.shape[1]) for x in xs]
    sizes = [b * s for b, s in groups]
    n = sum(sizes)
    depth = weights[0].shape[0]
    srcs = [xi.reshape(-1, D_MODEL) for xi in xs]
    tile_pos, tile_len = _seq_tables(groups, ts)
    tile_pos, tile_len = jnp.asarray(tile_pos), jnp.asarray(tile_len)
    nt = (TOP_K * n) // tm + N_EXPERTS + 2
    n_out_rows = TOP_K * n + 2 * tm
    nf = norm_final.reshape(1, D_MODEL)
    w_up, b_up, w_down, b_down = weights[-4:]
    w_up_bf, w_down_bf = w_up.astype(BF16), w_down.astype(BF16)
    b_up3 = b_up.reshape(depth, N_EXPERTS, 1, 2 * D_FF)
    b_down3 = b_down.reshape(depth, N_EXPERTS, 1, D_MODEL)
    outs = None
    for l in range(depth):
        p = _layer_params(l, *weights)
        ypn, hf, gate, xc = _mix_a(srcs, tile_pos, tile_len, p, ts=ts)
        x = _mix_b(srcs, xc, hf, gate, ypn, tile_pos, tile_len, p, ts=ts)
        hf3, meta, wts, counts = _router(x, p, tr=tr)
        tile_expert, used, tok3, dst3 = _routing_plan(meta, counts, n=n, tm=tm, nt=nt)
        ys3 = _moe(hf3, tile_expert, used, tok3, dst3, w_up_bf, b_up3, w_down_bf, b_down3,
                   layer=l, tm=tm, n_out_rows=n_out_rows)
        if l + 1 < depth:
            x = _combine(x, ys3, wts, nf, tc=tc, final=False, row0=0, n_rows=n)
            srcs = [x]
        else:
            outs, row0 = [], 0
            for (b, s), sz in zip(groups, sizes):
                y = _combine(x, ys3, wts, nf, tc=tc, final=True, row0=row0, n_rows=sz)
                outs.append(y.reshape(b, s, D_MODEL))
                row0 += sz
    return outs


def kernel(x_prompt, x_sample, norm_mix, w_in, pool_w, pool_scale, conv_w, conv_b, lru_wa, lru_ba, lru_wx, lru_bx,
           lru_lambda, out_norm_pool, out_norm_lru, w_out, norm_ffn, router_w, router_b, w_up, b_up, w_down,
           b_down, norm_final):
    weights = (norm_mix, w_in, pool_w, pool_scale, conv_w, conv_b, lru_wa, lru_ba, lru_wx, lru_bx, lru_lambda,
               out_norm_pool, out_norm_lru, w_out, norm_ffn, router_w, router_b, w_up, b_up, w_down, b_down)
    y_prompt, y_sample = _trunk([x_prompt, x_sample], weights, norm_final, ts=512, tr=512, tm=512, tc=256)
    return (y_prompt, y_sample)
```

```python
import functools

import numpy as np
import jax
import jax.numpy as jnp
from jax import lax
from jax.experimental import pallas as pl
from jax.experimental.pallas import tpu as pltpu

D_MODEL = 1024
D_POOL = 512
D_LRU = 512
D_IN = D_POOL + 2 * D_LRU
POOL_WINDOWS = (2, 4, 8, 16)
POOL_GROUP_W = D_POOL // len(POOL_WINDOWS)
N_LRU_HEADS = 4
LRU_BLOCK = D_LRU // N_LRU_HEADS
LRU_C = 8.0
CONV_W = 4
CONV_LEFT = 1
N_EXPERTS = 32
TOP_K = 4
D_FF = D_MODEL
SWIGLU_LIMIT = 7.0
SWIGLU_ALPHA = 1.702
EPS = 1e-6

LANES = 128
SUBLANES = 8
HALO = 16
VMEM_LIMIT = 48 * 1024 * 1024
MOE_VMEM_LIMIT = 56 * 1024 * 1024
K_CHUNKS = 2
HALF = D_MODEL // 2
SCAN_UNROLL = 4
OUT_SLOTS = 3

F32 = jnp.float32
BF16 = jnp.bfloat16


def _rms(x, g):
    return x * lax.rsqrt(jnp.mean(x * x, axis=-1, keepdims=True) + EPS) * g


def _pack_bf16_pairs(lo, hi):
    lo_bits = lax.bitcast_convert_type(lo.astype(BF16).astype(F32), jnp.uint32)
    hi_bits = lax.bitcast_convert_type(hi.astype(BF16).astype(F32), jnp.uint32)
    return lax.shift_right_logical(lo_bits, jnp.uint32(16)) | hi_bits


def _unpack_bf16_pairs(w):
    lo = lax.bitcast_convert_type(lax.shift_left(w, jnp.uint32(16)), F32)
    hi = lax.bitcast_convert_type(w & jnp.uint32(0xFFFF0000), F32)
    return lo, hi


def _shift_rows(v, s):
    n = v.shape[0]
    return pltpu.roll(v, s % n, axis=0)


def _lru_gates(xc, wg_ref, ba_ref, bx_ref, sp_ref, a_scr, b_scr):
    for h in range(N_LRU_HEADS):
        cs = slice(h * LRU_BLOCK, (h + 1) * LRU_BLOCK)
        xh = xc[:, cs]
        gr = jnp.dot(xh.astype(BF16), wg_ref[h], preferred_element_type=F32)
        r = jax.nn.sigmoid(gr[:, :LRU_BLOCK] + ba_ref[:, cs])
        i = jax.nn.sigmoid(gr[:, LRU_BLOCK:] + bx_ref[:, cs])
        log_a = -LRU_C * r * sp_ref[:, cs]
        a = jnp.exp(log_a)
        beta = jnp.sqrt(1.0 - a * a)
        a_scr[:, cs] = a
        b_scr[:, cs] = beta * i * xh


def _scan_rows(a_ref, b_ref, h_ref, carry0, n_rows, reverse):
    row = lax.broadcasted_iota(jnp.int32, (SUBLANES, D_LRU), 0)
    n_grp = n_rows // SUBLANES
    n_iter = n_grp // SCAN_UNROLL

    def prefix(r0):
        a = a_ref[pl.ds(r0, SUBLANES), :]
        b = b_ref[pl.ds(r0, SUBLANES), :]
        for s in (1, 2, 4):
            if reverse:
                ok = row < SUBLANES - s
                a_sh = jnp.where(ok, pltpu.roll(a, SUBLANES - s, axis=0), 1.0)
                b_sh = jnp.where(ok, pltpu.roll(b, SUBLANES - s, axis=0), 0.0)
            else:
                ok = row >= s
                a_sh = jnp.where(ok, pltpu.roll(a, s, axis=0), 1.0)
                b_sh = jnp.where(ok, pltpu.roll(b, s, axis=0), 0.0)
            b = a * b_sh + b
            a = a * a_sh
        return a, b

    def body(k, carry):
        first = (n_iter - 1 - k) if reverse else k
        order = range(SCAN_UNROLL - 1, -1, -1) if reverse else range(SCAN_UNROLL)
        starts = [pl.multiple_of((first * SCAN_UNROLL + u) * SUBLANES, SUBLANES) for u in order]
        parts = [prefix(r0) for r0 in starts]
        for r0, (a, b) in zip(starts, parts):
            h = a * carry + b
            h_ref[pl.ds(r0, SUBLANES), :] = h
            carry = h[0:1, :] if reverse else h[SUBLANES - 1:SUBLANES, :]
        return carry

    return lax.fori_loop(0, n_iter, body, carry0)


def _pick_source(g, starts, refs):
    v = refs[0][...]
    for start, ref in zip(starts[1:], refs[1:]):
        v = jnp.where(g >= start, ref[...], v)
    return v


def _mix_a_kernel(pos_ref, len_ref, *refs, ts, starts):
    n_src = len(starts)
    x_refs, xp_refs, xn_refs = refs[0:n_src], refs[n_src:2 * n_src], refs[2 * n_src:3 * n_src]
    (nm_ref, win_ref, pw_ref, ps_ref, cw_ref, cb_ref, wg_ref, ba_ref, bx_ref, sp_ref, onp_ref,
     ypn_ref, hf_ref, gate_ref, xc_ref, h_scr, a_scr, b_scr, carry_ref) = refs[3 * n_src:]
    g = pl.program_id(0)
    pos0 = pos_ref[g]
    slen = len_ref[g]
    keep_prev = jnp.where(pos0 == 0, 0.0, 1.0).astype(F32)
    keep_next = jnp.where(pos0 + ts == slen, 0.0, 1.0).astype(F32)
    nm = nm_ref[...]

    h_scr[0:HALO, :] = (_rms(_pick_source(g, starts, xp_refs), nm) * keep_prev).astype(BF16)
    h_scr[HALO:HALO + ts, :] = _rms(_pick_source(g, starts, x_refs), nm).astype(BF16)
    h_scr[HALO + ts:, :] = (_rms(_pick_source(g, starts, xn_refs), nm) * keep_next).astype(BF16)
    z = jnp.dot(h_scr[...], win_ref[...], preferred_element_type=F32)

    trow = pos0 + lax.broadcasted_iota(jnp.int32, (ts, POOL_GROUP_W), 0)
    ys = []
    for gi, win in enumerate(POOL_WINDOWS):
        u = z[:, gi * POOL_GROUP_W:(gi + 1) * POOL_GROUP_W]
        acc = u + _shift_rows(u, 1)
        half = 1
        while 2 * half < win:
            acc = _shift_rows(acc, half) + _shift_rows(acc, -half)
            half *= 2
        half = win // 2
        cnt = (jnp.minimum(trow + half, slen) - jnp.maximum(trow - half, 0)).astype(F32)
        p = acc[HALO:HALO + ts] / cnt - u[HALO:HALO + ts]
        ys.append(jnp.dot(p.astype(BF16), pw_ref[gi], preferred_element_type=F32))
    y_pool = jnp.concatenate(ys, axis=-1) * ps_ref[...]
    ypn_ref[...] = _rms(y_pool, onp_ref[...]).astype(BF16)

    ul = z[:, D_POOL:D_POOL + D_LRU]
    xc_ext = cb_ref[...] + _shift_rows(ul, CONV_LEFT) * cw_ref[0:1, :]
    xc_ext = xc_ext + ul * cw_ref[1:2, :]
    xc_ext = xc_ext + _shift_rows(ul, -1) * cw_ref[2:3, :]
    xc_ext = xc_ext + _shift_rows(ul, -2) * cw_ref[3:4, :]
    xc = xc_ext[HALO:HALO + ts]
    xc_ref[...] = xc

    gate_ref[...] = jax.nn.gelu(z[HALO:HALO + ts, D_POOL + D_LRU:], approximate=True)

    _lru_gates(xc, wg_ref, ba_ref, bx_ref, sp_ref, a_scr, b_scr)
    carry0 = jnp.where(pos0 == 0, 0.0, carry_ref[...])
    carry_ref[...] = _scan_rows(a_scr, b_scr, hf_ref, carry0, ts, reverse=False)


def _source_starts(xs, ts):
    starts, at = [], 0
    for x in xs:
        starts.append(at)
        at += x.shape[0] // ts
    return tuple(starts), at


def _mix_a(xs, tile_pos, tile_len, p, *, ts):
    starts, g_tiles = _source_starts(xs, ts)
    n = g_tiles * ts
    hb = ts // HALO
    row = lambda g, *_: (g, 0)
    const2 = lambda g, *_: (0, 0)
    const3 = lambda g, *_: (0, 0, 0)

    def cur(start, x):
        return pl.BlockSpec((ts, D_MODEL), lambda g, *_: (jnp.clip(g - start, 0, x.shape[0] // ts - 1), 0))

    def prev(start, x):
        return pl.BlockSpec((HALO, D_MODEL),
                            lambda g, *_: (jnp.clip((g - start) * hb - 1, 0, x.shape[0] // HALO - 1), 0))

    def nxt(start, x):
        return pl.BlockSpec((HALO, D_MODEL),
                            lambda g, *_: (jnp.clip((g - start + 1) * hb, 0, x.shape[0] // HALO - 1), 0))

    grid_spec = pltpu.PrefetchScalarGridSpec(
        num_scalar_prefetch=2,
        grid=(g_tiles,),
        in_specs=[
            *[cur(st, x) for st, x in zip(starts, xs)],
            *[prev(st, x) for st, x in zip(starts, xs)],
            *[nxt(st, x) for st, x in zip(starts, xs)],
            pl.BlockSpec((1, D_MODEL), const2),
            pl.BlockSpec((D_MODEL, D_IN), const2),
            pl.BlockSpec((len(POOL_WINDOWS), POOL_GROUP_W, POOL_GROUP_W), const3),
            pl.BlockSpec((1, D_POOL), const2),
            pl.BlockSpec((CONV_W, D_LRU), const2),
            pl.BlockSpec((1, D_LRU), const2),
            pl.BlockSpec((N_LRU_HEADS, LRU_BLOCK, 2 * LRU_BLOCK), const3),
            pl.BlockSpec((1, D_LRU), const2),
            pl.BlockSpec((1, D_LRU), const2),
            pl.BlockSpec((1, D_LRU), const2),
            pl.BlockSpec((1, D_POOL), const2),
        ],
        out_specs=[
            pl.BlockSpec((ts, D_POOL), row),
            pl.BlockSpec((ts, D_LRU), row),
            pl.BlockSpec((ts, D_LRU), row),
            pl.BlockSpec((ts, D_LRU), row),
        ],
        scratch_shapes=[
            pltpu.VMEM((ts + 2 * HALO, D_MODEL), BF16),
            pltpu.VMEM((ts, D_LRU), F32),
            pltpu.VMEM((ts, D_LRU), F32),
            pltpu.VMEM((1, D_LRU), F32),
        ],
    )
    return pl.pallas_call(
        functools.partial(_mix_a_kernel, ts=ts, starts=starts),
        grid_spec=grid_spec,
        out_shape=[
            jax.ShapeDtypeStruct((n, D_POOL), BF16),
            jax.ShapeDtypeStruct((n, D_LRU), F32),
            jax.ShapeDtypeStruct((n, D_LRU), F32),
            jax.ShapeDtypeStruct((n, D_LRU), F32),
        ],
        compiler_params=pltpu.CompilerParams(
            dimension_semantics=("arbitrary",), vmem_limit_bytes=VMEM_LIMIT),
        name="mix_a",
    )(tile_pos, tile_len, *xs, *xs, *xs, p["norm_mix"], p["w_in"], p["pool_w"], p["pool_scale"],
      p["conv_w"], p["conv_b"], p["wg_f"], p["ba_f"], p["bx_f"], p["sp_f"], p["out_norm_pool"])


def _mix_b_kernel(pos_ref, len_ref, *refs, ts, g_tiles, starts):
    n_src = len(starts)
    x_refs = refs[0:n_src]
    (xc_ref, hf_ref, gate_ref, ypn_ref, wg_ref, ba_ref, bx_ref, sp_ref, onl_ref, wout_ref,
     o_ref, a_scr, b_scr, hb_scr, carry_ref) = refs[n_src:]
    g = g_tiles - 1 - pl.program_id(0)
    at_end = pos_ref[g] + ts == len_ref[g]
    _lru_gates(xc_ref[...], wg_ref, ba_ref, bx_ref, sp_ref, a_scr, b_scr)
    carry0 = jnp.where(at_end, 0.0, carry_ref[...])
    carry_ref[...] = _scan_rows(a_scr, b_scr, hb_scr, carry0, ts, reverse=True)
    y_lru = (hf_ref[...] + hb_scr[...]) * gate_ref[...]
    yln = _rms(y_lru, onl_ref[...]).astype(BF16)
    out = jnp.dot(ypn_ref[...], wout_ref[0:D_POOL, :], preferred_element_type=F32)
    out = out + jnp.dot(yln, wout_ref[D_POOL:, :], preferred_element_type=F32)
    o_ref[...] = _pick_source(g, starts, x_refs) + out


def _mix_b(xs, xc, hf, gate, ypn, tile_pos, tile_len, p, *, ts):
    starts, g_tiles = _source_starts(xs, ts)
    n = g_tiles * ts
    row = lambda g, *_: (g_tiles - 1 - g, 0)

    def cur(start, x):
        return pl.BlockSpec((ts, D_MODEL),
                            lambda g, *_: (jnp.clip(g_tiles - 1 - g - start, 0, x.shape[0] // ts - 1), 0))

    const2 = lambda g, *_: (0, 0)
    const3 = lambda g, *_: (0, 0, 0)
    grid_spec = pltpu.PrefetchScalarGridSpec(
        num_scalar_prefetch=2,
        grid=(g_tiles,),
        in_specs=[
            *[cur(st, x) for st, x in zip(starts, xs)],
            pl.BlockSpec((ts, D_LRU), row),
            pl.BlockSpec((ts, D_LRU), row),
            pl.BlockSpec((ts, D_LRU), row),
            pl.BlockSpec((ts, D_POOL), row),
            pl.BlockSpec((N_LRU_HEADS, LRU_BLOCK, 2 * LRU_BLOCK), const3),
            pl.BlockSpec((1, D_LRU), const2),
            pl.BlockSpec((1, D_LRU), const2),
            pl.BlockSpec((1, D_LRU), const2),
            pl.BlockSpec((1, D_LRU), const2),
            pl.BlockSpec((D_POOL + D_LRU, D_MODEL), const2),
        ],
        out_specs=pl.BlockSpec((ts, D_MODEL), row),
        scratch_shapes=[
            pltpu.VMEM((ts, D_LRU), F32),
            pltpu.VMEM((ts, D_LRU), F32),
            pltpu.VMEM((ts, D_LRU), F32),
            pltpu.VMEM((1, D_LRU), F32),
        ],
    )
    return pl.pallas_call(
        functools.partial(_mix_b_kernel, ts=ts, g_tiles=g_tiles, starts=starts),
        grid_spec=grid_spec,
        out_shape=jax.ShapeDtypeStruct((n, D_MODEL), F32),
        compiler_params=pltpu.CompilerParams(
            dimension_semantics=("arbitrary",), vmem_limit_bytes=VMEM_LIMIT),
        name="mix_b",
    )(tile_pos, tile_len, *xs, xc, hf, gate, ypn, p["wg_b"], p["ba_b"], p["bx_b"], p["sp_b"],
      p["out_norm_lru"], p["w_out"])


def _router_kernel(x_ref, nf_ref, rwh_ref, rwl_ref, rb_ref,
                   hf_ref, meta_ref, wts_ref, cnt_ref,
                   carry_ref, *, tr):
    @pl.when(pl.program_id(0) == 0)
    def _():
        carry_ref[...] = jnp.zeros_like(carry_ref)

    hf = _rms(x_ref[...], nf_ref[...])
    hf_ref[...] = _pack_bf16_pairs(hf[:, :HALF], hf[:, HALF:]).reshape(tr, 1, HALF)
    h_hi = hf.astype(BF16)
    h_lo = (hf - h_hi.astype(F32)).astype(BF16)
    logits = (jnp.dot(h_hi, rwh_ref[...], preferred_element_type=F32)
              + jnp.dot(h_lo, rwh_ref[...], preferred_element_type=F32)
              + jnp.dot(h_hi, rwl_ref[...], preferred_element_type=F32)) + rb_ref[...]
    lane = lax.broadcasted_iota(jnp.int32, (tr, LANES), 1).astype(F32)

    vals, idxs = [], []
    l = logits
    for _ in range(TOP_K):
        m = jnp.max(l, axis=-1, keepdims=True)
        ik = jnp.min(jnp.where(l == m, lane, float(LANES)), axis=-1, keepdims=True)
        vals.append(m)
        idxs.append(ik)
        l = jnp.where(lane == ik, -jnp.inf, l)
    es = [jnp.exp(v - vals[0]) for v in vals]
    den = es[0] + es[1] + es[2] + es[3]

    hits = [lane == ik for ik in idxs]
    chosen = jnp.zeros((tr, LANES), F32)
    for hit in hits:
        chosen = chosen + jnp.where(hit, 1.0, 0.0)
    tri = jnp.where(lax.broadcasted_iota(jnp.int32, (tr, tr), 1) < lax.broadcasted_iota(jnp.int32, (tr, tr), 0),
                    1.0, 0.0).astype(BF16)
    before = jnp.dot(tri, chosen.astype(BF16), preferred_element_type=F32) + carry_ref[...]
    total = carry_ref[...] + jnp.sum(chosen, axis=0, keepdims=True)
    carry_ref[...] = total
    cnt_ref[...] = total

    meta = jnp.zeros((tr, LANES), F32)
    wts = jnp.zeros((tr, LANES), F32)
    for k in range(TOP_K):
        pos_k = jnp.sum(jnp.where(hits[k], before, 0.0), axis=-1, keepdims=True)
        meta = jnp.where(lane == float(k), idxs[k], meta)
        meta = jnp.where(lane == float(TOP_K + k), pos_k, meta)
        wts = jnp.where(lane == float(k), es[k] / den, wts)
    meta_ref[...] = jnp.transpose(meta)[0:2 * TOP_K, :].astype(jnp.int32)
    wts_ref[...] = wts


def _router(x, p, *, tr):
    n = x.shape[0]
    row = lambda j: (j, 0)
    const2 = lambda j: (0, 0)
    return pl.pallas_call(
        functools.partial(_router_kernel, tr=tr),
        grid=(n // tr,),
        in_specs=[
            pl.BlockSpec((tr, D_MODEL), row),
            pl.BlockSpec((1, D_MODEL), const2),
            pl.BlockSpec((D_MODEL, LANES), const2),
            pl.BlockSpec((D_MODEL, LANES), const2),
            pl.BlockSpec((1, LANES), const2),
        ],
        out_specs=[
            pl.BlockSpec((tr, 1, HALF), lambda j: (j, 0, 0)),
            pl.BlockSpec((2 * TOP_K, tr), lambda j: (0, j)),
            pl.BlockSpec((tr, LANES), row),
            pl.BlockSpec((1, LANES), const2),
        ],
        out_shape=[
            jax.ShapeDtypeStruct((n, 1, HALF), jnp.uint32),
            jax.ShapeDtypeStruct((2 * TOP_K, n), jnp.int32),
            jax.ShapeDtypeStruct((n, LANES), F32),
            jax.ShapeDtypeStruct((1, LANES), F32),
        ],
        scratch_shapes=[pltpu.VMEM((1, LANES), F32)],
        compiler_params=pltpu.CompilerParams(
            dimension_semantics=("arbitrary",), vmem_limit_bytes=VMEM_LIMIT),
        name="router",
    )(x, p["norm_ffn"], p["router_w_hi"], p["router_w_lo"], p["router_b"])


def _moe_kernel(te_ref, used_ref,
                tok0_ref, tokn_ref, dst_ref, hf_hbm, wup_ref, bup_ref, wdn_ref, bdn_ref,
                ys_hbm,
                xbuf, obuf, x2d, o2d, wup_bf, wdn_bf, gsem, ssem, *, tm, nt):
    s = pl.program_id(0)
    slot = lax.rem(s, 2)
    other = 1 - slot
    o_fill = lax.rem(s + (OUT_SLOTS - 1), OUT_SLOTS)
    o_send = lax.rem(s + (OUT_SLOTS - 2), OUT_SLOTS)

    def gather_wait(b):
        pltpu.make_async_copy(hf_hbm.at[pl.ds(0, tm)], xbuf.at[pl.ds(b * tm, tm)], gsem.at[b]).wait()

    def scatter_wait(b):
        pltpu.make_async_copy(obuf.at[pl.ds(b * tm, tm)], ys_hbm.at[pl.ds(0, tm)], ssem.at[b]).wait()

    def gather_row(tok, b, r, prio=0):
        pltpu.make_async_copy(hf_hbm.at[pl.ds(tok, 1)], xbuf.at[pl.ds(b * tm + r, 1)], gsem.at[b]).start(priority=prio)

    def scatter_row(dst, b, r, prio=0):
        pltpu.make_async_copy(obuf.at[pl.ds(b * tm + r, 1)], ys_hbm.at[pl.ds(dst, 1)], ssem.at[b]).start(priority=prio)

    @pl.when(s == 0)
    def _():
        obuf[...] = jnp.zeros_like(obuf)
        o2d[...] = jnp.zeros_like(o2d)

        def body(r, c):
            gather_row(tok0_ref[0, 0, r], 0, r)
            return c
        lax.fori_loop(0, tm, body, 0)

    last = used_ref[0] + 1

    @pl.when(s <= last)
    def _():
        @pl.when(s >= OUT_SLOTS - 1)
        def _():
            scatter_wait(o_fill)

        gather_wait(slot)

        @pl.when(jnp.logical_or(s == 0, te_ref[s] != te_ref[jnp.maximum(s - 1, 0)]))
        def _():
            wup_bf[...] = wup_ref[...].astype(BF16)
            wdn_bf[...] = wdn_ref[...].astype(BF16)

        xrows = xbuf.at[pl.ds(pl.multiple_of(slot * tm, tm), tm)]
        orows = obuf.at[pl.ds(pl.multiple_of(o_fill * tm, tm), tm)]
        orows[...] = _pack_bf16_pairs(o2d[:, :HALF], o2d[:, HALF:]).reshape(tm, 1, HALF)
        kw = HALF // K_CHUNKS
        bounds = [(tm * g) // (K_CHUNKS - 1) for g in range(K_CHUNKS)] + [tm]
        hh = bup_ref[...]
        for kc in range(K_CHUNKS):
            cols = slice(kc * kw, (kc + 1) * kw)
            x2d[:, cols] = xrows[:, :, cols].reshape(tm, kw)
            for r in range(bounds[kc], bounds[kc + 1]):
                gather_row(tokn_ref[0, 0, r], other, r, r % 2)
                scatter_row(dst_ref[0, 0, r], o_send, r, (r + 1) % 2)
            lo, hi = _unpack_bf16_pairs(x2d[:, cols])
            hh = hh + jnp.dot(lo.astype(BF16), wup_bf[kc * kw:(kc + 1) * kw, :], preferred_element_type=F32)
            hh = hh + jnp.dot(hi.astype(BF16), wup_bf[HALF + kc * kw:HALF + (kc + 1) * kw, :],
                              preferred_element_type=F32)
        x_glu = jnp.minimum(hh[:, :D_FF], SWIGLU_LIMIT)
        x_lin = jnp.clip(hh[:, D_FF:], -SWIGLU_LIMIT, SWIGLU_LIMIT)
        act = (x_glu * jax.nn.sigmoid(SWIGLU_ALPHA * x_glu) * (x_lin + 1.0)).astype(BF16)
        o2d[...] = jnp.dot(act, wdn_bf[...], preferred_element_type=F32) + bdn_ref[...]

        @pl.when(s == last)
        def _():
            scatter_wait(o_send)
            scatter_wait(lax.rem(s + (OUT_SLOTS - 3), OUT_SLOTS))
            gather_wait(other)


def _moe(hf3, tile_expert, used, tok3, dst3, w_up, b_up, w_down, b_down, *, layer, tm, n_out_rows):
    nt = tok3.shape[0]
    smem_blk = lambda fn: pl.BlockSpec((1, 1, tm), fn, memory_space=pltpu.SMEM)
    grid_spec = pltpu.PrefetchScalarGridSpec(
        num_scalar_prefetch=2,
        grid=(nt,),
        in_specs=[
            smem_blk(lambda s, te, u: (0, 0, 0)),
            smem_blk(lambda s, te, u: (jnp.minimum(s + 1, nt - 1), 0, 0)),
            smem_blk(lambda s, te, u: (s, 0, 0)),
            pl.BlockSpec(memory_space=pl.ANY),
            pl.BlockSpec((None, None, D_MODEL, 2 * D_FF), lambda s, te, u: (layer, te[s], 0, 0)),
            pl.BlockSpec((None, None, 1, 2 * D_FF), lambda s, te, u: (layer, te[s], 0, 0)),
            pl.BlockSpec((None, None, D_FF, D_MODEL), lambda s, te, u: (layer, te[s], 0, 0)),
            pl.BlockSpec((None, None, 1, D_MODEL), lambda s, te, u: (layer, te[s], 0, 0)),
        ],
        out_specs=pl.BlockSpec(memory_space=pl.ANY),
        scratch_shapes=[
            pltpu.VMEM((2 * tm, 1, HALF), jnp.uint32),
            pltpu.VMEM((OUT_SLOTS * tm, 1, HALF), jnp.uint32),
            pltpu.VMEM((tm, HALF), jnp.uint32),
            pltpu.VMEM((tm, D_MODEL), F32),
            pltpu.VMEM((D_MODEL, 2 * D_FF), BF16),
            pltpu.VMEM((D_FF, D_MODEL), BF16),
            pltpu.SemaphoreType.DMA((2,)),
            pltpu.SemaphoreType.DMA((OUT_SLOTS,)),
        ],
    )
    return pl.pallas_call(
        functools.partial(_moe_kernel, tm=tm, nt=nt),
        grid_spec=grid_spec,
        out_shape=jax.ShapeDtypeStruct((n_out_rows, 1, HALF), jnp.uint32),
        compiler_params=pltpu.CompilerParams(
            dimension_semantics=("arbitrary",), vmem_limit_bytes=MOE_VMEM_LIMIT),
        name="moe",
    )(tile_expert, used, tok3, tok3, dst3, hf3, w_up, b_up, w_down, b_down)


def _combine_kernel(x_ref, y0_ref, y1_ref, y2_ref, y3_ref, wts_ref, nf_ref, o_ref, y2d, *, tc, final):
    for k, y_ref in enumerate((y0_ref, y1_ref, y2_ref, y3_ref)):
        y2d[k] = y_ref[...].reshape(tc, HALF)
    w = wts_ref[...]
    acc_lo = acc_hi = None
    for k in range(TOP_K):
        lo, hi = _unpack_bf16_pairs(y2d[k])
        acc_lo = w[:, k:k + 1] * lo if k == 0 else acc_lo + w[:, k:k + 1] * lo
        acc_hi = w[:, k:k + 1] * hi if k == 0 else acc_hi + w[:, k:k + 1] * hi
    x = x_ref[...] + jnp.concatenate([acc_lo, acc_hi], axis=-1)
    o_ref[...] = _rms(x, nf_ref[...]) if final else x


def _combine(x, ys3, wts, norm_final, *, tc, final, row0, n_rows):
    n = x.shape[0]
    b0 = row0 // tc
    kb = n // tc
    row = lambda j: (b0 + j, 0)
    y_spec = lambda k: pl.BlockSpec((tc, 1, HALF), lambda j: (k * kb + b0 + j, 0, 0))
    return pl.pallas_call(
        functools.partial(_combine_kernel, tc=tc, final=final),
        grid=(n_rows // tc,),
        in_specs=[
            pl.BlockSpec((tc, D_MODEL), row),
            y_spec(0), y_spec(1), y_spec(2), y_spec(3),
            pl.BlockSpec((tc, LANES), row),
            pl.BlockSpec((1, D_MODEL), lambda j: (0, 0)),
        ],
        out_specs=pl.BlockSpec((tc, D_MODEL), lambda j: (j, 0)),
        out_shape=jax.ShapeDtypeStruct((n_rows, D_MODEL), F32),
        scratch_shapes=[pltpu.VMEM((TOP_K, tc, HALF), jnp.uint32)],
        compiler_params=pltpu.CompilerParams(
            dimension_semantics=("arbitrary",), vmem_limit_bytes=VMEM_LIMIT),
        name="combine_final" if final else "combine",
    )(x, ys3, ys3, ys3, ys3, wts, norm_final)


def _seq_tables(groups, ts):
    pos, length = [], []
    for n_seq, seq_len in groups:
        assert seq_len % ts == 0
        for _ in range(n_seq):
            for t in range(seq_len // ts):
                pos.append(t * ts)
                length.append(seq_len)
    return np.asarray(pos, np.int32), np.asarray(length, np.int32)


def _routing_plan(meta, counts, *, n, tm, nt):
    idx = meta[0:TOP_K]
    pos = meta[TOP_K:2 * TOP_K]
    cnt = counts[0, :N_EXPERTS].astype(jnp.int32)
    tiles_e = (cnt + tm - 1) // tm
    tile_end = jnp.cumsum(tiles_e)
    base = (tile_end - tiles_e) * tm
    base_of = jnp.zeros_like(idx)
    for e in range(N_EXPERTS):
        base_of = jnp.where(idx == e, base[e], base_of)
    slot = (base_of + pos).reshape(-1)
    tile_ids = jnp.arange(nt, dtype=jnp.int32)
    tile_expert = jnp.minimum(jnp.sum(tile_end[None, :] <= tile_ids[:, None], axis=1), N_EXPERTS - 1).astype(jnp.int32)
    pairs = jnp.arange(TOP_K * n, dtype=jnp.int32)
    rows = jnp.arange((nt + 2) * tm, dtype=jnp.int32)
    spare = TOP_K * n + ((rows // tm) % 2) * tm + rows % tm
    dst = spare.at[slot + 2 * tm].set(pairs, unique_indices=True, mode="promise_in_bounds")
    src = dst[2 * tm:]
    tok = jnp.where(src < TOP_K * n, src % n, 0)
    return tile_expert, tile_end[-1:], tok.reshape(nt, 1, tm), dst[:nt * tm].reshape(nt, 1, tm)


def _layer_params(l, norm_mix, w_in, pool_w, pool_scale, conv_w, conv_b, lru_wa, lru_ba, lru_wx, lru_bx,
                  lru_lambda, out_norm_pool, out_norm_lru, w_out, norm_ffn, router_w, router_b,
                  w_up, b_up, w_down, b_down):
    def gates(d):
        return dict(
            wg=jnp.concatenate([lru_wa[l, d], lru_wx[l, d]], axis=-1).astype(BF16),
            ba=lru_ba[l, d].reshape(1, D_LRU),
            bx=lru_bx[l, d].reshape(1, D_LRU),
            sp=jax.nn.softplus(-lru_lambda[l, d]).reshape(1, D_LRU),
        )
    gf, gb = gates(0), gates(1)
    pad = LANES - N_EXPERTS
    rw = jnp.pad(router_w[l], ((0, 0), (0, pad)))
    rw_hi = rw.astype(BF16)
    return dict(
        norm_mix=norm_mix[l].reshape(1, D_MODEL),
        w_in=w_in[l].astype(BF16),
        pool_w=pool_w[l].astype(BF16),
        pool_scale=pool_scale[l].reshape(1, D_POOL),
        conv_w=conv_w[l],
        conv_b=conv_b[l].reshape(1, D_LRU),
        wg_f=gf["wg"], ba_f=gf["ba"], bx_f=gf["bx"], sp_f=gf["sp"],
        wg_b=gb["wg"], ba_b=gb["ba"], bx_b=gb["bx"], sp_b=gb["sp"],
        out_norm_pool=out_norm_pool[l].reshape(1, D_POOL),
        out_norm_lru=out_norm_lru[l].reshape(1, D_LRU),
        w_out=w_out[l].astype(BF16),
        norm_ffn=norm_ffn[l].reshape(1, D_MODEL),
        router_w_hi=rw_hi,
        router_w_lo=(rw - rw_hi.astype(F32)).astype(BF16),
        router_b=jnp.pad(router_b[l].reshape(1, N_EXPERTS), ((0, 0), (0, pad)), constant_values=-jnp.inf),
    )


def _trunk(xs, weights, norm_final, *, ts, tr, tm, tc):
    groups = [(x.shape[0], x.shape[1]) for x in xs]
    sizes = [b * s for b, s in groups]
    n = sum(sizes)
    depth = weights[0].shape[0]
    srcs = [xi.reshape(-1, D_MODEL) for xi in xs]
    tile_pos, tile_len = _seq_tables(groups, ts)
    tile_pos, tile_len = jnp.asarray(tile_pos), jnp.asarray(tile_len)
    nt = (TOP_K * n) // tm + N_EXPERTS + 2
    n_out_rows = TOP_K * n + 2 * tm
    nf = norm_final.reshape(1, D_MODEL)
    w_up, b_up, w_down, b_down = weights[-4:]
    b_up3 = b_up.reshape(depth, N_EXPERTS, 1, 2 * D_FF)
    b_down3 = b_down.reshape(depth, N_EXPERTS, 1, D_MODEL)
    outs = None
    for l in range(depth):
        p = _layer_params(l, *weights)
        ypn, hf, gate, xc = _mix_a(srcs, tile_pos, tile_len, p, ts=ts)
        x = _mix_b(srcs, xc, hf, gate, ypn, tile_pos, tile_len, p, ts=ts)
        hf3, meta, wts, counts = _router(x, p, tr=tr)
        tile_expert, used, tok3, dst3 = _routing_plan(meta, counts, n=n, tm=tm, nt=nt)
        ys3 = _moe(hf3, tile_expert, used, tok3, dst3, w_up, b_up3, w_down, b_down3,
                   layer=l, tm=tm, n_out_rows=n_out_rows)
        if l + 1 < depth:
            x = _combine(x, ys3, wts, nf, tc=tc, final=False, row0=0, n_rows=n)
            srcs = [x]
        else:
            outs, row0 = [], 0
            for (b, s), sz in zip(groups, sizes):
                y = _combine(x, ys3, wts, nf, tc=tc, final=True, row0=row0, n_rows=sz)
                outs.append(y.reshape(b, s, D_MODEL))
                row0 += sz
    return outs


def kernel(x_prompt, x_sample, norm_mix, w_in, pool_w, pool_scale, conv_w, conv_b, lru_wa, lru_ba, lru_wx, lru_bx,
           lru_lambda, out_norm_pool, out_norm_lru, w_out, norm_ffn, router_w, router_b, w_up, b_up, w_down,
           b_down, norm_final):
    weights = (norm_mix, w_in, pool_w, pool_scale, conv_w, conv_b, lru_wa, lru_ba, lru_wx, lru_bx, lru_lambda,
               out_norm_pool, out_norm_lru, w_out, norm_ffn, router_w, router_b, w_up, b_up, w_down, b_down)
    y_prompt, y_sample = _trunk([x_prompt, x_sample], weights, norm_final, ts=512, tr=512, tm=512, tc=256)
    return (y_prompt, y_sample)
```

```python
import functools

import numpy as np
import jax
import jax.numpy as jnp
from jax import lax
from jax.experimental import pallas as pl
from jax.experimental.pallas import tpu as pltpu

D_MODEL = 1024
D_POOL = 512
D_LRU = 512
D_IN = D_POOL + 2 * D_LRU
POOL_WINDOWS = (2, 4, 8, 16)
POOL_GROUP_W = D_POOL // len(POOL_WINDOWS)
N_LRU_HEADS = 4
LRU_BLOCK = D_LRU // N_LRU_HEADS
LRU_C = 8.0
CONV_W = 4
CONV_LEFT = 1
N_EXPERTS = 32
TOP_K = 4
D_FF = D_MODEL
SWIGLU_LIMIT = 7.0
SWIGLU_ALPHA = 1.702
EPS = 1e-6

LANES = 128
SUBLANES = 8
HALO = 16
VMEM_LIMIT = 56 * 1024 * 1024
K_CHUNKS = 2
HALF = D_MODEL // 2
SCAN_UNROLL = 4
OUT_SLOTS = 3

F32 = jnp.float32
BF16 = jnp.bfloat16


def _rms(x, g):
    return x * lax.rsqrt(jnp.mean(x * x, axis=-1, keepdims=True) + EPS) * g


def _pack_bf16_pairs(lo, hi):
    lo_bits = lax.bitcast_convert_type(lo.astype(BF16).astype(F32), jnp.uint32)
    hi_bits = lax.bitcast_convert_type(hi.astype(BF16).astype(F32), jnp.uint32)
    return lax.shift_right_logical(lo_bits, jnp.uint32(16)) | hi_bits


def _unpack_bf16_pairs(w):
    lo = lax.bitcast_convert_type(lax.shift_left(w, jnp.uint32(16)), F32)
    hi = lax.bitcast_convert_type(w & jnp.uint32(0xFFFF0000), F32)
    return lo, hi


def _shift_rows(v, s):
    n = v.shape[0]
    return pltpu.roll(v, s % n, axis=0)


def _lru_gates(xc, wg_ref, ba_ref, bx_ref, sp_ref, a_scr, b_scr):
    for h in range(N_LRU_HEADS):
        cs = slice(h * LRU_BLOCK, (h + 1) * LRU_BLOCK)
        xh = xc[:, cs]
        gr = jnp.dot(xh.astype(BF16), wg_ref[h], preferred_element_type=F32)
        r = jax.nn.sigmoid(gr[:, :LRU_BLOCK] + ba_ref[:, cs])
        i = jax.nn.sigmoid(gr[:, LRU_BLOCK:] + bx_ref[:, cs])
        log_a = -LRU_C * r * sp_ref[:, cs]
        a = jnp.exp(log_a)
        beta = jnp.sqrt(1.0 - a * a)
        a_scr[:, cs] = a
        b_scr[:, cs] = beta * i * xh


def _scan_rows(a_ref, b_ref, h_ref, carry0, n_rows, reverse):
    row = lax.broadcasted_iota(jnp.int32, (SUBLANES, D_LRU), 0)
    n_grp = n_rows // SUBLANES
    n_iter = n_grp // SCAN_UNROLL

    def prefix(r0):
        a = a_ref[pl.ds(r0, SUBLANES), :]
        b = b_ref[pl.ds(r0, SUBLANES), :]
        for s in (1, 2, 4):
            if reverse:
                ok = row < SUBLANES - s
                a_sh = jnp.where(ok, pltpu.roll(a, SUBLANES - s, axis=0), 1.0)
                b_sh = jnp.where(ok, pltpu.roll(b, SUBLANES - s, axis=0), 0.0)
            else:
                ok = row >= s
                a_sh = jnp.where(ok, pltpu.roll(a, s, axis=0), 1.0)
                b_sh = jnp.where(ok, pltpu.roll(b, s, axis=0), 0.0)
            b = a * b_sh + b
            a = a * a_sh
        return a, b

    def body(k, carry):
        first = (n_iter - 1 - k) if reverse else k
        order = range(SCAN_UNROLL - 1, -1, -1) if reverse else range(SCAN_UNROLL)
        starts = [pl.multiple_of((first * SCAN_UNROLL + u) * SUBLANES, SUBLANES) for u in order]
        parts = [prefix(r0) for r0 in starts]
        for r0, (a, b) in zip(starts, parts):
            h = a * carry + b
            h_ref[pl.ds(r0, SUBLANES), :] = h
            carry = h[0:1, :] if reverse else h[SUBLANES - 1:SUBLANES, :]
        return carry

    return lax.fori_loop(0, n_iter, body, carry0)


def _pick_source(g, starts, refs):
    v = refs[0][...]
    for start, ref in zip(starts[1:], refs[1:]):
        v = jnp.where(g >= start, ref[...], v)
    return v


def _mix_a_kernel(pos_ref, len_ref, *refs, ts, starts):
    n_src = len(starts)
    x_refs, xp_refs, xn_refs = refs[0:n_src], refs[n_src:2 * n_src], refs[2 * n_src:3 * n_src]
    (nm_ref, win_ref, pw_ref, ps_ref, cw_ref, cb_ref, wg_ref, ba_ref, bx_ref, sp_ref, onp_ref,
     ypn_ref, hf_ref, gate_ref, xc_ref, h_scr, a_scr, b_scr, carry_ref) = refs[3 * n_src:]
    g = pl.program_id(0)
    pos0 = pos_ref[g]
    slen = len_ref[g]
    keep_prev = jnp.where(pos0 == 0, 0.0, 1.0).astype(F32)
    keep_next = jnp.where(pos0 + ts == slen, 0.0, 1.0).astype(F32)
    nm = nm_ref[...]

    h_scr[0:HALO, :] = (_rms(_pick_source(g, starts, xp_refs), nm) * keep_prev).astype(BF16)
    h_scr[HALO:HALO + ts, :] = _rms(_pick_source(g, starts, x_refs), nm).astype(BF16)
    h_scr[HALO + ts:, :] = (_rms(_pick_source(g, starts, xn_refs), nm) * keep_next).astype(BF16)
    z = jnp.dot(h_scr[...], win_ref[...], preferred_element_type=F32)

    trow = pos0 + lax.broadcasted_iota(jnp.int32, (ts, POOL_GROUP_W), 0)
    ys = []
    for gi, win in enumerate(POOL_WINDOWS):
        u = z[:, gi * POOL_GROUP_W:(gi + 1) * POOL_GROUP_W]
        acc = u + _shift_rows(u, 1)
        half = 1
        while 2 * half < win:
            acc = _shift_rows(acc, half) + _shift_rows(acc, -half)
            half *= 2
        half = win // 2
        cnt = (jnp.minimum(trow + half, slen) - jnp.maximum(trow - half, 0)).astype(F32)
        p = acc[HALO:HALO + ts] / cnt - u[HALO:HALO + ts]
        ys.append(jnp.dot(p.astype(BF16), pw_ref[gi], preferred_element_type=F32))
    y_pool = jnp.concatenate(ys, axis=-1) * ps_ref[...]
    ypn_ref[...] = _rms(y_pool, onp_ref[...]).astype(BF16)

    ul = z[:, D_POOL:D_POOL + D_LRU]
    xc_ext = cb_ref[...] + _shift_rows(ul, CONV_LEFT) * cw_ref[0:1, :]
    xc_ext = xc_ext + ul * cw_ref[1:2, :]
    xc_ext = xc_ext + _shift_rows(ul, -1) * cw_ref[2:3, :]
    xc_ext = xc_ext + _shift_rows(ul, -2) * cw_ref[3:4, :]
    xc = xc_ext[HALO:HALO + ts]
    xc_ref[...] = xc

    gate_ref[...] = jax.nn.gelu(z[HALO:HALO + ts, D_POOL + D_LRU:], approximate=True)

    _lru_gates(xc, wg_ref, ba_ref, bx_ref, sp_ref, a_scr, b_scr)
    carry0 = jnp.where(pos0 == 0, 0.0, carry_ref[...])
    carry_ref[...] = _scan_rows(a_scr, b_scr, hf_ref, carry0, ts, reverse=False)


def _source_starts(xs, ts):
    starts, at = [], 0
    for x in xs:
        starts.append(at)
        at += x.shape[0] // ts
    return tuple(starts), at


def _mix_a(xs, tile_pos, tile_len, p, *, ts):
    starts, g_tiles = _source_starts(xs, ts)
    n = g_tiles * ts
    hb = ts // HALO
    row = lambda g, *_: (g, 0)
    const2 = lambda g, *_: (0, 0)
    const3 = lambda g, *_: (0, 0, 0)

    def cur(start, x):
        return pl.BlockSpec((ts, D_MODEL), lambda g, *_: (jnp.clip(g - start, 0, x.shape[0] // ts - 1), 0))

    def prev(start, x):
        return pl.BlockSpec((HALO, D_MODEL),
                            lambda g, *_: (jnp.clip((g - start) * hb - 1, 0, x.shape[0] // HALO - 1), 0))

    def nxt(start, x):
        return pl.BlockSpec((HALO, D_MODEL),
                            lambda g, *_: (jnp.clip((g - start + 1) * hb, 0, x.shape[0] // HALO - 1), 0))

    grid_spec = pltpu.PrefetchScalarGridSpec(
        num_scalar_prefetch=2,
        grid=(g_tiles,),
        in_specs=[
            *[cur(st, x) for st, x in zip(starts, xs)],
            *[prev(st, x) for st, x in zip(starts, xs)],
            *[nxt(st, x) for st, x in zip(starts, xs)],
            pl.BlockSpec((1, D_MODEL), const2),
            pl.BlockSpec((D_MODEL, D_IN), const2),
            pl.BlockSpec((len(POOL_WINDOWS), POOL_GROUP_W, POOL_GROUP_W), const3),
            pl.BlockSpec((1, D_POOL), const2),
            pl.BlockSpec((CONV_W, D_LRU), const2),
            pl.BlockSpec((1, D_LRU), const2),
            pl.BlockSpec((N_LRU_HEADS, LRU_BLOCK, 2 * LRU_BLOCK), const3),
            pl.BlockSpec((1, D_LRU), const2),
            pl.BlockSpec((1, D_LRU), const2),
            pl.BlockSpec((1, D_LRU), const2),
            pl.BlockSpec((1, D_POOL), const2),
        ],
        out_specs=[
            pl.BlockSpec((ts, D_POOL), row),
            pl.BlockSpec((ts, D_LRU), row),
            pl.BlockSpec((ts, D_LRU), row),
            pl.BlockSpec((ts, D_LRU), row),
        ],
        scratch_shapes=[
            pltpu.VMEM((ts + 2 * HALO, D_MODEL), BF16),
            pltpu.VMEM((ts, D_LRU), F32),
            pltpu.VMEM((ts, D_LRU), F32),
            pltpu.VMEM((1, D_LRU), F32),
        ],
    )
    return pl.pallas_call(
        functools.partial(_mix_a_kernel, ts=ts, starts=starts),
        grid_spec=grid_spec,
        out_shape=[
            jax.ShapeDtypeStruct((n, D_POOL), BF16),
            jax.ShapeDtypeStruct((n, D_LRU), F32),
            jax.ShapeDtypeStruct((n, D_LRU), F32),
            jax.ShapeDtypeStruct((n, D_LRU), F32),
        ],
        compiler_params=pltpu.CompilerParams(
            dimension_semantics=("arbitrary",), vmem_limit_bytes=VMEM_LIMIT),
        name="mix_a",
    )(tile_pos, tile_len, *xs, *xs, *xs, p["norm_mix"], p["w_in"], p["pool_w"], p["pool_scale"],
      p["conv_w"], p["conv_b"], p["wg_f"], p["ba_f"], p["bx_f"], p["sp_f"], p["out_norm_pool"])


def _mix_b_kernel(pos_ref, len_ref, *refs, ts, g_tiles, starts):
    n_src = len(starts)
    x_refs = refs[0:n_src]
    (xc_ref, hf_ref, gate_ref, ypn_ref, wg_ref, ba_ref, bx_ref, sp_ref, onl_ref, wout_ref,
     o_ref, a_scr, b_scr, hb_scr, carry_ref) = refs[n_src:]
    g = g_tiles - 1 - pl.program_id(0)
    at_end = pos_ref[g] + ts == len_ref[g]
    _lru_gates(xc_ref[...], wg_ref, ba_ref, bx_ref, sp_ref, a_scr, b_scr)
    carry0 = jnp.where(at_end, 0.0, carry_ref[...])
    carry_ref[...] = _scan_rows(a_scr, b_scr, hb_scr, carry0, ts, reverse=True)
    y_lru = (hf_ref[...] + hb_scr[...]) * gate_ref[...]
    yln = _rms(y_lru, onl_ref[...]).astype(BF16)
    out = jnp.dot(ypn_ref[...], wout_ref[0:D_POOL, :], preferred_element_type=F32)
    out = out + jnp.dot(yln, wout_ref[D_POOL:, :], preferred_element_type=F32)
    o_ref[...] = _pick_source(g, starts, x_refs) + out


def _mix_b(xs, xc, hf, gate, ypn, tile_pos, tile_len, p, *, ts):
    starts, g_tiles = _source_starts(xs, ts)
    n = g_tiles * ts
    row = lambda g, *_: (g_tiles - 1 - g, 0)

    def cur(start, x):
        return pl.BlockSpec((ts, D_MODEL),
                            lambda g, *_: (jnp.clip(g_tiles - 1 - g - start, 0, x.shape[0] // ts - 1), 0))

    const2 = lambda g, *_: (0, 0)
    const3 = lambda g, *_: (0, 0, 0)
    grid_spec = pltpu.PrefetchScalarGridSpec(
        num_scalar_prefetch=2,
        grid=(g_tiles,),
        in_specs=[
            *[cur(st, x) for st, x in zip(starts, xs)],
            pl.BlockSpec((ts, D_LRU), row),
            pl.BlockSpec((ts, D_LRU), row),
            pl.BlockSpec((ts, D_LRU), row),
            pl.BlockSpec((ts, D_POOL), row),
            pl.BlockSpec((N_LRU_HEADS, LRU_BLOCK, 2 * LRU_BLOCK), const3),
            pl.BlockSpec((1, D_LRU), const2),
            pl.BlockSpec((1, D_LRU), const2),
            pl.BlockSpec((1, D_LRU), const2),
            pl.BlockSpec((1, D_LRU), const2),
            pl.BlockSpec((D_POOL + D_LRU, D_MODEL), const2),
        ],
        out_specs=pl.BlockSpec((ts, D_MODEL), row),
        scratch_shapes=[
            pltpu.VMEM((ts, D_LRU), F32),
            pltpu.VMEM((ts, D_LRU), F32),
            pltpu.VMEM((ts, D_LRU), F32),
            pltpu.VMEM((1, D_LRU), F32),
        ],
    )
    return pl.pallas_call(
        functools.partial(_mix_b_kernel, ts=ts, g_tiles=g_tiles, starts=starts),
        grid_spec=grid_spec,
        out_shape=jax.ShapeDtypeStruct((n, D_MODEL), F32),
        compiler_params=pltpu.CompilerParams(
            dimension_semantics=("arbitrary",), vmem_limit_bytes=VMEM_LIMIT),
        name="mix_b",
    )(tile_pos, tile_len, *xs, xc, hf, gate, ypn, p["wg_b"], p["ba_b"], p["bx_b"], p["sp_b"],
      p["out_norm_lru"], p["w_out"])


def _router_kernel(x_ref, nf_ref, rwh_ref, rwl_ref, rb_ref,
                   hf_ref, meta_ref, wts_ref, cnt_ref,
                   carry_ref, *, tr):
    @pl.when(pl.program_id(0) == 0)
    def _():
        carry_ref[...] = jnp.zeros_like(carry_ref)

    hf = _rms(x_ref[...], nf_ref[...])
    hf_ref[...] = _pack_bf16_pairs(hf[:, :HALF], hf[:, HALF:]).reshape(tr, 1, HALF)
    h_hi = hf.astype(BF16)
    h_lo = (hf - h_hi.astype(F32)).astype(BF16)
    logits = (jnp.dot(h_hi, rwh_ref[...], preferred_element_type=F32)
              + jnp.dot(h_lo, rwh_ref[...], preferred_element_type=F32)
              + jnp.dot(h_hi, rwl_ref[...], preferred_element_type=F32)) + rb_ref[...]
    lane = lax.broadcasted_iota(jnp.int32, (tr, LANES), 1).astype(F32)

    vals, idxs = [], []
    l = logits
    for _ in range(TOP_K):
        m = jnp.max(l, axis=-1, keepdims=True)
        ik = jnp.min(jnp.where(l == m, lane, float(LANES)), axis=-1, keepdims=True)
        vals.append(m)
        idxs.append(ik)
        l = jnp.where(lane == ik, -jnp.inf, l)
    es = [jnp.exp(v - vals[0]) for v in vals]
    den = es[0] + es[1] + es[2] + es[3]

    hits = [lane == ik for ik in idxs]
    chosen = jnp.zeros((tr, LANES), F32)
    for hit in hits:
        chosen = chosen + jnp.where(hit, 1.0, 0.0)
    tri = jnp.where(lax.broadcasted_iota(jnp.int32, (tr, tr), 1) < lax.broadcasted_iota(jnp.int32, (tr, tr), 0),
                    1.0, 0.0).astype(BF16)
    before = jnp.dot(tri, chosen.astype(BF16), preferred_element_type=F32) + carry_ref[...]
    total = carry_ref[...] + jnp.sum(chosen, axis=0, keepdims=True)
    carry_ref[...] = total
    cnt_ref[...] = total

    meta = jnp.zeros((tr, LANES), F32)
    wts = jnp.zeros((tr, LANES), F32)
    for k in range(TOP_K):
        pos_k = jnp.sum(jnp.where(hits[k], before, 0.0), axis=-1, keepdims=True)
        meta = jnp.where(lane == float(k), idxs[k], meta)
        meta = jnp.where(lane == float(TOP_K + k), pos_k, meta)
        wts = jnp.where(lane == float(k), es[k] / den, wts)
    meta_ref[...] = jnp.transpose(meta)[0:2 * TOP_K, :].astype(jnp.int32)
    wts_ref[...] = wts


def _router(x, p, *, tr):
    n = x.shape[0]
    row = lambda j: (j, 0)
    const2 = lambda j: (0, 0)
    return pl.pallas_call(
        functools.partial(_router_kernel, tr=tr),
        grid=(n // tr,),
        in_specs=[
            pl.BlockSpec((tr, D_MODEL), row),
            pl.BlockSpec((1, D_MODEL), const2),
            pl.BlockSpec((D_MODEL, LANES), const2),
            pl.BlockSpec((D_MODEL, LANES), const2),
            pl.BlockSpec((1, LANES), const2),
        ],
        out_specs=[
            pl.BlockSpec((tr, 1, HALF), lambda j: (j, 0, 0)),
            pl.BlockSpec((2 * TOP_K, tr), lambda j: (0, j)),
            pl.BlockSpec((tr, LANES), row),
            pl.BlockSpec((1, LANES), const2),
        ],
        out_shape=[
            jax.ShapeDtypeStruct((n, 1, HALF), jnp.uint32),
            jax.ShapeDtypeStruct((2 * TOP_K, n), jnp.int32),
            jax.ShapeDtypeStruct((n, LANES), F32),
            jax.ShapeDtypeStruct((1, LANES), F32),
        ],
        scratch_shapes=[pltpu.VMEM((1, LANES), F32)],
        compiler_params=pltpu.CompilerParams(
            dimension_semantics=("arbitrary",), vmem_limit_bytes=VMEM_LIMIT),
        name="router",
    )(x, p["norm_ffn"], p["router_w_hi"], p["router_w_lo"], p["router_b"])


def _moe_kernel(te_ref, used_ref,
                tok0_ref, tokn_ref, dst_ref, hf_hbm, wup_ref, bup_ref, wdn_ref, bdn_ref,
                ys_hbm,
                xbuf, obuf, x2d, o2d, wup_bf, wdn_bf, gsem, ssem, *, tm, nt):
    s = pl.program_id(0)
    slot = lax.rem(s, 2)
    other = 1 - slot
    o_fill = lax.rem(s + (OUT_SLOTS - 1), OUT_SLOTS)
    o_send = lax.rem(s + (OUT_SLOTS - 2), OUT_SLOTS)

    def gather_wait(b):
        pltpu.make_async_copy(hf_hbm.at[pl.ds(0, tm)], xbuf.at[pl.ds(b * tm, tm)], gsem.at[b]).wait()

    def scatter_wait(b):
        pltpu.make_async_copy(obuf.at[pl.ds(b * tm, tm)], ys_hbm.at[pl.ds(0, tm)], ssem.at[b]).wait()

    def gather_row(tok, b, r, prio=0):
        pltpu.make_async_copy(hf_hbm.at[pl.ds(tok, 1)], xbuf.at[pl.ds(b * tm + r, 1)], gsem.at[b]).start(priority=prio)

    def scatter_row(dst, b, r, prio=0):
        pltpu.make_async_copy(obuf.at[pl.ds(b * tm + r, 1)], ys_hbm.at[pl.ds(dst, 1)], ssem.at[b]).start(priority=prio)

    @pl.when(s == 0)
    def _():
        obuf[...] = jnp.zeros_like(obuf)
        o2d[...] = jnp.zeros_like(o2d)

        def body(r, c):
            gather_row(tok0_ref[0, 0, r], 0, r)
            return c
        lax.fori_loop(0, tm, body, 0)

    last = used_ref[0] + 1

    @pl.when(s <= last)
    def _():
        @pl.when(s >= OUT_SLOTS - 1)
        def _():
            scatter_wait(o_fill)

        gather_wait(slot)

        @pl.when(jnp.logical_or(s == 0, te_ref[s] != te_ref[jnp.maximum(s - 1, 0)]))
        def _():
            wup_bf[...] = wup_ref[...].astype(BF16)
            wdn_bf[...] = wdn_ref[...].astype(BF16)

        xrows = xbuf.at[pl.ds(pl.multiple_of(slot * tm, tm), tm)]
        orows = obuf.at[pl.ds(pl.multiple_of(o_fill * tm, tm), tm)]
        orows[...] = _pack_bf16_pairs(o2d[:, :HALF], o2d[:, HALF:]).reshape(tm, 1, HALF)
        kw = HALF // K_CHUNKS
        bounds = [(tm * g) // (K_CHUNKS - 1) for g in range(K_CHUNKS)] + [tm]
        hh = bup_ref[...]
        for kc in range(K_CHUNKS):
            cols = slice(kc * kw, (kc + 1) * kw)
            x2d[:, cols] = xrows[:, :, cols].reshape(tm, kw)
            for r in range(bounds[kc], bounds[kc + 1]):
                gather_row(tokn_ref[0, 0, r], other, r, r % 2)
                scatter_row(dst_ref[0, 0, r], o_send, r, (r + 1) % 2)
            lo, hi = _unpack_bf16_pairs(x2d[:, cols])
            hh = hh + jnp.dot(lo.astype(BF16), wup_bf[kc * kw:(kc + 1) * kw, :], preferred_element_type=F32)
            hh = hh + jnp.dot(hi.astype(BF16), wup_bf[HALF + kc * kw:HALF + (kc + 1) * kw, :],
                              preferred_element_type=F32)
        x_glu = jnp.minimum(hh[:, :D_FF], SWIGLU_LIMIT)
        x_lin = jnp.clip(hh[:, D_FF:], -SWIGLU_LIMIT, SWIGLU_LIMIT)
        act = (x_glu * jax.nn.sigmoid(SWIGLU_ALPHA * x_glu) * (x_lin + 1.0)).astype(BF16)
        o2d[...] = jnp.dot(act, wdn_bf[...], preferred_element_type=F32) + bdn_ref[...]


        @pl.when(s == last)
        def _():
            scatter_wait(o_send)
            scatter_wait(lax.rem(s + (OUT_SLOTS - 3), OUT_SLOTS))
            gather_wait(other)


def _moe(hf3, tile_expert, used, tok3, dst3, w_up, b_up, w_down, b_down, *, layer, tm, n_out_rows):
    nt = tok3.shape[0]
    smem_blk = lambda fn: pl.BlockSpec((1, 1, tm), fn, memory_space=pltpu.SMEM)
    grid_spec = pltpu.PrefetchScalarGridSpec(
        num_scalar_prefetch=2,
        grid=(nt,),
        in_specs=[
            smem_blk(lambda s, te, u: (0, 0, 0)),
            smem_blk(lambda s, te, u: (jnp.minimum(s + 1, nt - 1), 0, 0)),
            smem_blk(lambda s, te, u: (s, 0, 0)),
            pl.BlockSpec(memory_space=pl.ANY),
            pl.BlockSpec((None, None, D_MODEL, 2 * D_FF), lambda s, te, u: (layer, te[s], 0, 0)),
            pl.BlockSpec((None, None, 1, 2 * D_FF), lambda s, te, u: (layer, te[s], 0, 0)),
            pl.BlockSpec((None, None, D_FF, D_MODEL), lambda s, te, u: (layer, te[s], 0, 0)),
            pl.BlockSpec((None, None, 1, D_MODEL), lambda s, te, u: (layer, te[s], 0, 0)),
        ],
        out_specs=pl.BlockSpec(memory_space=pl.ANY),
        scratch_shapes=[
            pltpu.VMEM((2 * tm, 1, HALF), jnp.uint32),
            pltpu.VMEM((OUT_SLOTS * tm, 1, HALF), jnp.uint32),
            pltpu.VMEM((tm, HALF), jnp.uint32),
            pltpu.VMEM((tm, D_MODEL), F32),
            pltpu.VMEM((D_MODEL, 2 * D_FF), BF16),
            pltpu.VMEM((D_FF, D_MODEL), BF16),
            pltpu.SemaphoreType.DMA((2,)),
            pltpu.SemaphoreType.DMA((OUT_SLOTS,)),
        ],
    )
    return pl.pallas_call(
        functools.partial(_moe_kernel, tm=tm, nt=nt),
        grid_spec=grid_spec,
        out_shape=jax.ShapeDtypeStruct((n_out_rows, 1, HALF), jnp.uint32),
        compiler_params=pltpu.CompilerParams(
            dimension_semantics=("arbitrary",), vmem_limit_bytes=VMEM_LIMIT),
        name="moe",
    )(tile_expert, used, tok3, tok3, dst3, hf3, w_up, b_up, w_down, b_down)


def _combine_kernel(x_ref, y0_ref, y1_ref, y2_ref, y3_ref, wts_ref, nf_ref, o_ref, y2d, *, tc, final):
    for k, y_ref in enumerate((y0_ref, y1_ref, y2_ref, y3_ref)):
        y2d[k] = y_ref[...].reshape(tc, HALF)
    w = wts_ref[...]
    acc_lo = acc_hi = None
    for k in range(TOP_K):
        lo, hi = _unpack_bf16_pairs(y2d[k])
        acc_lo = w[:, k:k + 1] * lo if k == 0 else acc_lo + w[:, k:k + 1] * lo
        acc_hi = w[:, k:k + 1] * hi if k == 0 else acc_hi + w[:, k:k + 1] * hi
    x = x_ref[...] + jnp.concatenate([acc_lo, acc_hi], axis=-1)
    o_ref[...] = _rms(x, nf_ref[...]) if final else x


def _combine(x, ys3, wts, norm_final, *, tc, final, row0, n_rows):
    n = x.shape[0]
    b0 = row0 // tc
    kb = n // tc
    row = lambda j: (b0 + j, 0)
    y_spec = lambda k: pl.BlockSpec((tc, 1, HALF), lambda j: (k * kb + b0 + j, 0, 0))
    return pl.pallas_call(
        functools.partial(_combine_kernel, tc=tc, final=final),
        grid=(n_rows // tc,),
        in_specs=[
            pl.BlockSpec((tc, D_MODEL), row),
            y_spec(0), y_spec(1), y_spec(2), y_spec(3),
            pl.BlockSpec((tc, LANES), row),
            pl.BlockSpec((1, D_MODEL), lambda j: (0, 0)),
        ],
        out_specs=pl.BlockSpec((tc, D_MODEL), lambda j: (j, 0)),
        out_shape=jax.ShapeDtypeStruct((n_rows, D_MODEL), F32),
        scratch_shapes=[pltpu.VMEM((TOP_K, tc, HALF), jnp.uint32)],
        compiler_params=pltpu.CompilerParams(
            dimension_semantics=("arbitrary",), vmem_limit_bytes=VMEM_LIMIT),
        name="combine_final" if final else "combine",
    )(x, ys3, ys3, ys3, ys3, wts, norm_final)


def _seq_tables(groups, ts):
    pos, length = [], []
    for n_seq, seq_len in groups:
        assert seq_len % ts == 0
        for _ in range(n_seq):
            for t in range(seq_len // ts):
                pos.append(t * ts)
                length.append(seq_len)
    return np.asarray(pos, np.int32), np.asarray(length, np.int32)


def _routing_plan(meta, counts, *, n, tm, nt):
    idx = meta[0:TOP_K]
    pos = meta[TOP_K:2 * TOP_K]
    cnt = counts[0, :N_EXPERTS].astype(jnp.int32)
    tiles_e = (cnt + tm - 1) // tm
    tile_end = jnp.cumsum(tiles_e)
    base = (tile_end - tiles_e) * tm
    base_of = jnp.zeros_like(idx)
    for e in range(N_EXPERTS):
        base_of = jnp.where(idx == e, base[e], base_of)
    slot = (base_of + pos).reshape(-1)
    tile_ids = jnp.arange(nt, dtype=jnp.int32)
    tile_expert = jnp.minimum(jnp.sum(tile_end[None, :] <= tile_ids[:, None], axis=1), N_EXPERTS - 1).astype(jnp.int32)
    pairs = jnp.arange(TOP_K * n, dtype=jnp.int32)
    rows = jnp.arange((nt + 2) * tm, dtype=jnp.int32)
    spare = TOP_K * n + ((rows // tm) % 2) * tm + rows % tm
    dst = spare.at[slot + 2 * tm].set(pairs, unique_indices=True, mode="promise_in_bounds")
    src = dst[2 * tm:]
    tok = jnp.where(src < TOP_K * n, src % n, 0)
    return tile_expert, tile_end[-1:], tok.reshape(nt, 1, tm), dst[:nt * tm].reshape(nt, 1, tm)


def _layer_params(l, norm_mix, w_in, pool_w, pool_scale, conv_w, conv_b, lru_wa, lru_ba, lru_wx, lru_bx,
                  lru_lambda, out_norm_pool, out_norm_lru, w_out, norm_ffn, router_w, router_b,
                  w_up, b_up, w_down, b_down):
    def gates(d):
        return dict(
            wg=jnp.concatenate([lru_wa[l, d], lru_wx[l, d]], axis=-1).astype(BF16),
            ba=lru_ba[l, d].reshape(1, D_LRU),
            bx=lru_bx[l, d].reshape(1, D_LRU),
            sp=jax.nn.softplus(-lru_lambda[l, d]).reshape(1, D_LRU),
        )
    gf, gb = gates(0), gates(1)
    pad = LANES - N_EXPERTS
    rw = jnp.pad(router_w[l], ((0, 0), (0, pad)))
    rw_hi = rw.astype(BF16)
    return dict(
        norm_mix=norm_mix[l].reshape(1, D_MODEL),
        w_in=w_in[l].astype(BF16),
        pool_w=pool_w[l].astype(BF16),
        pool_scale=pool_scale[l].reshape(1, D_POOL),
        conv_w=conv_w[l],
        conv_b=conv_b[l].reshape(1, D_LRU),
        wg_f=gf["wg"], ba_f=gf["ba"], bx_f=gf["bx"], sp_f=gf["sp"],
        wg_b=gb["wg"], ba_b=gb["ba"], bx_b=gb["bx"], sp_b=gb["sp"],
        out_norm_pool=out_norm_pool[l].reshape(1, D_POOL),
        out_norm_lru=out_norm_lru[l].reshape(1, D_LRU),
        w_out=w_out[l].astype(BF16),
        norm_ffn=norm_ffn[l].reshape(1, D_MODEL),
        router_w_hi=rw_hi,
        router_w_lo=(rw - rw_hi.astype(F32)).astype(BF16),
        router_b=jnp.pad(router_b[l].reshape(1, N_EXPERTS), ((0, 0), (0, pad)), constant_values=-jnp.inf),
    )


def _trunk(xs, weights, norm_final, *, ts, tr, tm, tc):
    groups = [(x.shape[0], x.shape[1]) for x in xs]
    sizes = [b * s for b, s in groups]
    n = sum(sizes)
    depth = weights[0].shape[0]
    srcs = [xi.reshape(-1, D_MODEL) for xi in xs]
    tile_pos, tile_len = _seq_tables(groups, ts)
    tile_pos, tile_len = jnp.asarray(tile_pos), jnp.asarray(tile_len)
    nt = (TOP_K * n) // tm + N_EXPERTS + 2
    n_out_rows = TOP_K * n + 2 * tm
    nf = norm_final.reshape(1, D_MODEL)
    w_up, b_up, w_down, b_down = weights[-4:]
    b_up3 = b_up.reshape(depth, N_EXPERTS, 1, 2 * D_FF)
    b_down3 = b_down.reshape(depth, N_EXPERTS, 1, D_MODEL)
    outs = None
    for l in range(depth):
        p = _layer_params(l, *weights)
        ypn, hf, gate, xc = _mix_a(srcs, tile_pos, tile_len, p, ts=ts)
        x = _mix_b(srcs, xc, hf, gate, ypn, tile_pos, tile_len, p, ts=ts)
        hf3, meta, wts, counts = _router(x, p, tr=tr)
        tile_expert, used, tok3, dst3 = _routing_plan(meta, counts, n=n, tm=tm, nt=nt)
        ys3 = _moe(hf3, tile_expert, used, tok3, dst3, w_up, b_up3, w_down, b_down3,
                   layer=l, tm=tm, n_out_rows=n_out_rows)
        if l + 1 < depth:
            x = _combine(x, ys3, wts, nf, tc=tc, final=False, row0=0, n_rows=n)
            srcs = [x]
        else:
            outs, row0 = [], 0
            for (b, s), sz in zip(groups, sizes):
                y = _combine(x, ys3, wts, nf, tc=tc, final=True, row0=row0, n_rows=sz)
                outs.append(y.reshape(b, s, D_MODEL))
                row0 += sz
    return outs


def kernel(x_prompt, x_sample, norm_mix, w_in, pool_w, pool_scale, conv_w, conv_b, lru_wa, lru_ba, lru_wx, lru_bx,
           lru_lambda, out_norm_pool, out_norm_lru, w_out, norm_ffn, router_w, router_b, w_up, b_up, w_down,
           b_down, norm_final):
    weights = (norm_mix, w_in, pool_w, pool_scale, conv_w, conv_b, lru_wa, lru_ba, lru_wx, lru_bx, lru_lambda,
               out_norm_pool, out_norm_lru, w_out, norm_ffn, router_w, router_b, w_up, b_up, w_down, b_down)
    y_prompt, y_sample = _trunk([x_prompt, x_sample], weights, norm_final, ts=1024, tr=512, tm=512, tc=512)
    return (y_prompt, y_sample)
```

```python
import functools

import numpy as np
import jax
import jax.numpy as jnp
from jax import lax
from jax.experimental import pallas as pl
from jax.experimental.pallas import tpu as pltpu

D_MODEL = 1024
D_POOL = 512
D_LRU = 512
D_IN = D_POOL + 2 * D_LRU
POOL_WINDOWS = (2, 4, 8, 16)
POOL_GROUP_W = D_POOL // len(POOL_WINDOWS)
N_LRU_HEADS = 4
LRU_BLOCK = D_LRU // N_LRU_HEADS
LRU_C = 8.0
CONV_W = 4
CONV_LEFT = 1
N_EXPERTS = 32
TOP_K = 4
D_FF = D_MODEL
SWIGLU_LIMIT = 7.0
SWIGLU_ALPHA = 1.702
EPS = 1e-6

LANES = 128
SUBLANES = 8
HALO = 16
VMEM_LIMIT = 56 * 1024 * 1024
K_CHUNKS = 2
HALF = D_MODEL // 2
SCAN_UNROLL = 4
OUT_SLOTS = 3

F32 = jnp.float32
BF16 = jnp.bfloat16


def _rms(x, g):
    return x * lax.rsqrt(jnp.mean(x * x, axis=-1, keepdims=True) + EPS) * g


def _pack_bf16_pairs(lo, hi):
    lo_bits = lax.bitcast_convert_type(lo.astype(BF16).astype(F32), jnp.uint32)
    hi_bits = lax.bitcast_convert_type(hi.astype(BF16).astype(F32), jnp.uint32)
    return lax.shift_right_logical(lo_bits, jnp.uint32(16)) | hi_bits


def _unpack_bf16_pairs(w):
    lo = lax.bitcast_convert_type(lax.shift_left(w, jnp.uint32(16)), F32)
    hi = lax.bitcast_convert_type(w & jnp.uint32(0xFFFF0000), F32)
    return lo, hi


def _shift_rows(v, s):
    n = v.shape[0]
    return pltpu.roll(v, s % n, axis=0)


def _lru_gates(xc, wg_ref, ba_ref, bx_ref, sp_ref, a_scr, b_scr):
    for h in range(N_LRU_HEADS):
        cs = slice(h * LRU_BLOCK, (h + 1) * LRU_BLOCK)
        xh = xc[:, cs]
        gr = jnp.dot(xh.astype(BF16), wg_ref[h], preferred_element_type=F32)
        r = jax.nn.sigmoid(gr[:, :LRU_BLOCK] + ba_ref[:, cs])
        i = jax.nn.sigmoid(gr[:, LRU_BLOCK:] + bx_ref[:, cs])
        log_a = -LRU_C * r * sp_ref[:, cs]
        a = jnp.exp(log_a)
        beta = jnp.sqrt(1.0 - a * a)
        a_scr[:, cs] = a
        b_scr[:, cs] = beta * i * xh


def _scan_rows(a_ref, b_ref, h_ref, carry0, n_rows, reverse):
    row = lax.broadcasted_iota(jnp.int32, (SUBLANES, D_LRU), 0)
    n_grp = n_rows // SUBLANES
    n_iter = n_grp // SCAN_UNROLL

    def prefix(r0):
        a = a_ref[pl.ds(r0, SUBLANES), :]
        b = b_ref[pl.ds(r0, SUBLANES), :]
        for s in (1, 2, 4):
            if reverse:
                ok = row < SUBLANES - s
                a_sh = jnp.where(ok, pltpu.roll(a, SUBLANES - s, axis=0), 1.0)
                b_sh = jnp.where(ok, pltpu.roll(b, SUBLANES - s, axis=0), 0.0)
            else:
                ok = row >= s
                a_sh = jnp.where(ok, pltpu.roll(a, s, axis=0), 1.0)
                b_sh = jnp.where(ok, pltpu.roll(b, s, axis=0), 0.0)
            b = a * b_sh + b
            a = a * a_sh
        return a, b

    def body(k, carry):
        first = (n_iter - 1 - k) if reverse else k
        order = range(SCAN_UNROLL - 1, -1, -1) if reverse else range(SCAN_UNROLL)
        starts = [pl.multiple_of((first * SCAN_UNROLL + u) * SUBLANES, SUBLANES) for u in order]
        parts = [prefix(r0) for r0 in starts]
        for r0, (a, b) in zip(starts, parts):
            h = a * carry + b
            h_ref[pl.ds(r0, SUBLANES), :] = h
            carry = h[0:1, :] if reverse else h[SUBLANES - 1:SUBLANES, :]
        return carry

    return lax.fori_loop(0, n_iter, body, carry0)


def _pick_source(g, starts, refs):
    v = refs[0][...]
    for start, ref in zip(starts[1:], refs[1:]):
        v = jnp.where(g >= start, ref[...], v)
    return v


def _mix_a_kernel(pos_ref, len_ref, *refs, ts, starts):
    n_src = len(starts)
    x_refs, xp_refs, xn_refs = refs[0:n_src], refs[n_src:2 * n_src], refs[2 * n_src:3 * n_src]
    (nm_ref, win_ref, pw_ref, ps_ref, cw_ref, cb_ref, wg_ref, ba_ref, bx_ref, sp_ref, onp_ref,
     ypn_ref, hf_ref, gate_ref, xc_ref, h_scr, a_scr, b_scr, carry_ref) = refs[3 * n_src:]
    g = pl.program_id(0)
    pos0 = pos_ref[g]
    slen = len_ref[g]
    keep_prev = jnp.where(pos0 == 0, 0.0, 1.0).astype(F32)
    keep_next = jnp.where(pos0 + ts == slen, 0.0, 1.0).astype(F32)
    nm = nm_ref[...]

    h_scr[0:HALO, :] = (_rms(_pick_source(g, starts, xp_refs), nm) * keep_prev).astype(BF16)
    h_scr[HALO:HALO + ts, :] = _rms(_pick_source(g, starts, x_refs), nm).astype(BF16)
    h_scr[HALO + ts:, :] = (_rms(_pick_source(g, starts, xn_refs), nm) * keep_next).astype(BF16)
    z = jnp.dot(h_scr[...], win_ref[...], preferred_element_type=F32)

    trow = pos0 + lax.broadcasted_iota(jnp.int32, (ts, POOL_GROUP_W), 0)
    ys = []
    for gi, win in enumerate(POOL_WINDOWS):
        u = z[:, gi * POOL_GROUP_W:(gi + 1) * POOL_GROUP_W]
        acc = u + _shift_rows(u, 1)
        half = 1
        while 2 * half < win:
            acc = _shift_rows(acc, half) + _shift_rows(acc, -half)
            half *= 2
        half = win // 2
        cnt = (jnp.minimum(trow + half, slen) - jnp.maximum(trow - half, 0)).astype(F32)
        p = acc[HALO:HALO + ts] / cnt - u[HALO:HALO + ts]
        ys.append(jnp.dot(p.astype(BF16), pw_ref[gi], preferred_element_type=F32))
    y_pool = jnp.concatenate(ys, axis=-1) * ps_ref[...]
    ypn_ref[...] = _rms(y_pool, onp_ref[...]).astype(BF16)

    ul = z[:, D_POOL:D_POOL + D_LRU]
    xc_ext = cb_ref[...] + _shift_rows(ul, CONV_LEFT) * cw_ref[0:1, :]
    xc_ext = xc_ext + ul * cw_ref[1:2, :]
    xc_ext = xc_ext + _shift_rows(ul, -1) * cw_ref[2:3, :]
    xc_ext = xc_ext + _shift_rows(ul, -2) * cw_ref[3:4, :]
    xc = xc_ext[HALO:HALO + ts]
    xc_ref[...] = xc

    gate_ref[...] = jax.nn.gelu(z[HALO:HALO + ts, D_POOL + D_LRU:], approximate=True)

    _lru_gates(xc, wg_ref, ba_ref, bx_ref, sp_ref, a_scr, b_scr)
    carry0 = jnp.where(pos0 == 0, 0.0, carry_ref[...])
    carry_ref[...] = _scan_rows(a_scr, b_scr, hf_ref, carry0, ts, reverse=False)


def _source_starts(xs, ts):
    starts, at = [], 0
    for x in xs:
        starts.append(at)
        at += x.shape[0] // ts
    return tuple(starts), at


def _mix_a(xs, tile_pos, tile_len, p, *, ts):
    starts, g_tiles = _source_starts(xs, ts)
    n = g_tiles * ts
    hb = ts // HALO
    row = lambda g, *_: (g, 0)
    const2 = lambda g, *_: (0, 0)
    const3 = lambda g, *_: (0, 0, 0)

    def cur(start, x):
        return pl.BlockSpec((ts, D_MODEL), lambda g, *_: (jnp.clip(g - start, 0, x.shape[0] // ts - 1), 0))

    def prev(start, x):
        return pl.BlockSpec((HALO, D_MODEL),
                            lambda g, *_: (jnp.clip((g - start) * hb - 1, 0, x.shape[0] // HALO - 1), 0))

    def nxt(start, x):
        return pl.BlockSpec((HALO, D_MODEL),
                            lambda g, *_: (jnp.clip((g - start + 1) * hb, 0, x.shape[0] // HALO - 1), 0))

    grid_spec = pltpu.PrefetchScalarGridSpec(
        num_scalar_prefetch=2,
        grid=(g_tiles,),
        in_specs=[
            *[cur(st, x) for st, x in zip(starts, xs)],
            *[prev(st, x) for st, x in zip(starts, xs)],
            *[nxt(st, x) for st, x in zip(starts, xs)],
            pl.BlockSpec((1, D_MODEL), const2),
            pl.BlockSpec((D_MODEL, D_IN), const2),
            pl.BlockSpec((len(POOL_WINDOWS), POOL_GROUP_W, POOL_GROUP_W), const3),
            pl.BlockSpec((1, D_POOL), const2),
            pl.BlockSpec((CONV_W, D_LRU), const2),
            pl.BlockSpec((1, D_LRU), const2),
            pl.BlockSpec((N_LRU_HEADS, LRU_BLOCK, 2 * LRU_BLOCK), const3),
            pl.BlockSpec((1, D_LRU), const2),
            pl.BlockSpec((1, D_LRU), const2),
            pl.BlockSpec((1, D_LRU), const2),
            pl.BlockSpec((1, D_POOL), const2),
        ],
        out_specs=[
            pl.BlockSpec((ts, D_POOL), row),
            pl.BlockSpec((ts, D_LRU), row),
            pl.BlockSpec((ts, D_LRU), row),
            pl.BlockSpec((ts, D_LRU), row),
        ],
        scratch_shapes=[
            pltpu.VMEM((ts + 2 * HALO, D_MODEL), BF16),
            pltpu.VMEM((ts, D_LRU), F32),
            pltpu.VMEM((ts, D_LRU), F32),
            pltpu.VMEM((1, D_LRU), F32),
        ],
    )
    return pl.pallas_call(
        functools.partial(_mix_a_kernel, ts=ts, starts=starts),
        grid_spec=grid_spec,
        out_shape=[
            jax.ShapeDtypeStruct((n, D_POOL), BF16),
            jax.ShapeDtypeStruct((n, D_LRU), F32),
            jax.ShapeDtypeStruct((n, D_LRU), F32),
            jax.ShapeDtypeStruct((n, D_LRU), F32),
        ],
        compiler_params=pltpu.CompilerParams(
            dimension_semantics=("arbitrary",), vmem_limit_bytes=VMEM_LIMIT),
        name="mix_a",
    )(tile_pos, tile_len, *xs, *xs, *xs, p["norm_mix"], p["w_in"], p["pool_w"], p["pool_scale"],
      p["conv_w"], p["conv_b"], p["wg_f"], p["ba_f"], p["bx_f"], p["sp_f"], p["out_norm_pool"])


def _mix_b_kernel(pos_ref, len_ref, *refs, ts, g_tiles, starts):
    n_src = len(starts)
    x_refs = refs[0:n_src]
    (xc_ref, hf_ref, gate_ref, ypn_ref, wg_ref, ba_ref, bx_ref, sp_ref, onl_ref, wout_ref,
     o_ref, a_scr, b_scr, hb_scr, carry_ref) = refs[n_src:]
    g = g_tiles - 1 - pl.program_id(0)
    at_end = pos_ref[g] + ts == len_ref[g]
    _lru_gates(xc_ref[...], wg_ref, ba_ref, bx_ref, sp_ref, a_scr, b_scr)
    carry0 = jnp.where(at_end, 0.0, carry_ref[...])
    carry_ref[...] = _scan_rows(a_scr, b_scr, hb_scr, carry0, ts, reverse=True)
    y_lru = (hf_ref[...] + hb_scr[...]) * gate_ref[...]
    yln = _rms(y_lru, onl_ref[...]).astype(BF16)
    out = jnp.dot(ypn_ref[...], wout_ref[0:D_POOL, :], preferred_element_type=F32)
    out = out + jnp.dot(yln, wout_ref[D_POOL:, :], preferred_element_type=F32)
    o_ref[...] = _pick_source(g, starts, x_refs) + out


def _mix_b(xs, xc, hf, gate, ypn, tile_pos, tile_len, p, *, ts):
    starts, g_tiles = _source_starts(xs, ts)
    n = g_tiles * ts
    row = lambda g, *_: (g_tiles - 1 - g, 0)

    def cur(start, x):
        return pl.BlockSpec((ts, D_MODEL),
                            lambda g, *_: (jnp.clip(g_tiles - 1 - g - start, 0, x.shape[0] // ts - 1), 0))

    const2 = lambda g, *_: (0, 0)
    const3 = lambda g, *_: (0, 0, 0)
    grid_spec = pltpu.PrefetchScalarGridSpec(
        num_scalar_prefetch=2,
        grid=(g_tiles,),
        in_specs=[
            *[cur(st, x) for st, x in zip(starts, xs)],
            pl.BlockSpec((ts, D_LRU), row),
            pl.BlockSpec((ts, D_LRU), row),
            pl.BlockSpec((ts, D_LRU), row),
            pl.BlockSpec((ts, D_POOL), row),
            pl.BlockSpec((N_LRU_HEADS, LRU_BLOCK, 2 * LRU_BLOCK), const3),
            pl.BlockSpec((1, D_LRU), const2),
            pl.BlockSpec((1, D_LRU), const2),
            pl.BlockSpec((1, D_LRU), const2),
            pl.BlockSpec((1, D_LRU), const2),
            pl.BlockSpec((D_POOL + D_LRU, D_MODEL), const2),
        ],
        out_specs=pl.BlockSpec((ts, D_MODEL), row),
        scratch_shapes=[
            pltpu.VMEM((ts, D_LRU), F32),
            pltpu.VMEM((ts, D_LRU), F32),
            pltpu.VMEM((ts, D_LRU), F32),
            pltpu.VMEM((1, D_LRU), F32),
        ],
    )
    return pl.pallas_call(
        functools.partial(_mix_b_kernel, ts=ts, g_tiles=g_tiles, starts=starts),
        grid_spec=grid_spec,
        out_shape=jax.ShapeDtypeStruct((n, D_MODEL), F32),
        compiler_params=pltpu.CompilerParams(
            dimension_semantics=("arbitrary",), vmem_limit_bytes=VMEM_LIMIT),
        name="mix_b",
    )(tile_pos, tile_len, *xs, xc, hf, gate, ypn, p["wg_b"], p["ba_b"], p["bx_b"], p["sp_b"],
      p["out_norm_lru"], p["w_out"])


def _router_kernel(x_ref, nf_ref, rwh_ref, rwl_ref, rb_ref,
                   hf_ref, meta_ref, wts_ref, cnt_ref,
                   carry_ref, *, tr):
    @pl.when(pl.program_id(0) == 0)
    def _():
        carry_ref[...] = jnp.zeros_like(carry_ref)

    hf = _rms(x_ref[...], nf_ref[...])
    hf_ref[...] = hf.reshape(tr, 1, D_MODEL)
    h_hi = hf.astype(BF16)
    h_lo = (hf - h_hi.astype(F32)).astype(BF16)
    logits = (jnp.dot(h_hi, rwh_ref[...], preferred_element_type=F32)
              + jnp.dot(h_lo, rwh_ref[...], preferred_element_type=F32)
              + jnp.dot(h_hi, rwl_ref[...], preferred_element_type=F32)) + rb_ref[...]
    lane = lax.broadcasted_iota(jnp.int32, (tr, LANES), 1).astype(F32)

    vals, idxs = [], []
    l = logits
    for _ in range(TOP_K):
        m = jnp.max(l, axis=-1, keepdims=True)
        ik = jnp.min(jnp.where(l == m, lane, float(LANES)), axis=-1, keepdims=True)
        vals.append(m)
        idxs.append(ik)
        l = jnp.where(lane == ik, -jnp.inf, l)
    es = [jnp.exp(v - vals[0]) for v in vals]
    den = es[0] + es[1] + es[2] + es[3]

    hits = [lane == ik for ik in idxs]
    chosen = jnp.zeros((tr, LANES), F32)
    for hit in hits:
        chosen = chosen + jnp.where(hit, 1.0, 0.0)
    tri = jnp.where(lax.broadcasted_iota(jnp.int32, (tr, tr), 1) < lax.broadcasted_iota(jnp.int32, (tr, tr), 0),
                    1.0, 0.0).astype(BF16)
    before = jnp.dot(tri, chosen.astype(BF16), preferred_element_type=F32) + carry_ref[...]
    total = carry_ref[...] + jnp.sum(chosen, axis=0, keepdims=True)
    carry_ref[...] = total
    cnt_ref[...] = total

    meta = jnp.zeros((tr, LANES), F32)
    wts = jnp.zeros((tr, LANES), F32)
    for k in range(TOP_K):
        pos_k = jnp.sum(jnp.where(hits[k], before, 0.0), axis=-1, keepdims=True)
        meta = jnp.where(lane == float(k), idxs[k], meta)
        meta = jnp.where(lane == float(TOP_K + k), pos_k, meta)
        wts = jnp.where(lane == float(k), es[k] / den, wts)
    meta_ref[...] = jnp.transpose(meta)[0:2 * TOP_K, :].astype(jnp.int32)
    wts_ref[...] = wts


def _router(x, p, *, tr):
    n = x.shape[0]
    row = lambda j: (j, 0)
    const2 = lambda j: (0, 0)
    return pl.pallas_call(
        functools.partial(_router_kernel, tr=tr),
        grid=(n // tr,),
        in_specs=[
            pl.BlockSpec((tr, D_MODEL), row),
            pl.BlockSpec((1, D_MODEL), const2),
            pl.BlockSpec((D_MODEL, LANES), const2),
            pl.BlockSpec((D_MODEL, LANES), const2),
            pl.BlockSpec((1, LANES), const2),
        ],
        out_specs=[
            pl.BlockSpec((tr, 1, D_MODEL), lambda j: (j, 0, 0)),
            pl.BlockSpec((2 * TOP_K, tr), lambda j: (0, j)),
            pl.BlockSpec((tr, LANES), row),
            pl.BlockSpec((1, LANES), const2),
        ],
        out_shape=[
            jax.ShapeDtypeStruct((n, 1, D_MODEL), F32),
            jax.ShapeDtypeStruct((2 * TOP_K, n), jnp.int32),
            jax.ShapeDtypeStruct((n, LANES), F32),
            jax.ShapeDtypeStruct((1, LANES), F32),
        ],
        scratch_shapes=[pltpu.VMEM((1, LANES), F32)],
        compiler_params=pltpu.CompilerParams(
            dimension_semantics=("arbitrary",), vmem_limit_bytes=VMEM_LIMIT),
        name="router",
    )(x, p["norm_ffn"], p["router_w_hi"], p["router_w_lo"], p["router_b"])


def _moe_kernel(te_ref, used_ref,
                tok0_ref, tokn_ref, dst_ref, hf_hbm, wup_ref, bup_ref, wdn_ref, bdn_ref,
                ys_hbm,
                xbuf, obuf, x2d, o2d, wup_bf, wdn_bf, gsem, ssem, *, tm, nt):
    s = pl.program_id(0)
    slot = lax.rem(s, 2)
    other = 1 - slot
    o_fill = lax.rem(s + (OUT_SLOTS - 1), OUT_SLOTS)
    o_send = lax.rem(s + (OUT_SLOTS - 2), OUT_SLOTS)

    def gather_wait(b):
        pltpu.make_async_copy(hf_hbm.at[pl.ds(0, tm)], xbuf.at[pl.ds(b * tm, tm)], gsem.at[b]).wait()

    def scatter_wait(b):
        pltpu.make_async_copy(obuf.at[pl.ds(b * tm, tm)], ys_hbm.at[pl.ds(0, tm)], ssem.at[b]).wait()

    def gather_row(tok, b, r, prio=0):
        pltpu.make_async_copy(hf_hbm.at[pl.ds(tok, 1)], xbuf.at[pl.ds(b * tm + r, 1)], gsem.at[b]).start(priority=prio)

    def scatter_row(dst, b, r, prio=0):
        pltpu.make_async_copy(obuf.at[pl.ds(b * tm + r, 1)], ys_hbm.at[pl.ds(dst, 1)], ssem.at[b]).start(priority=prio)

    @pl.when(s == 0)
    def _():
        obuf[...] = jnp.zeros_like(obuf)
        o2d[...] = jnp.zeros_like(o2d)

        def body(r, c):
            gather_row(tok0_ref[0, 0, r], 0, r)
            return c
        lax.fori_loop(0, tm, body, 0)

    last = used_ref[0] + 1

    @pl.when(s <= last)
    def _():
        @pl.when(s >= OUT_SLOTS - 1)
        def _():
            scatter_wait(o_fill)

        gather_wait(slot)

        @pl.when(jnp.logical_or(s == 0, te_ref[s] != te_ref[jnp.maximum(s - 1, 0)]))
        def _():
            wup_bf[...] = wup_ref[...].astype(BF16)
            wdn_bf[...] = wdn_ref[...].astype(BF16)

        xrows = xbuf.at[pl.ds(pl.multiple_of(slot * tm, tm), tm)]
        orows = obuf.at[pl.ds(pl.multiple_of(o_fill * tm, tm), tm)]
        orows[...] = _pack_bf16_pairs(o2d[:, :HALF], o2d[:, HALF:]).reshape(tm, 1, HALF)
        kw = D_MODEL // K_CHUNKS
        bounds = [(tm * g) // (K_CHUNKS - 1) for g in range(K_CHUNKS)] + [tm]
        hh = bup_ref[...]
        for kc in range(K_CHUNKS):
            cols = slice(kc * kw, (kc + 1) * kw)
            x2d[:, cols] = xrows[:, :, cols].reshape(tm, kw)
            for r in range(bounds[kc], bounds[kc + 1]):
                gather_row(tokn_ref[0, 0, r], other, r, r % 2)
                scatter_row(dst_ref[0, 0, r], o_send, r, (r + 1) % 2)
            hh = hh + jnp.dot(x2d[:, cols].astype(BF16), wup_bf[cols, :], preferred_element_type=F32)
        x_glu = jnp.minimum(hh[:, :D_FF], SWIGLU_LIMIT)
        x_lin = jnp.clip(hh[:, D_FF:], -SWIGLU_LIMIT, SWIGLU_LIMIT)
        act = (x_glu * jax.nn.sigmoid(SWIGLU_ALPHA * x_glu) * (x_lin + 1.0)).astype(BF16)
        o2d[...] = jnp.dot(act, wdn_bf[...], preferred_element_type=F32) + bdn_ref[...]


        @pl.when(s == last)
        def _():
            scatter_wait(o_send)
            scatter_wait(lax.rem(s + (OUT_SLOTS - 3), OUT_SLOTS))
            gather_wait(other)


def _moe(hf3, tile_expert, used, tok3, dst3, w_up, b_up, w_down, b_down, *, layer, tm, n_out_rows):
    nt = tok3.shape[0]
    smem_blk = lambda fn: pl.BlockSpec((1, 1, tm), fn, memory_space=pltpu.SMEM)
    grid_spec = pltpu.PrefetchScalarGridSpec(
        num_scalar_prefetch=2,
        grid=(nt,),
        in_specs=[
            smem_blk(lambda s, te, u: (0, 0, 0)),
            smem_blk(lambda s, te, u: (jnp.minimum(s + 1, nt - 1), 0, 0)),
            smem_blk(lambda s, te, u: (s, 0, 0)),
            pl.BlockSpec(memory_space=pl.ANY),
            pl.BlockSpec((None, None, D_MODEL, 2 * D_FF), lambda s, te, u: (layer, te[s], 0, 0)),
            pl.BlockSpec((None, None, 1, 2 * D_FF), lambda s, te, u: (layer, te[s], 0, 0)),
            pl.BlockSpec((None, None, D_FF, D_MODEL), lambda s, te, u: (layer, te[s], 0, 0)),
            pl.BlockSpec((None, None, 1, D_MODEL), lambda s, te, u: (layer, te[s], 0, 0)),
        ],
        out_specs=pl.BlockSpec(memory_space=pl.ANY),
        scratch_shapes=[
            pltpu.VMEM((2 * tm, 1, D_MODEL), F32),
            pltpu.VMEM((OUT_SLOTS * tm, 1, HALF), jnp.uint32),
            pltpu.VMEM((tm, D_MODEL), F32),
            pltpu.VMEM((tm, D_MODEL), F32),
            pltpu.VMEM((D_MODEL, 2 * D_FF), BF16),
            pltpu.VMEM((D_FF, D_MODEL), BF16),
            pltpu.SemaphoreType.DMA((2,)),
            pltpu.SemaphoreType.DMA((OUT_SLOTS,)),
        ],
    )
    return pl.pallas_call(
        functools.partial(_moe_kernel, tm=tm, nt=nt),
        grid_spec=grid_spec,
        out_shape=jax.ShapeDtypeStruct((n_out_rows, 1, HALF), jnp.uint32),
        compiler_params=pltpu.CompilerParams(
            dimension_semantics=("arbitrary",), vmem_limit_bytes=VMEM_LIMIT),
        name="moe",
    )(tile_expert, used, tok3, tok3, dst3, hf3, w_up, b_up, w_down, b_down)


def _combine_kernel(x_ref, y0_ref, y1_ref, y2_ref, y3_ref, wts_ref, nf_ref, o_ref, y2d, *, tc, final):
    for k, y_ref in enumerate((y0_ref, y1_ref, y2_ref, y3_ref)):
        y2d[k] = y_ref[...].reshape(tc, HALF)
    w = wts_ref[...]
    acc_lo = acc_hi = None
    for k in range(TOP_K):
        lo, hi = _unpack_bf16_pairs(y2d[k])
        acc_lo = w[:, k:k + 1] * lo if k == 0 else acc_lo + w[:, k:k + 1] * lo
        acc_hi = w[:, k:k + 1] * hi if k == 0 else acc_hi + w[:, k:k + 1] * hi
    x = x_ref[...] + jnp.concatenate([acc_lo, acc_hi], axis=-1)
    o_ref[...] = _rms(x, nf_ref[...]) if final else x


def _combine(x, ys3, wts, norm_final, *, tc, final, row0, n_rows):
    n = x.shape[0]
    b0 = row0 // tc
    kb = n // tc
    row = lambda j: (b0 + j, 0)
    y_spec = lambda k: pl.BlockSpec((tc, 1, HALF), lambda j: (k * kb + b0 + j, 0, 0))
    return pl.pallas_call(
        functools.partial(_combine_kernel, tc=tc, final=final),
        grid=(n_rows // tc,),
        in_specs=[
            pl.BlockSpec((tc, D_MODEL), row),
            y_spec(0), y_spec(1), y_spec(2), y_spec(3),
            pl.BlockSpec((tc, LANES), row),
            pl.BlockSpec((1, D_MODEL), lambda j: (0, 0)),
        ],
        out_specs=pl.BlockSpec((tc, D_MODEL), lambda j: (j, 0)),
        out_shape=jax.ShapeDtypeStruct((n_rows, D_MODEL), F32),
        scratch_shapes=[pltpu.VMEM((TOP_K, tc, HALF), jnp.uint32)],
        compiler_params=pltpu.CompilerParams(
            dimension_semantics=("arbitrary",), vmem_limit_bytes=VMEM_LIMIT),
        name="combine_final" if final else "combine",
    )(x, ys3, ys3, ys3, ys3, wts, norm_final)


def _seq_tables(groups, ts):
    pos, length = [], []
    for n_seq, seq_len in groups:
        assert seq_len % ts == 0
        for _ in range(n_seq):
            for t in range(seq_len // ts):
                pos.append(t * ts)
                length.append(seq_len)
    return np.asarray(pos, np.int32), np.asarray(length, np.int32)


def _routing_plan(meta, counts, *, n, tm, nt):
    idx = meta[0:TOP_K]
    pos = meta[TOP_K:2 * TOP_K]
    cnt = counts[0, :N_EXPERTS].astype(jnp.int32)
    tiles_e = (cnt + tm - 1) // tm
    tile_end = jnp.cumsum(tiles_e)
    base = (tile_end - tiles_e) * tm
    base_of = jnp.zeros_like(idx)
    for e in range(N_EXPERTS):
        base_of = jnp.where(idx == e, base[e], base_of)
    slot = (base_of + pos).reshape(-1)
    tile_ids = jnp.arange(nt, dtype=jnp.int32)
    tile_expert = jnp.minimum(jnp.sum(tile_end[None, :] <= tile_ids[:, None], axis=1), N_EXPERTS - 1).astype(jnp.int32)
    pairs = jnp.arange(TOP_K * n, dtype=jnp.int32)
    rows = jnp.arange((nt + 2) * tm, dtype=jnp.int32)
    spare = TOP_K * n + ((rows // tm) % 2) * tm + rows % tm
    dst = spare.at[slot + 2 * tm].set(pairs, unique_indices=True, mode="promise_in_bounds")
    src = dst[2 * tm:]
    tok = jnp.where(src < TOP_K * n, src % n, 0)
    return tile_expert, tile_end[-1:], tok.reshape(nt, 1, tm), dst[:nt * tm].reshape(nt, 1, tm)


def _layer_params(l, norm_mix, w_in, pool_w, pool_scale, conv_w, conv_b, lru_wa, lru_ba, lru_wx, lru_bx,
                  lru_lambda, out_norm_pool, out_norm_lru, w_out, norm_ffn, router_w, router_b,
                  w_up, b_up, w_down, b_down):
    def gates(d):
        return dict(
            wg=jnp.concatenate([lru_wa[l, d], lru_wx[l, d]], axis=-1).astype(BF16),
            ba=lru_ba[l, d].reshape(1, D_LRU),
            bx=lru_bx[l, d].reshape(1, D_LRU),
            sp=jax.nn.softplus(-lru_lambda[l, d]).reshape(1, D_LRU),
        )
    gf, gb = gates(0), gates(1)
    pad = LANES - N_EXPERTS
    rw = jnp.pad(router_w[l], ((0, 0), (0, pad)))
    rw_hi = rw.astype(BF16)
    return dict(
        norm_mix=norm_mix[l].reshape(1, D_MODEL),
        w_in=w_in[l].astype(BF16),
        pool_w=pool_w[l].astype(BF16),
        pool_scale=pool_scale[l].reshape(1, D_POOL),
        conv_w=conv_w[l],
        conv_b=conv_b[l].reshape(1, D_LRU),
        wg_f=gf["wg"], ba_f=gf["ba"], bx_f=gf["bx"], sp_f=gf["sp"],
        wg_b=gb["wg"], ba_b=gb["ba"], bx_b=gb["bx"], sp_b=gb["sp"],
        out_norm_pool=out_norm_pool[l].reshape(1, D_POOL),
        out_norm_lru=out_norm_lru[l].reshape(1, D_LRU),
        w_out=w_out[l].astype(BF16),
        norm_ffn=norm_ffn[l].reshape(1, D_MODEL),
        router_w_hi=rw_hi,
        router_w_lo=(rw - rw_hi.astype(F32)).astype(BF16),
        router_b=jnp.pad(router_b[l].reshape(1, N_EXPERTS), ((0, 0), (0, pad)), constant_values=-jnp.inf),
    )


def _trunk(xs, weights, norm_final, *, ts, tr, tm, tc):
    groups = [(x.shape[0], x.shape[1]) for x in xs]
    sizes = [b * s for b, s in groups]
    n = sum(sizes)
    depth = weights[0].shape[0]
    srcs = [xi.reshape(-1, D_MODEL) for xi in xs]
    tile_pos, tile_len = _seq_tables(groups, ts)
    tile_pos, tile_len = jnp.asarray(tile_pos), jnp.asarray(tile_len)
    nt = (TOP_K * n) // tm + N_EXPERTS + 2
    n_out_rows = TOP_K * n + 2 * tm
    nf = norm_final.reshape(1, D_MODEL)
    w_up, b_up, w_down, b_down = weights[-4:]
    b_up3 = b_up.reshape(depth, N_EXPERTS, 1, 2 * D_FF)
    b_down3 = b_down.reshape(depth, N_EXPERTS, 1, D_MODEL)
    outs = None
    for l in range(depth):
        p = _layer_params(l, *weights)
        ypn, hf, gate, xc = _mix_a(srcs, tile_pos, tile_len, p, ts=ts)
        x = _mix_b(srcs, xc, hf, gate, ypn, tile_pos, tile_len, p, ts=ts)
        hf3, meta, wts, counts = _router(x, p, tr=tr)
        tile_expert, used, tok3, dst3 = _routing_plan(meta, counts, n=n, tm=tm, nt=nt)
        ys3 = _moe(hf3, tile_expert, used, tok3, dst3, w_up, b_up3, w_down, b_down3,
                   layer=l, tm=tm, n_out_rows=n_out_rows)
        if l + 1 < depth:
            x = _combine(x, ys3, wts, nf, tc=tc, final=False, row0=0, n_rows=n)
            srcs = [x]
        else:
            outs, row0 = [], 0
            for (b, s), sz in zip(groups, sizes):
                y = _combine(x, ys3, wts, nf, tc=tc, final=True, row0=row0, n_rows=sz)
                outs.append(y.reshape(b, s, D_MODEL))
                row0 += sz
    return outs


def kernel(x_prompt, x_sample, norm_mix, w_in, pool_w, pool_scale, conv_w, conv_b, lru_wa, lru_ba, lru_wx, lru_bx,
           lru_lambda, out_norm_pool, out_norm_lru, w_out, norm_ffn, router_w, router_b, w_up, b_up, w_down,
           b_down, norm_final):
    weights = (norm_mix, w_in, pool_w, pool_scale, conv_w, conv_b, lru_wa, lru_ba, lru_wx, lru_bx, lru_lambda,
               out_norm_pool, out_norm_lru, w_out, norm_ffn, router_w, router_b, w_up, b_up, w_down, b_down)
    y_prompt, y_sample = _trunk([x_prompt, x_sample], weights, norm_final, ts=1024, tr=512, tm=512, tc=512)
    return (y_prompt, y_sample)
```

```python
import functools

import numpy as np
import jax
import jax.numpy as jnp
from jax import lax
from jax.experimental import pallas as pl
from jax.experimental.pallas import tpu as pltpu

D_MODEL = 1024
D_POOL = 512
D_LRU = 512
D_IN = D_POOL + 2 * D_LRU
POOL_WINDOWS = (2, 4, 8, 16)
POOL_GROUP_W = D_POOL // len(POOL_WINDOWS)
N_LRU_HEADS = 4
LRU_BLOCK = D_LRU // N_LRU_HEADS
LRU_C = 8.0
CONV_W = 4
CONV_LEFT = 1
N_EXPERTS = 32
TOP_K = 4
D_FF = D_MODEL
SWIGLU_LIMIT = 7.0
SWIGLU_ALPHA = 1.702
EPS = 1e-6

LANES = 128
SUBLANES = 8
HALO = 16
VMEM_LIMIT = 56 * 1024 * 1024
K_CHUNKS = 2
HALF = D_MODEL // 2
SCAN_UNROLL = 4
OUT_SLOTS = 3

F32 = jnp.float32
BF16 = jnp.bfloat16


def _rms(x, g):
    return x * lax.rsqrt(jnp.mean(x * x, axis=-1, keepdims=True) + EPS) * g


def _pack_bf16_pairs(lo, hi):
    lo_bits = lax.bitcast_convert_type(lo.astype(BF16).astype(F32), jnp.uint32)
    hi_bits = lax.bitcast_convert_type(hi.astype(BF16).astype(F32), jnp.uint32)
    return lax.shift_right_logical(lo_bits, jnp.uint32(16)) | hi_bits


def _unpack_bf16_pairs(w):
    lo = lax.bitcast_convert_type(lax.shift_left(w, jnp.uint32(16)), F32)
    hi = lax.bitcast_convert_type(w & jnp.uint32(0xFFFF0000), F32)
    return lo, hi


def _shift_rows(v, s):
    n = v.shape[0]
    return pltpu.roll(v, s % n, axis=0)


def _lru_gates(xc, wg_ref, ba_ref, bx_ref, sp_ref, a_scr, b_scr):
    for h in range(N_LRU_HEADS):
        cs = slice(h * LRU_BLOCK, (h + 1) * LRU_BLOCK)
        xh = xc[:, cs]
        gr = jnp.dot(xh.astype(BF16), wg_ref[h], preferred_element_type=F32)
        r = jax.nn.sigmoid(gr[:, :LRU_BLOCK] + ba_ref[:, cs])
        i = jax.nn.sigmoid(gr[:, LRU_BLOCK:] + bx_ref[:, cs])
        log_a = -LRU_C * r * sp_ref[:, cs]
        a = jnp.exp(log_a)
        beta = jnp.sqrt(1.0 - a * a)
        a_scr[:, cs] = a
        b_scr[:, cs] = beta * i * xh


def _scan_rows(a_ref, b_ref, h_ref, carry0, n_rows, reverse):
    row = lax.broadcasted_iota(jnp.int32, (SUBLANES, D_LRU), 0)
    n_grp = n_rows // SUBLANES
    n_iter = n_grp // SCAN_UNROLL

    def prefix(r0):
        a = a_ref[pl.ds(r0, SUBLANES), :]
        b = b_ref[pl.ds(r0, SUBLANES), :]
        for s in (1, 2, 4):
            if reverse:
                ok = row < SUBLANES - s
                a_sh = jnp.where(ok, pltpu.roll(a, SUBLANES - s, axis=0), 1.0)
                b_sh = jnp.where(ok, pltpu.roll(b, SUBLANES - s, axis=0), 0.0)
            else:
                ok = row >= s
                a_sh = jnp.where(ok, pltpu.roll(a, s, axis=0), 1.0)
                b_sh = jnp.where(ok, pltpu.roll(b, s, axis=0), 0.0)
            b = a * b_sh + b
            a = a * a_sh
        return a, b

    def body(k, carry):
        first = (n_iter - 1 - k) if reverse else k
        order = range(SCAN_UNROLL - 1, -1, -1) if reverse else range(SCAN_UNROLL)
        starts = [pl.multiple_of((first * SCAN_UNROLL + u) * SUBLANES, SUBLANES) for u in order]
        parts = [prefix(r0) for r0 in starts]
        for r0, (a, b) in zip(starts, parts):
            h = a * carry + b
            h_ref[pl.ds(r0, SUBLANES), :] = h
            carry = h[0:1, :] if reverse else h[SUBLANES - 1:SUBLANES, :]
        return carry

    return lax.fori_loop(0, n_iter, body, carry0)


def _pick_source(g, starts, refs):
    v = refs[0][...]
    for start, ref in zip(starts[1:], refs[1:]):
        v = jnp.where(g >= start, ref[...], v)
    return v


def _mix_a_kernel(pos_ref, len_ref, *refs, ts, starts):
    n_src = len(starts)
    x_refs, xp_refs, xn_refs = refs[0:n_src], refs[n_src:2 * n_src], refs[2 * n_src:3 * n_src]
    (nm_ref, win_ref, pw_ref, ps_ref, cw_ref, cb_ref, wg_ref, ba_ref, bx_ref, sp_ref, onp_ref,
     ypn_ref, hf_ref, gate_ref, xc_ref, h_scr, a_scr, b_scr, carry_ref) = refs[3 * n_src:]
    g = pl.program_id(0)
    pos0 = pos_ref[g]
    slen = len_ref[g]
    keep_prev = jnp.where(pos0 == 0, 0.0, 1.0).astype(F32)
    keep_next = jnp.where(pos0 + ts == slen, 0.0, 1.0).astype(F32)
    nm = nm_ref[...]

    h_scr[0:HALO, :] = (_rms(_pick_source(g, starts, xp_refs), nm) * keep_prev).astype(BF16)
    h_scr[HALO:HALO + ts, :] = _rms(_pick_source(g, starts, x_refs), nm).astype(BF16)
    h_scr[HALO + ts:, :] = (_rms(_pick_source(g, starts, xn_refs), nm) * keep_next).astype(BF16)
    z = jnp.dot(h_scr[...], win_ref[...], preferred_element_type=F32)

    trow = pos0 + lax.broadcasted_iota(jnp.int32, (ts, POOL_GROUP_W), 0)
    ys = []
    for gi, win in enumerate(POOL_WINDOWS):
        u = z[:, gi * POOL_GROUP_W:(gi + 1) * POOL_GROUP_W]
        acc = u + _shift_rows(u, 1)
        half = 1
        while 2 * half < win:
            acc = _shift_rows(acc, half) + _shift_rows(acc, -half)
            half *= 2
        half = win // 2
        cnt = (jnp.minimum(trow + half, slen) - jnp.maximum(trow - half, 0)).astype(F32)
        p = acc[HALO:HALO + ts] / cnt - u[HALO:HALO + ts]
        ys.append(jnp.dot(p.astype(BF16), pw_ref[gi], preferred_element_type=F32))
    y_pool = jnp.concatenate(ys, axis=-1) * ps_ref[...]
    ypn_ref[...] = _rms(y_pool, onp_ref[...]).astype(BF16)

    ul = z[:, D_POOL:D_POOL + D_LRU]
    xc_ext = cb_ref[...] + _shift_rows(ul, CONV_LEFT) * cw_ref[0:1, :]
    xc_ext = xc_ext + ul * cw_ref[1:2, :]
    xc_ext = xc_ext + _shift_rows(ul, -1) * cw_ref[2:3, :]
    xc_ext = xc_ext + _shift_rows(ul, -2) * cw_ref[3:4, :]
    xc = xc_ext[HALO:HALO + ts]
    xc_ref[...] = xc

    gate_ref[...] = jax.nn.gelu(z[HALO:HALO + ts, D_POOL + D_LRU:], approximate=True)

    _lru_gates(xc, wg_ref, ba_ref, bx_ref, sp_ref, a_scr, b_scr)
    carry0 = jnp.where(pos0 == 0, 0.0, carry_ref[...])
    carry_ref[...] = _scan_rows(a_scr, b_scr, hf_ref, carry0, ts, reverse=False)


def _source_starts(xs, ts):
    starts, at = [], 0
    for x in xs:
        starts.append(at)
        at += x.shape[0] // ts
    return tuple(starts), at


def _mix_a(xs, tile_pos, tile_len, p, *, ts):
    starts, g_tiles = _source_starts(xs, ts)
    n = g_tiles * ts
    hb = ts // HALO
    row = lambda g, *_: (g, 0)
    const2 = lambda g, *_: (0, 0)
    const3 = lambda g, *_: (0, 0, 0)

    def cur(start, x):
        return pl.BlockSpec((ts, D_MODEL), lambda g, *_: (jnp.clip(g - start, 0, x.shape[0] // ts - 1), 0))

    def prev(start, x):
        return pl.BlockSpec((HALO, D_MODEL),
                            lambda g, *_: (jnp.clip((g - start) * hb - 1, 0, x.shape[0] // HALO - 1), 0))

    def nxt(start, x):
        return pl.BlockSpec((HALO, D_MODEL),
                            lambda g, *_: (jnp.clip((g - start + 1) * hb, 0, x.shape[0] // HALO - 1), 0))

    grid_spec = pltpu.PrefetchScalarGridSpec(
        num_scalar_prefetch=2,
        grid=(g_tiles,),
        in_specs=[
            *[cur(st, x) for st, x in zip(starts, xs)],
            *[prev(st, x) for st, x in zip(starts, xs)],
            *[nxt(st, x) for st, x in zip(starts, xs)],
            pl.BlockSpec((1, D_MODEL), const2),
            pl.BlockSpec((D_MODEL, D_IN), const2),
            pl.BlockSpec((len(POOL_WINDOWS), POOL_GROUP_W, POOL_GROUP_W), const3),
            pl.BlockSpec((1, D_POOL), const2),
            pl.BlockSpec((CONV_W, D_LRU), const2),
            pl.BlockSpec((1, D_LRU), const2),
            pl.BlockSpec((N_LRU_HEADS, LRU_BLOCK, 2 * LRU_BLOCK), const3),
            pl.BlockSpec((1, D_LRU), const2),
            pl.BlockSpec((1, D_LRU), const2),
            pl.BlockSpec((1, D_LRU), const2),
            pl.BlockSpec((1, D_POOL), const2),
        ],
        out_specs=[
            pl.BlockSpec((ts, D_POOL), row),
            pl.BlockSpec((ts, D_LRU), row),
            pl.BlockSpec((ts, D_LRU), row),
            pl.BlockSpec((ts, D_LRU), row),
        ],
        scratch_shapes=[
            pltpu.VMEM((ts + 2 * HALO, D_MODEL), BF16),
            pltpu.VMEM((ts, D_LRU), F32),
            pltpu.VMEM((ts, D_LRU), F32),
            pltpu.VMEM((1, D_LRU), F32),
        ],
    )
    return pl.pallas_call(
        functools.partial(_mix_a_kernel, ts=ts, starts=starts),
        grid_spec=grid_spec,
        out_shape=[
            jax.ShapeDtypeStruct((n, D_POOL), BF16),
            jax.ShapeDtypeStruct((n, D_LRU), F32),
            jax.ShapeDtypeStruct((n, D_LRU), F32),
            jax.ShapeDtypeStruct((n, D_LRU), F32),
        ],
        compiler_params=pltpu.CompilerParams(
            dimension_semantics=("arbitrary",), vmem_limit_bytes=VMEM_LIMIT),
        name="mix_a",
    )(tile_pos, tile_len, *xs, *xs, *xs, p["norm_mix"], p["w_in"], p["pool_w"], p["pool_scale"],
      p["conv_w"], p["conv_b"], p["wg_f"], p["ba_f"], p["bx_f"], p["sp_f"], p["out_norm_pool"])


def _mix_b_kernel(pos_ref, len_ref, *refs, ts, g_tiles, starts):
    n_src = len(starts)
    x_refs = refs[0:n_src]
    (xc_ref, hf_ref, gate_ref, ypn_ref, wg_ref, ba_ref, bx_ref, sp_ref, onl_ref, wout_ref,
     o_ref, a_scr, b_scr, hb_scr, carry_ref) = refs[n_src:]
    g = g_tiles - 1 - pl.program_id(0)
    at_end = pos_ref[g] + ts == len_ref[g]
    _lru_gates(xc_ref[...], wg_ref, ba_ref, bx_ref, sp_ref, a_scr, b_scr)
    carry0 = jnp.where(at_end, 0.0, carry_ref[...])
    carry_ref[...] = _scan_rows(a_scr, b_scr, hb_scr, carry0, ts, reverse=True)
    y_lru = (hf_ref[...] + hb_scr[...]) * gate_ref[...]
    yln = _rms(y_lru, onl_ref[...]).astype(BF16)
    out = jnp.dot(ypn_ref[...], wout_ref[0:D_POOL, :], preferred_element_type=F32)
    out = out + jnp.dot(yln, wout_ref[D_POOL:, :], preferred_element_type=F32)
    o_ref[...] = _pick_source(g, starts, x_refs) + out


def _mix_b(xs, xc, hf, gate, ypn, tile_pos, tile_len, p, *, ts):
    starts, g_tiles = _source_starts(xs, ts)
    n = g_tiles * ts
    row = lambda g, *_: (g_tiles - 1 - g, 0)

    def cur(start, x):
        return pl.BlockSpec((ts, D_MODEL),
                            lambda g, *_: (jnp.clip(g_tiles - 1 - g - start, 0, x.shape[0] // ts - 1), 0))

    const2 = lambda g, *_: (0, 0)
    const3 = lambda g, *_: (0, 0, 0)
    grid_spec = pltpu.PrefetchScalarGridSpec(
        num_scalar_prefetch=2,
        grid=(g_tiles,),
        in_specs=[
            *[cur(st, x) for st, x in zip(starts, xs)],
            pl.BlockSpec((ts, D_LRU), row),
            pl.BlockSpec((ts, D_LRU), row),
            pl.BlockSpec((ts, D_LRU), row),
            pl.BlockSpec((ts, D_POOL), row),
            pl.BlockSpec((N_LRU_HEADS, LRU_BLOCK, 2 * LRU_BLOCK), const3),
            pl.BlockSpec((1, D_LRU), const2),
            pl.BlockSpec((1, D_LRU), const2),
            pl.BlockSpec((1, D_LRU), const2),
            pl.BlockSpec((1, D_LRU), const2),
            pl.BlockSpec((D_POOL + D_LRU, D_MODEL), const2),
        ],
        out_specs=pl.BlockSpec((ts, D_MODEL), row),
        scratch_shapes=[
            pltpu.VMEM((ts, D_LRU), F32),
            pltpu.VMEM((ts, D_LRU), F32),
            pltpu.VMEM((ts, D_LRU), F32),
            pltpu.VMEM((1, D_LRU), F32),
        ],
    )
    return pl.pallas_call(
        functools.partial(_mix_b_kernel, ts=ts, g_tiles=g_tiles, starts=starts),
        grid_spec=grid_spec,
        out_shape=jax.ShapeDtypeStruct((n, D_MODEL), F32),
        compiler_params=pltpu.CompilerParams(
            dimension_semantics=("arbitrary",), vmem_limit_bytes=VMEM_LIMIT),
        name="mix_b",
    )(tile_pos, tile_len, *xs, xc, hf, gate, ypn, p["wg_b"], p["ba_b"], p["bx_b"], p["sp_b"],
      p["out_norm_lru"], p["w_out"])


def _router_kernel(x_ref, nf_ref, rwh_ref, rwl_ref, rb_ref,
                   hf_ref, meta_ref, wts_ref, cnt_ref,
                   carry_ref, *, tr):
    @pl.when(pl.program_id(0) == 0)
    def _():
        carry_ref[...] = jnp.zeros_like(carry_ref)

    hf = _rms(x_ref[...], nf_ref[...])
    hf_ref[...] = hf.reshape(tr, 1, D_MODEL)
    h_hi = hf.astype(BF16)
    h_lo = (hf - h_hi.astype(F32)).astype(BF16)
    logits = (jnp.dot(h_hi, rwh_ref[...], preferred_element_type=F32)
              + jnp.dot(h_lo, rwh_ref[...], preferred_element_type=F32)
              + jnp.dot(h_hi, rwl_ref[...], preferred_element_type=F32)) + rb_ref[...]
    lane = lax.broadcasted_iota(jnp.int32, (tr, LANES), 1).astype(F32)

    vals, idxs = [], []
    l = logits
    for _ in range(TOP_K):
        m = jnp.max(l, axis=-1, keepdims=True)
        ik = jnp.min(jnp.where(l == m, lane, float(LANES)), axis=-1, keepdims=True)
        vals.append(m)
        idxs.append(ik)
        l = jnp.where(lane == ik, -jnp.inf, l)
    es = [jnp.exp(v - vals[0]) for v in vals]
    den = es[0] + es[1] + es[2] + es[3]

    hits = [lane == ik for ik in idxs]
    chosen = jnp.zeros((tr, LANES), F32)
    for hit in hits:
        chosen = chosen + jnp.where(hit, 1.0, 0.0)
    tri = jnp.where(lax.broadcasted_iota(jnp.int32, (tr, tr), 1) < lax.broadcasted_iota(jnp.int32, (tr, tr), 0),
                    1.0, 0.0).astype(BF16)
    before = jnp.dot(tri, chosen.astype(BF16), preferred_element_type=F32) + carry_ref[...]
    total = carry_ref[...] + jnp.sum(chosen, axis=0, keepdims=True)
    carry_ref[...] = total
    cnt_ref[...] = total

    meta = jnp.zeros((tr, LANES), F32)
    wts = jnp.zeros((tr, LANES), F32)
    for k in range(TOP_K):
        pos_k = jnp.sum(jnp.where(hits[k], before, 0.0), axis=-1, keepdims=True)
        meta = jnp.where(lane == float(k), idxs[k], meta)
        meta = jnp.where(lane == float(TOP_K + k), pos_k, meta)
        wts = jnp.where(lane == float(k), es[k] / den, wts)
    meta_ref[...] = jnp.transpose(meta)[0:2 * TOP_K, :].astype(jnp.int32)
    wts_ref[...] = wts


def _router(x, p, *, tr):
    n = x.shape[0]
    row = lambda j: (j, 0)
    const2 = lambda j: (0, 0)
    return pl.pallas_call(
        functools.partial(_router_kernel, tr=tr),
        grid=(n // tr,),
        in_specs=[
            pl.BlockSpec((tr, D_MODEL), row),
            pl.BlockSpec((1, D_MODEL), const2),
            pl.BlockSpec((D_MODEL, LANES), const2),
            pl.BlockSpec((D_MODEL, LANES), const2),
            pl.BlockSpec((1, LANES), const2),
        ],
        out_specs=[
            pl.BlockSpec((tr, 1, D_MODEL), lambda j: (j, 0, 0)),
            pl.BlockSpec((2 * TOP_K, tr), lambda j: (0, j)),
            pl.BlockSpec((tr, LANES), row),
            pl.BlockSpec((1, LANES), const2),
        ],
        out_shape=[
            jax.ShapeDtypeStruct((n, 1, D_MODEL), F32),
            jax.ShapeDtypeStruct((2 * TOP_K, n), jnp.int32),
            jax.ShapeDtypeStruct((n, LANES), F32),
            jax.ShapeDtypeStruct((1, LANES), F32),
        ],
        scratch_shapes=[pltpu.VMEM((1, LANES), F32)],
        compiler_params=pltpu.CompilerParams(
            dimension_semantics=("arbitrary",), vmem_limit_bytes=VMEM_LIMIT),
        name="router",
    )(x, p["norm_ffn"], p["router_w_hi"], p["router_w_lo"], p["router_b"])


def _moe_kernel(te_ref, used_ref,
                tok0_ref, tokn_ref, dst_ref, hf_hbm, wup_ref, bup_ref, wdn_ref, bdn_ref,
                ys_hbm,
                xbuf, obuf, x2d, o2d, wup_bf, wdn_bf, gsem, ssem, *, tm, nt):
    s = pl.program_id(0)
    slot = lax.rem(s, 2)
    other = 1 - slot
    o_fill = lax.rem(s + (OUT_SLOTS - 1), OUT_SLOTS)
    o_send = lax.rem(s + (OUT_SLOTS - 2), OUT_SLOTS)

    def gather_wait(b):
        pltpu.make_async_copy(hf_hbm.at[pl.ds(0, tm)], xbuf.at[pl.ds(b * tm, tm)], gsem.at[b]).wait()

    def scatter_wait(b):
        pltpu.make_async_copy(obuf.at[pl.ds(b * tm, tm)], ys_hbm.at[pl.ds(0, tm)], ssem.at[b]).wait()

    def gather_row(tok, b, r, prio=0):
        pltpu.make_async_copy(hf_hbm.at[pl.ds(tok, 1)], xbuf.at[pl.ds(b * tm + r, 1)], gsem.at[b]).start(priority=prio)

    def scatter_row(dst, b, r, prio=0):
        pltpu.make_async_copy(obuf.at[pl.ds(b * tm + r, 1)], ys_hbm.at[pl.ds(dst, 1)], ssem.at[b]).start(priority=prio)

    @pl.when(s == 0)
    def _():
        obuf[...] = jnp.zeros_like(obuf)
        o2d[...] = jnp.zeros_like(o2d)

        def body(r, c):
            gather_row(tok0_ref[0, 0, r], 0, r)
            return c
        lax.fori_loop(0, tm, body, 0)

    last = used_ref[0] + 1

    @pl.when(s <= last)
    def _():
        @pl.when(s >= OUT_SLOTS - 1)
        def _():
            scatter_wait(o_fill)

        gather_wait(slot)

        @pl.when(jnp.logical_or(s == 0, te_ref[s] != te_ref[jnp.maximum(s - 1, 0)]))
        def _():
            wup_bf[...] = wup_ref[...].astype(BF16)
            wdn_bf[...] = wdn_ref[...].astype(BF16)

        xrows = xbuf.at[pl.ds(pl.multiple_of(slot * tm, tm), tm)]
        orows = obuf.at[pl.ds(pl.multiple_of(o_fill * tm, tm), tm)]
        orows[...] = _pack_bf16_pairs(o2d[:, :HALF], o2d[:, HALF:]).reshape(tm, 1, HALF)
        kw = D_MODEL // K_CHUNKS
        bounds = [(tm * g) // (K_CHUNKS - 1) for g in range(K_CHUNKS)] + [tm]
        hh = bup_ref[...]
        for kc in range(K_CHUNKS):
            cols = slice(kc * kw, (kc + 1) * kw)
            x2d[:, cols] = xrows[:, :, cols].reshape(tm, kw)
            for r in range(bounds[kc], bounds[kc + 1]):
                gather_row(tokn_ref[0, 0, r], other, r, r % 2)
                scatter_row(dst_ref[0, 0, r], o_send, r, (r + 1) % 2)
            hh = hh + jnp.dot(x2d[:, cols].astype(BF16), wup_bf[cols, :], preferred_element_type=F32)
        x_glu = jnp.minimum(hh[:, :D_FF], SWIGLU_LIMIT)
        x_lin = jnp.clip(hh[:, D_FF:], -SWIGLU_LIMIT, SWIGLU_LIMIT)
        act = (x_glu * jax.nn.sigmoid(SWIGLU_ALPHA * x_glu) * (x_lin + 1.0)).astype(BF16)
        o2d[...] = jnp.dot(act, wdn_bf[...], preferred_element_type=F32) + bdn_ref[...]


        @pl.when(s == last)
        def _():
            scatter_wait(o_send)
            scatter_wait(lax.rem(s + (OUT_SLOTS - 3), OUT_SLOTS))
            gather_wait(other)


def _moe(hf3, tile_expert, used, tok3, dst3, w_up, b_up, w_down, b_down, *, layer, tm, n_out_rows):
    nt = tok3.shape[0]
    smem_blk = lambda fn: pl.BlockSpec((1, 1, tm), fn, memory_space=pltpu.SMEM)
    grid_spec = pltpu.PrefetchScalarGridSpec(
        num_scalar_prefetch=2,
        grid=(nt,),
        in_specs=[
            smem_blk(lambda s, te, u: (0, 0, 0)),
            smem_blk(lambda s, te, u: (jnp.minimum(s + 1, nt - 1), 0, 0)),
            smem_blk(lambda s, te, u: (s, 0, 0)),
            pl.BlockSpec(memory_space=pl.ANY),
            pl.BlockSpec((None, None, D_MODEL, 2 * D_FF), lambda s, te, u: (layer, te[s], 0, 0)),
            pl.BlockSpec((None, None, 1, 2 * D_FF), lambda s, te, u: (layer, te[s], 0, 0)),
            pl.BlockSpec((None, None, D_FF, D_MODEL), lambda s, te, u: (layer, te[s], 0, 0)),
            pl.BlockSpec((None, None, 1, D_MODEL), lambda s, te, u: (layer, te[s], 0, 0)),
        ],
        out_specs=pl.BlockSpec(memory_space=pl.ANY),
        scratch_shapes=[
            pltpu.VMEM((2 * tm, 1, D_MODEL), F32),
            pltpu.VMEM((OUT_SLOTS * tm, 1, HALF), jnp.uint32),
            pltpu.VMEM((tm, D_MODEL), F32),
            pltpu.VMEM((tm, D_MODEL), F32),
            pltpu.VMEM((D_MODEL, 2 * D_FF), BF16),
            pltpu.VMEM((D_FF, D_MODEL), BF16),
            pltpu.SemaphoreType.DMA((2,)),
            pltpu.SemaphoreType.DMA((OUT_SLOTS,)),
        ],
    )
    return pl.pallas_call(
        functools.partial(_moe_kernel, tm=tm, nt=nt),
        grid_spec=grid_spec,
        out_shape=jax.ShapeDtypeStruct((n_out_rows, 1, HALF), jnp.uint32),
        compiler_params=pltpu.CompilerParams(
            dimension_semantics=("arbitrary",), vmem_limit_bytes=VMEM_LIMIT),
        name="moe",
    )(tile_expert, used, tok3, tok3, dst3, hf3, w_up, b_up, w_down, b_down)


def _combine_kernel(x_ref, y0_ref, y1_ref, y2_ref, y3_ref, wts_ref, nf_ref, o_ref, y2d, *, tc, final):
    for k, y_ref in enumerate((y0_ref, y1_ref, y2_ref, y3_ref)):
        y2d[k] = y_ref[...].reshape(tc, HALF)
    w = wts_ref[...]
    acc_lo = acc_hi = None
    for k in range(TOP_K):
        lo, hi = _unpack_bf16_pairs(y2d[k])
        acc_lo = w[:, k:k + 1] * lo if k == 0 else acc_lo + w[:, k:k + 1] * lo
        acc_hi = w[:, k:k + 1] * hi if k == 0 else acc_hi + w[:, k:k + 1] * hi
    x = x_ref[...] + jnp.concatenate([acc_lo, acc_hi], axis=-1)
    o_ref[...] = _rms(x, nf_ref[...]) if final else x


def _combine(x, ys3, wts, norm_final, *, tc, final, row0, n_rows):
    n = x.shape[0]
    b0 = row0 // tc
    kb = n // tc
    row = lambda j: (b0 + j, 0)
    y_spec = lambda k: pl.BlockSpec((tc, 1, HALF), lambda j: (k * kb + b0 + j, 0, 0))
    return pl.pallas_call(
        functools.partial(_combine_kernel, tc=tc, final=final),
        grid=(n_rows // tc,),
        in_specs=[
            pl.BlockSpec((tc, D_MODEL), row),
            y_spec(0), y_spec(1), y_spec(2), y_spec(3),
            pl.BlockSpec((tc, LANES), row),
            pl.BlockSpec((1, D_MODEL), lambda j: (0, 0)),
        ],
        out_specs=pl.BlockSpec((tc, D_MODEL), lambda j: (j, 0)),
        out_shape=jax.ShapeDtypeStruct((n_rows, D_MODEL), F32),
        scratch_shapes=[pltpu.VMEM((TOP_K, tc, HALF), jnp.uint32)],
        compiler_params=pltpu.CompilerParams(
            dimension_semantics=("arbitrary",), vmem_limit_bytes=VMEM_LIMIT),
        name="combine_final" if final else "combine",
    )(x, ys3, ys3, ys3, ys3, wts, norm_final)


def _seq_tables(groups, ts):
    pos, length = [], []
    for n_seq, seq_len in groups:
        assert seq_len % ts == 0
        for _ in range(n_seq):
            for t in range(seq_len // ts):
                pos.append(t * ts)
                length.append(seq_len)
    return np.asarray(pos, np.int32), np.asarray(length, np.int32)


def _routing_plan(meta, counts, *, n, tm, nt):
    idx = meta[0:TOP_K]
    pos = meta[TOP_K:2 * TOP_K]
    cnt = counts[0, :N_EXPERTS].astype(jnp.int32)
    tiles_e = (cnt + tm - 1) // tm
    tile_end = jnp.cumsum(tiles_e)
    base = (tile_end - tiles_e) * tm
    base_of = jnp.zeros_like(idx)
    for e in range(N_EXPERTS):
        base_of = jnp.where(idx == e, base[e], base_of)
    slot = (base_of + pos).reshape(-1)
    tile_ids = jnp.arange(nt, dtype=jnp.int32)
    tile_expert = jnp.minimum(jnp.sum(tile_end[None, :] <= tile_ids[:, None], axis=1), N_EXPERTS - 1).astype(jnp.int32)
    pairs_by_slot = jnp.argsort(slot).astype(jnp.int32)
    first = tile_end - tiles_e
    before = jnp.cumsum(cnt) - cnt
    row0 = (tile_ids - jnp.take(first, tile_expert)) * tm
    in_use = tile_ids < tile_end[-1]
    n_valid = jnp.where(in_use, jnp.clip(jnp.take(cnt, tile_expert) - row0, 0, tm), 0)
    start = jnp.where(in_use, jnp.take(before, tile_expert) + row0, 0)
    padded = jnp.concatenate([pairs_by_slot, jnp.zeros((tm,), jnp.int32)])
    win = jax.vmap(lambda st: lax.dynamic_slice(padded, (st,), (tm,)))(start)
    r = jnp.arange(tm, dtype=jnp.int32)[None, :]
    valid = r < n_valid[:, None]
    spare = TOP_K * n + (tile_ids[:, None] % 2) * tm + r
    tok = jnp.where(valid, win % n, 0)
    dst_tiles = jnp.where(valid, win, spare)
    fake = TOP_K * n + jnp.arange(2, dtype=jnp.int32)[:, None] * tm + r
    dst = jnp.concatenate([fake, dst_tiles[:nt - 2]], axis=0)
    return tile_expert, tile_end[-1:], tok.reshape(nt, 1, tm), dst.reshape(nt, 1, tm)


def _layer_params(l, norm_mix, w_in, pool_w, pool_scale, conv_w, conv_b, lru_wa, lru_ba, lru_wx, lru_bx,
                  lru_lambda, out_norm_pool, out_norm_lru, w_out, norm_ffn, router_w, router_b,
                  w_up, b_up, w_down, b_down):
    def gates(d):
        return dict(
            wg=jnp.concatenate([lru_wa[l, d], lru_wx[l, d]], axis=-1).astype(BF16),
            ba=lru_ba[l, d].reshape(1, D_LRU),
            bx=lru_bx[l, d].reshape(1, D_LRU),
            sp=jax.nn.softplus(-lru_lambda[l, d]).reshape(1, D_LRU),
        )
    gf, gb = gates(0), gates(1)
    pad = LANES - N_EXPERTS
    rw = jnp.pad(router_w[l], ((0, 0), (0, pad)))
    rw_hi = rw.astype(BF16)
    return dict(
        norm_mix=norm_mix[l].reshape(1, D_MODEL),
        w_in=w_in[l].astype(BF16),
        pool_w=pool_w[l].astype(BF16),
        pool_scale=pool_scale[l].reshape(1, D_POOL),
        conv_w=conv_w[l],
        conv_b=conv_b[l].reshape(1, D_LRU),
        wg_f=gf["wg"], ba_f=gf["ba"], bx_f=gf["bx"], sp_f=gf["sp"],
        wg_b=gb["wg"], ba_b=gb["ba"], bx_b=gb["bx"], sp_b=gb["sp"],
        out_norm_pool=out_norm_pool[l].reshape(1, D_POOL),
        out_norm_lru=out_norm_lru[l].reshape(1, D_LRU),
        w_out=w_out[l].astype(BF16),
        norm_ffn=norm_ffn[l].reshape(1, D_MODEL),
        router_w_hi=rw_hi,
        router_w_lo=(rw - rw_hi.astype(F32)).astype(BF16),
        router_b=jnp.pad(router_b[l].reshape(1, N_EXPERTS), ((0, 0), (0, pad)), constant_values=-jnp.inf),
    )


def _trunk(xs, weights, norm_final, *, ts, tr, tm, tc):
    groups = [(x.shape[0], x.shape[1]) for x in xs]
    sizes = [b * s for b, s in groups]
    n = sum(sizes)
    depth = weights[0].shape[0]
    srcs = [xi.reshape(-1, D_MODEL) for xi in xs]
    tile_pos, tile_len = _seq_tables(groups, ts)
    tile_pos, tile_len = jnp.asarray(tile_pos), jnp.asarray(tile_len)
    nt = (TOP_K * n) // tm + N_EXPERTS + 2
    n_out_rows = TOP_K * n + 2 * tm
    nf = norm_final.reshape(1, D_MODEL)
    w_up, b_up, w_down, b_down = weights[-4:]
    b_up3 = b_up.reshape(depth, N_EXPERTS, 1, 2 * D_FF)
    b_down3 = b_down.reshape(depth, N_EXPERTS, 1, D_MODEL)
    outs = None
    for l in range(depth):
        p = _layer_params(l, *weights)
        ypn, hf, gate, xc = _mix_a(srcs, tile_pos, tile_len, p, ts=ts)
        x = _mix_b(srcs, xc, hf, gate, ypn, tile_pos, tile_len, p, ts=ts)
        hf3, meta, wts, counts = _router(x, p, tr=tr)
        tile_expert, used, tok3, dst3 = _routing_plan(meta, counts, n=n, tm=tm, nt=nt)
        ys3 = _moe(hf3, tile_expert, used, tok3, dst3, w_up, b_up3, w_down, b_down3,
                   layer=l, tm=tm, n_out_rows=n_out_rows)
        if l + 1 < depth:
            x = _combine(x, ys3, wts, nf, tc=tc, final=False, row0=0, n_rows=n)
            srcs = [x]
        else:
            outs, row0 = [], 0
            for (b, s), sz in zip(groups, sizes):
                y = _combine(x, ys3, wts, nf, tc=tc, final=True, row0=row0, n_rows=sz)
                outs.append(y.reshape(b, s, D_MODEL))
                row0 += sz
    return outs


def kernel(x_prompt, x_sample, norm_mix, w_in, pool_w, pool_scale, conv_w, conv_b, lru_wa, lru_ba, lru_wx, lru_bx,
           lru_lambda, out_norm_pool, out_norm_lru, w_out, norm_ffn, router_w, router_b, w_up, b_up, w_down,
           b_down, norm_final):
    weights = (norm_mix, w_in, pool_w, pool_scale, conv_w, conv_b, lru_wa, lru_ba, lru_wx, lru_bx, lru_lambda,
               out_norm_pool, out_norm_lru, w_out, norm_ffn, router_w, router_b, w_up, b_up, w_down, b_down)
    y_prompt, y_sample = _trunk([x_prompt, x_sample], weights, norm_final, ts=1024, tr=512, tm=512, tc=512)
    return (y_prompt, y_sample)
```

```python
import functools

import numpy as np
import jax
import jax.numpy as jnp
from jax import lax
from jax.experimental import pallas as pl
from jax.experimental.pallas import tpu as pltpu

D_MODEL = 1024
D_POOL = 512
D_LRU = 512
D_IN = D_POOL + 2 * D_LRU
POOL_WINDOWS = (2, 4, 8, 16)
POOL_GROUP_W = D_POOL // len(POOL_WINDOWS)
N_LRU_HEADS = 4
LRU_BLOCK = D_LRU // N_LRU_HEADS
LRU_C = 8.0
CONV_W = 4
CONV_LEFT = 1
N_EXPERTS = 32
TOP_K = 4
D_FF = D_MODEL
SWIGLU_LIMIT = 7.0
SWIGLU_ALPHA = 1.702
EPS = 1e-6

LANES = 128
SUBLANES = 8
HALO = 16
VMEM_LIMIT = 56 * 1024 * 1024
K_CHUNKS = 2
HALF = D_MODEL // 2
SCAN_UNROLL = 4
OUT_SLOTS = 3

F32 = jnp.float32
BF16 = jnp.bfloat16


def _rms(x, g):
    return x * lax.rsqrt(jnp.mean(x * x, axis=-1, keepdims=True) + EPS) * g


def _pack_bf16_pairs(lo, hi):
    lo_bits = lax.bitcast_convert_type(lo.astype(BF16).astype(F32), jnp.uint32)
    hi_bits = lax.bitcast_convert_type(hi.astype(BF16).astype(F32), jnp.uint32)
    return lax.shift_right_logical(lo_bits, jnp.uint32(16)) | hi_bits


def _unpack_bf16_pairs(w):
    lo = lax.bitcast_convert_type(lax.shift_left(w, jnp.uint32(16)), F32)
    hi = lax.bitcast_convert_type(w & jnp.uint32(0xFFFF0000), F32)
    return lo, hi


def _shift_rows(v, s):
    n = v.shape[0]
    return pltpu.roll(v, s % n, axis=0)


def _lru_gates(xc, wg_ref, ba_ref, bx_ref, sp_ref, a_scr, b_scr):
    for h in range(N_LRU_HEADS):
        cs = slice(h * LRU_BLOCK, (h + 1) * LRU_BLOCK)
        xh = xc[:, cs]
        gr = jnp.dot(xh.astype(BF16), wg_ref[h], preferred_element_type=F32)
        r = jax.nn.sigmoid(gr[:, :LRU_BLOCK] + ba_ref[:, cs])
        i = jax.nn.sigmoid(gr[:, LRU_BLOCK:] + bx_ref[:, cs])
        log_a = -LRU_C * r * sp_ref[:, cs]
        a = jnp.exp(log_a)
        beta = jnp.sqrt(1.0 - a * a)
        a_scr[:, cs] = a
        b_scr[:, cs] = beta * i * xh


def _scan_rows(a_ref, b_ref, h_ref, carry0, n_rows, reverse):
    row = lax.broadcasted_iota(jnp.int32, (SUBLANES, D_LRU), 0)
    n_grp = n_rows // SUBLANES
    n_iter = n_grp // SCAN_UNROLL

    def prefix(r0):
        a = a_ref[pl.ds(r0, SUBLANES), :]
        b = b_ref[pl.ds(r0, SUBLANES), :]
        for s in (1, 2, 4):
            if reverse:
                ok = row < SUBLANES - s
                a_sh = jnp.where(ok, pltpu.roll(a, SUBLANES - s, axis=0), 1.0)
                b_sh = jnp.where(ok, pltpu.roll(b, SUBLANES - s, axis=0), 0.0)
            else:
                ok = row >= s
                a_sh = jnp.where(ok, pltpu.roll(a, s, axis=0), 1.0)
                b_sh = jnp.where(ok, pltpu.roll(b, s, axis=0), 0.0)
            b = a * b_sh + b
            a = a * a_sh
        return a, b

    def body(k, carry):
        first = (n_iter - 1 - k) if reverse else k
        order = range(SCAN_UNROLL - 1, -1, -1) if reverse else range(SCAN_UNROLL)
        starts = [pl.multiple_of((first * SCAN_UNROLL + u) * SUBLANES, SUBLANES) for u in order]
        parts = [prefix(r0) for r0 in starts]
        for r0, (a, b) in zip(starts, parts):
            h = a * carry + b
            h_ref[pl.ds(r0, SUBLANES), :] = h
            carry = h[0:1, :] if reverse else h[SUBLANES - 1:SUBLANES, :]
        return carry

    return lax.fori_loop(0, n_iter, body, carry0)


def _pick_source(g, starts, refs):
    v = refs[0][...]
    for start, ref in zip(starts[1:], refs[1:]):
        v = jnp.where(g >= start, ref[...], v)
    return v


def _mix_a_kernel(pos_ref, len_ref, *refs, ts, starts):
    n_src = len(starts)
    x_refs, xp_refs, xn_refs = refs[0:n_src], refs[n_src:2 * n_src], refs[2 * n_src:3 * n_src]
    (nm_ref, win_ref, pw_ref, ps_ref, cw_ref, cb_ref, wg_ref, ba_ref, bx_ref, sp_ref, onp_ref,
     ypn_ref, hf_ref, gate_ref, xc_ref, h_scr, a_scr, b_scr, carry_ref) = refs[3 * n_src:]
    g = pl.program_id(0)
    pos0 = pos_ref[g]
    slen = len_ref[g]
    keep_prev = jnp.where(pos0 == 0, 0.0, 1.0).astype(F32)
    keep_next = jnp.where(pos0 + ts == slen, 0.0, 1.0).astype(F32)
    nm = nm_ref[...]

    h_scr[0:HALO, :] = (_rms(_pick_source(g, starts, xp_refs), nm) * keep_prev).astype(BF16)
    h_scr[HALO:HALO + ts, :] = _rms(_pick_source(g, starts, x_refs), nm).astype(BF16)
    h_scr[HALO + ts:, :] = (_rms(_pick_source(g, starts, xn_refs), nm) * keep_next).astype(BF16)
    z = jnp.dot(h_scr[...], win_ref[...], preferred_element_type=F32)

    trow = pos0 + lax.broadcasted_iota(jnp.int32, (ts, POOL_GROUP_W), 0)
    ys = []
    for gi, win in enumerate(POOL_WINDOWS):
        u = z[:, gi * POOL_GROUP_W:(gi + 1) * POOL_GROUP_W]
        acc = u + _shift_rows(u, 1)
        half = 1
        while 2 * half < win:
            acc = _shift_rows(acc, half) + _shift_rows(acc, -half)
            half *= 2
        half = win // 2
        cnt = (jnp.minimum(trow + half, slen) - jnp.maximum(trow - half, 0)).astype(F32)
        p = acc[HALO:HALO + ts] / cnt - u[HALO:HALO + ts]
        ys.append(jnp.dot(p.astype(BF16), pw_ref[gi], preferred_element_type=F32))
    y_pool = jnp.concatenate(ys, axis=-1) * ps_ref[...]
    ypn_ref[...] = _rms(y_pool, onp_ref[...]).astype(BF16)

    ul = z[:, D_POOL:D_POOL + D_LRU]
    xc_ext = cb_ref[...] + _shift_rows(ul, CONV_LEFT) * cw_ref[0:1, :]
    xc_ext = xc_ext + ul * cw_ref[1:2, :]
    xc_ext = xc_ext + _shift_rows(ul, -1) * cw_ref[2:3, :]
    xc_ext = xc_ext + _shift_rows(ul, -2) * cw_ref[3:4, :]
    xc = xc_ext[HALO:HALO + ts]
    xc_ref[...] = xc

    gate_ref[...] = jax.nn.gelu(z[HALO:HALO + ts, D_POOL + D_LRU:], approximate=True)

    _lru_gates(xc, wg_ref, ba_ref, bx_ref, sp_ref, a_scr, b_scr)
    carry0 = jnp.where(pos0 == 0, 0.0, carry_ref[...])
    carry_ref[...] = _scan_rows(a_scr, b_scr, hf_ref, carry0, ts, reverse=False)


def _source_starts(xs, ts):
    starts, at = [], 0
    for x in xs:
        starts.append(at)
        at += x.shape[0] // ts
    return tuple(starts), at


def _mix_a(xs, tile_pos, tile_len, p, *, ts):
    starts, g_tiles = _source_starts(xs, ts)
    n = g_tiles * ts
    hb = ts // HALO
    row = lambda g, *_: (g, 0)
    const2 = lambda g, *_: (0, 0)
    const3 = lambda g, *_: (0, 0, 0)

    def cur(start, x):
        return pl.BlockSpec((ts, D_MODEL), lambda g, *_: (jnp.clip(g - start, 0, x.shape[0] // ts - 1), 0))

    def prev(start, x):
        return pl.BlockSpec((HALO, D_MODEL),
                            lambda g, *_: (jnp.clip((g - start) * hb - 1, 0, x.shape[0] // HALO - 1), 0))

    def nxt(start, x):
        return pl.BlockSpec((HALO, D_MODEL),
                            lambda g, *_: (jnp.clip((g - start + 1) * hb, 0, x.shape[0] // HALO - 1), 0))

    grid_spec = pltpu.PrefetchScalarGridSpec(
        num_scalar_prefetch=2,
        grid=(g_tiles,),
        in_specs=[
            *[cur(st, x) for st, x in zip(starts, xs)],
            *[prev(st, x) for st, x in zip(starts, xs)],
            *[nxt(st, x) for st, x in zip(starts, xs)],
            pl.BlockSpec((1, D_MODEL), const2),
            pl.BlockSpec((D_MODEL, D_IN), const2),
            pl.BlockSpec((len(POOL_WINDOWS), POOL_GROUP_W, POOL_GROUP_W), const3),
            pl.BlockSpec((1, D_POOL), const2),
            pl.BlockSpec((CONV_W, D_LRU), const2),
            pl.BlockSpec((1, D_LRU), const2),
            pl.BlockSpec((N_LRU_HEADS, LRU_BLOCK, 2 * LRU_BLOCK), const3),
            pl.BlockSpec((1, D_LRU), const2),
            pl.BlockSpec((1, D_LRU), const2),
            pl.BlockSpec((1, D_LRU), const2),
            pl.BlockSpec((1, D_POOL), const2),
        ],
        out_specs=[
            pl.BlockSpec((ts, D_POOL), row),
            pl.BlockSpec((ts, D_LRU), row),
            pl.BlockSpec((ts, D_LRU), row),
            pl.BlockSpec((ts, D_LRU), row),
        ],
        scratch_shapes=[
            pltpu.VMEM((ts + 2 * HALO, D_MODEL), BF16),
            pltpu.VMEM((ts, D_LRU), F32),
            pltpu.VMEM((ts, D_LRU), F32),
            pltpu.VMEM((1, D_LRU), F32),
        ],
    )
    return pl.pallas_call(
        functools.partial(_mix_a_kernel, ts=ts, starts=starts),
        grid_spec=grid_spec,
        out_shape=[
            jax.ShapeDtypeStruct((n, D_POOL), BF16),
            jax.ShapeDtypeStruct((n, D_LRU), F32),
            jax.ShapeDtypeStruct((n, D_LRU), F32),
            jax.ShapeDtypeStruct((n, D_LRU), F32),
        ],
        compiler_params=pltpu.CompilerParams(
            dimension_semantics=("arbitrary",), vmem_limit_bytes=VMEM_LIMIT),
        name="mix_a",
    )(tile_pos, tile_len, *xs, *xs, *xs, p["norm_mix"], p["w_in"], p["pool_w"], p["pool_scale"],
      p["conv_w"], p["conv_b"], p["wg_f"], p["ba_f"], p["bx_f"], p["sp_f"], p["out_norm_pool"])


def _mix_b_kernel(pos_ref, len_ref, *refs, ts, g_tiles, starts):
    n_src = len(starts)
    x_refs = refs[0:n_src]
    (xc_ref, hf_ref, gate_ref, ypn_ref, wg_ref, ba_ref, bx_ref, sp_ref, onl_ref, wout_ref,
     o_ref, a_scr, b_scr, hb_scr, carry_ref) = refs[n_src:]
    g = g_tiles - 1 - pl.program_id(0)
    at_end = pos_ref[g] + ts == len_ref[g]
    _lru_gates(xc_ref[...], wg_ref, ba_ref, bx_ref, sp_ref, a_scr, b_scr)
    carry0 = jnp.where(at_end, 0.0, carry_ref[...])
    carry_ref[...] = _scan_rows(a_scr, b_scr, hb_scr, carry0, ts, reverse=True)
    y_lru = (hf_ref[...] + hb_scr[...]) * gate_ref[...]
    yln = _rms(y_lru, onl_ref[...]).astype(BF16)
    out = jnp.dot(ypn_ref[...], wout_ref[0:D_POOL, :], preferred_element_type=F32)
    out = out + jnp.dot(yln, wout_ref[D_POOL:, :], preferred_element_type=F32)
    o_ref[...] = _pick_source(g, starts, x_refs) + out


def _mix_b(xs, xc, hf, gate, ypn, tile_pos, tile_len, p, *, ts):
    starts, g_tiles = _source_starts(xs, ts)
    n = g_tiles * ts
    row = lambda g, *_: (g_tiles - 1 - g, 0)

    def cur(start, x):
        return pl.BlockSpec((ts, D_MODEL),
                            lambda g, *_: (jnp.clip(g_tiles - 1 - g - start, 0, x.shape[0] // ts - 1), 0))

    const2 = lambda g, *_: (0, 0)
    const3 = lambda g, *_: (0, 0, 0)
    grid_spec = pltpu.PrefetchScalarGridSpec(
        num_scalar_prefetch=2,
        grid=(g_tiles,),
        in_specs=[
            *[cur(st, x) for st, x in zip(starts, xs)],
            pl.BlockSpec((ts, D_LRU), row),
            pl.BlockSpec((ts, D_LRU), row),
            pl.BlockSpec((ts, D_LRU), row),
            pl.BlockSpec((ts, D_POOL), row),
            pl.BlockSpec((N_LRU_HEADS, LRU_BLOCK, 2 * LRU_BLOCK), const3),
            pl.BlockSpec((1, D_LRU), const2),
            pl.BlockSpec((1, D_LRU), const2),
            pl.BlockSpec((1, D_LRU), const2),
            pl.BlockSpec((1, D_LRU), const2),
            pl.BlockSpec((D_POOL + D_LRU, D_MODEL), const2),
        ],
        out_specs=pl.BlockSpec((ts, D_MODEL), row),
        scratch_shapes=[
            pltpu.VMEM((ts, D_LRU), F32),
            pltpu.VMEM((ts, D_LRU), F32),
            pltpu.VMEM((ts, D_LRU), F32),
            pltpu.VMEM((1, D_LRU), F32),
        ],
    )
    return pl.pallas_call(
        functools.partial(_mix_b_kernel, ts=ts, g_tiles=g_tiles, starts=starts),
        grid_spec=grid_spec,
        out_shape=jax.ShapeDtypeStruct((n, D_MODEL), F32),
        compiler_params=pltpu.CompilerParams(
            dimension_semantics=("arbitrary",), vmem_limit_bytes=VMEM_LIMIT),
        name="mix_b",
    )(tile_pos, tile_len, *xs, xc, hf, gate, ypn, p["wg_b"], p["ba_b"], p["bx_b"], p["sp_b"],
      p["out_norm_lru"], p["w_out"])


def _router_kernel(x_ref, nf_ref, rwh_ref, rwl_ref, rb_ref,
                   hf_ref, meta_ref, wts_ref, cnt_ref,
                   carry_ref, *, tr):
    @pl.when(pl.program_id(0) == 0)
    def _():
        carry_ref[...] = jnp.zeros_like(carry_ref)

    hf = _rms(x_ref[...], nf_ref[...])
    hf_ref[...] = hf.reshape(tr, 1, D_MODEL)
    h_hi = hf.astype(BF16)
    h_lo = (hf - h_hi.astype(F32)).astype(BF16)
    logits = (jnp.dot(h_hi, rwh_ref[...], preferred_element_type=F32)
              + jnp.dot(h_lo, rwh_ref[...], preferred_element_type=F32)
              + jnp.dot(h_hi, rwl_ref[...], preferred_element_type=F32)) + rb_ref[...]
    lane = lax.broadcasted_iota(jnp.int32, (tr, LANES), 1).astype(F32)

    vals, idxs = [], []
    l = logits
    for _ in range(TOP_K):
        m = jnp.max(l, axis=-1, keepdims=True)
        ik = jnp.min(jnp.where(l == m, lane, float(LANES)), axis=-1, keepdims=True)
        vals.append(m)
        idxs.append(ik)
        l = jnp.where(lane == ik, -jnp.inf, l)
    es = [jnp.exp(v - vals[0]) for v in vals]
    den = es[0] + es[1] + es[2] + es[3]

    hits = [lane == ik for ik in idxs]
    chosen = jnp.zeros((tr, LANES), F32)
    for hit in hits:
        chosen = chosen + jnp.where(hit, 1.0, 0.0)
    tri = jnp.where(lax.broadcasted_iota(jnp.int32, (tr, tr), 1) < lax.broadcasted_iota(jnp.int32, (tr, tr), 0),
                    1.0, 0.0).astype(BF16)
    before = jnp.dot(tri, chosen.astype(BF16), preferred_element_type=F32) + carry_ref[...]
    total = carry_ref[...] + jnp.sum(chosen, axis=0, keepdims=True)
    carry_ref[...] = total
    cnt_ref[...] = total

    meta = jnp.zeros((tr, LANES), F32)
    wts = jnp.zeros((tr, LANES), F32)
    for k in range(TOP_K):
        pos_k = jnp.sum(jnp.where(hits[k], before, 0.0), axis=-1, keepdims=True)
        meta = jnp.where(lane == float(k), idxs[k], meta)
        meta = jnp.where(lane == float(TOP_K + k), pos_k, meta)
        wts = jnp.where(lane == float(k), es[k] / den, wts)
    meta_ref[...] = jnp.transpose(meta)[0:2 * TOP_K, :].astype(jnp.int32)
    wts_ref[...] = wts


def _router(x, p, *, tr):
    n = x.shape[0]
    row = lambda j: (j, 0)
    const2 = lambda j: (0, 0)
    return pl.pallas_call(
        functools.partial(_router_kernel, tr=tr),
        grid=(n // tr,),
        in_specs=[
            pl.BlockSpec((tr, D_MODEL), row),
            pl.BlockSpec((1, D_MODEL), const2),
            pl.BlockSpec((D_MODEL, LANES), const2),
            pl.BlockSpec((D_MODEL, LANES), const2),
            pl.BlockSpec((1, LANES), const2),
        ],
        out_specs=[
            pl.BlockSpec((tr, 1, D_MODEL), lambda j: (j, 0, 0)),
            pl.BlockSpec((2 * TOP_K, tr), lambda j: (0, j)),
            pl.BlockSpec((tr, LANES), row),
            pl.BlockSpec((1, LANES), const2),
        ],
        out_shape=[
            jax.ShapeDtypeStruct((n, 1, D_MODEL), F32),
            jax.ShapeDtypeStruct((2 * TOP_K, n), jnp.int32),
            jax.ShapeDtypeStruct((n, LANES), F32),
            jax.ShapeDtypeStruct((1, LANES), F32),
        ],
        scratch_shapes=[pltpu.VMEM((1, LANES), F32)],
        compiler_params=pltpu.CompilerParams(
            dimension_semantics=("arbitrary",), vmem_limit_bytes=VMEM_LIMIT),
        name="router",
    )(x, p["norm_ffn"], p["router_w_hi"], p["router_w_lo"], p["router_b"])


def _moe_kernel(te_ref, used_ref,
                tok0_ref, tokn_ref, dst_ref, hf_hbm, wup_ref, bup_ref, wdn_ref, bdn_ref,
                ys_hbm,
                xbuf, obuf, x2d, o2d, wup_bf, wdn_bf, gsem, ssem, *, tm, nt):
    s = pl.program_id(0)
    slot = lax.rem(s, 2)
    other = 1 - slot
    o_fill = lax.rem(s + (OUT_SLOTS - 1), OUT_SLOTS)
    o_send = lax.rem(s + (OUT_SLOTS - 2), OUT_SLOTS)

    def gather_wait(b):
        pltpu.make_async_copy(hf_hbm.at[pl.ds(0, tm)], xbuf.at[pl.ds(b * tm, tm)], gsem.at[b]).wait()

    def scatter_wait(b):
        pltpu.make_async_copy(obuf.at[pl.ds(b * tm, tm)], ys_hbm.at[pl.ds(0, tm)], ssem.at[b]).wait()

    def gather_row(tok, b, r, prio=0):
        pltpu.make_async_copy(hf_hbm.at[pl.ds(tok, 1)], xbuf.at[pl.ds(b * tm + r, 1)], gsem.at[b]).start(priority=prio)

    def scatter_row(dst, b, r, prio=0):
        pltpu.make_async_copy(obuf.at[pl.ds(b * tm + r, 1)], ys_hbm.at[pl.ds(dst, 1)], ssem.at[b]).start(priority=prio)

    @pl.when(s == 0)
    def _():
        obuf[...] = jnp.zeros_like(obuf)
        o2d[...] = jnp.zeros_like(o2d)

        def body(r, c):
            gather_row(tok0_ref[0, 0, r], 0, r)
            return c
        lax.fori_loop(0, tm, body, 0)

    last = used_ref[0] + 1

    @pl.when(s <= last)
    def _():
        @pl.when(s >= OUT_SLOTS - 1)
        def _():
            scatter_wait(o_fill)

        gather_wait(slot)

        @pl.when(jnp.logical_or(s == 0, te_ref[s] != te_ref[jnp.maximum(s - 1, 0)]))
        def _():
            wup_bf[...] = wup_ref[...].astype(BF16)
            wdn_bf[...] = wdn_ref[...].astype(BF16)

        xrows = xbuf.at[pl.ds(pl.multiple_of(slot * tm, tm), tm)]
        orows = obuf.at[pl.ds(pl.multiple_of(o_fill * tm, tm), tm)]
        orows[...] = _pack_bf16_pairs(o2d[:, :HALF], o2d[:, HALF:]).reshape(tm, 1, HALF)
        kw = D_MODEL // K_CHUNKS
        bounds = [(tm * g) // (K_CHUNKS - 1) for g in range(K_CHUNKS)] + [tm]
        hh = bup_ref[...]
        for kc in range(K_CHUNKS):
            cols = slice(kc * kw, (kc + 1) * kw)
            x2d[:, cols] = xrows[:, :, cols].reshape(tm, kw)
            for r in range(bounds[kc], bounds[kc + 1]):
                gather_row(tokn_ref[0, 0, r], other, r, r % 2)
                scatter_row(dst_ref[0, 0, r], o_send, r, (r + 1) % 2)
            hh = hh + jnp.dot(x2d[:, cols].astype(BF16), wup_bf[cols, :], preferred_element_type=F32)
        x_glu = jnp.minimum(hh[:, :D_FF], SWIGLU_LIMIT)
        x_lin = jnp.clip(hh[:, D_FF:], -SWIGLU_LIMIT, SWIGLU_LIMIT)
        act = (x_glu * jax.nn.sigmoid(SWIGLU_ALPHA * x_glu) * (x_lin + 1.0)).astype(BF16)
        o2d[...] = jnp.dot(act, wdn_bf[...], preferred_element_type=F32) + bdn_ref[...]


        @pl.when(s == last)
        def _():
            scatter_wait(o_send)
            scatter_wait(lax.rem(s + (OUT_SLOTS - 3), OUT_SLOTS))
            gather_wait(other)


def _moe(hf3, tile_expert, used, tok3, dst3, w_up, b_up, w_down, b_down, *, layer, tm, n_out_rows):
    nt = tok3.shape[0]
    smem_blk = lambda fn: pl.BlockSpec((1, 1, tm), fn, memory_space=pltpu.SMEM)
    grid_spec = pltpu.PrefetchScalarGridSpec(
        num_scalar_prefetch=2,
        grid=(nt,),
        in_specs=[
            smem_blk(lambda s, te, u: (0, 0, 0)),
            smem_blk(lambda s, te, u: (jnp.minimum(s + 1, nt - 1), 0, 0)),
            smem_blk(lambda s, te, u: (s, 0, 0)),
            pl.BlockSpec(memory_space=pl.ANY),
            pl.BlockSpec((None, None, D_MODEL, 2 * D_FF), lambda s, te, u: (layer, te[s], 0, 0)),
            pl.BlockSpec((None, None, 1, 2 * D_FF), lambda s, te, u: (layer, te[s], 0, 0)),
            pl.BlockSpec((None, None, D_FF, D_MODEL), lambda s, te, u: (layer, te[s], 0, 0)),
            pl.BlockSpec((None, None, 1, D_MODEL), lambda s, te, u: (layer, te[s], 0, 0)),
        ],
        out_specs=pl.BlockSpec(memory_space=pl.ANY),
        scratch_shapes=[
            pltpu.VMEM((2 * tm, 1, D_MODEL), F32),
            pltpu.VMEM((OUT_SLOTS * tm, 1, HALF), jnp.uint32),
            pltpu.VMEM((tm, D_MODEL), F32),
            pltpu.VMEM((tm, D_MODEL), F32),
            pltpu.VMEM((D_MODEL, 2 * D_FF), BF16),
            pltpu.VMEM((D_FF, D_MODEL), BF16),
            pltpu.SemaphoreType.DMA((2,)),
            pltpu.SemaphoreType.DMA((OUT_SLOTS,)),
        ],
    )
    return pl.pallas_call(
        functools.partial(_moe_kernel, tm=tm, nt=nt),
        grid_spec=grid_spec,
        out_shape=jax.ShapeDtypeStruct((n_out_rows, 1, HALF), jnp.uint32),
        compiler_params=pltpu.CompilerParams(
            dimension_semantics=("arbitrary",), vmem_limit_bytes=VMEM_LIMIT),
        name="moe",
    )(tile_expert, used, tok3, tok3, dst3, hf3, w_up, b_up, w_down, b_down)


def _combine_kernel(x_ref, y0_ref, y1_ref, y2_ref, y3_ref, wts_ref, nf_ref, o_ref, y2d, *, tc, final):
    for k, y_ref in enumerate((y0_ref, y1_ref, y2_ref, y3_ref)):
        y2d[k] = y_ref[...].reshape(tc, HALF)
    w = wts_ref[...]
    acc_lo = acc_hi = None
    for k in range(TOP_K):
        lo, hi = _unpack_bf16_pairs(y2d[k])
        acc_lo = w[:, k:k + 1] * lo if k == 0 else acc_lo + w[:, k:k + 1] * lo
        acc_hi = w[:, k:k + 1] * hi if k == 0 else acc_hi + w[:, k:k + 1] * hi
    x = x_ref[...] + jnp.concatenate([acc_lo, acc_hi], axis=-1)
    o_ref[...] = _rms(x, nf_ref[...]) if final else x


def _combine(x, ys3, wts, norm_final, *, tc, final, row0, n_rows):
    n = x.shape[0]
    b0 = row0 // tc
    kb = n // tc
    row = lambda j: (b0 + j, 0)
    y_spec = lambda k: pl.BlockSpec((tc, 1, HALF), lambda j: (k * kb + b0 + j, 0, 0))
    return pl.pallas_call(
        functools.partial(_combine_kernel, tc=tc, final=final),
        grid=(n_rows // tc,),
        in_specs=[
            pl.BlockSpec((tc, D_MODEL), row),
            y_spec(0), y_spec(1), y_spec(2), y_spec(3),
            pl.BlockSpec((tc, LANES), row),
            pl.BlockSpec((1, D_MODEL), lambda j: (0, 0)),
        ],
        out_specs=pl.BlockSpec((tc, D_MODEL), lambda j: (j, 0)),
        out_shape=jax.ShapeDtypeStruct((n_rows, D_MODEL), F32),
        scratch_shapes=[pltpu.VMEM((TOP_K, tc, HALF), jnp.uint32)],
        compiler_params=pltpu.CompilerParams(
            dimension_semantics=("arbitrary",), vmem_limit_bytes=VMEM_LIMIT),
        name="combine_final" if final else "combine",
    )(x, ys3, ys3, ys3, ys3, wts, norm_final)


def _seq_tables(groups, ts):
    pos, length = [], []
    for n_seq, seq_len in groups:
        assert seq_len % ts == 0
        for _ in range(n_seq):
            for t in range(seq_len // ts):
                pos.append(t * ts)
                length.append(seq_len)
    return np.asarray(pos, np.int32), np.asarray(length, np.int32)


def _routing_plan(meta, counts, *, n, tm, nt):
    idx = meta[0:TOP_K]
    pos = meta[TOP_K:2 * TOP_K]
    cnt = counts[0, :N_EXPERTS].astype(jnp.int32)
    tiles_e = (cnt + tm - 1) // tm
    tile_end = jnp.cumsum(tiles_e)
    base = (tile_end - tiles_e) * tm
    base_of = jnp.zeros_like(idx)
    for e in range(N_EXPERTS):
        base_of = jnp.where(idx == e, base[e], base_of)
    slot = (base_of + pos).reshape(-1)
    tile_ids = jnp.arange(nt, dtype=jnp.int32)
    tile_expert = jnp.minimum(jnp.sum(tile_end[None, :] <= tile_ids[:, None], axis=1), N_EXPERTS - 1).astype(jnp.int32)
    first = tile_end - tiles_e
    row0 = (tile_ids - jnp.take(first, tile_expert)) * tm
    in_use = tile_ids < tile_end[-1]
    n_valid = jnp.where(in_use, jnp.clip(jnp.take(cnt, tile_expert) - row0, 0, tm), 0)
    r = jnp.arange(tm, dtype=jnp.int32)[None, :]
    all_slots = tile_ids[:, None] * tm + r
    never = jnp.int32(2 * nt * tm)
    pad_keys = jnp.where(r >= n_valid[:, None], all_slots, never).reshape(-1)
    keys = jnp.concatenate([slot, pad_keys])
    vals = jnp.concatenate([jnp.arange(TOP_K * n, dtype=jnp.int32), jnp.full((nt * tm,), -1, jnp.int32)])
    _, in_slot = lax.sort_key_val(keys, vals)
    win = in_slot[:nt * tm].reshape(nt, tm)
    valid = win >= 0
    spare = TOP_K * n + (tile_ids[:, None] % 2) * tm + r
    tok = jnp.where(valid, win % n, 0)
    dst_tiles = jnp.where(valid, win, spare)
    fake = TOP_K * n + jnp.arange(2, dtype=jnp.int32)[:, None] * tm + r
    dst = jnp.concatenate([fake, dst_tiles[:nt - 2]], axis=0)
    return tile_expert, tile_end[-1:], tok.reshape(nt, 1, tm), dst.reshape(nt, 1, tm)


def _layer_params(l, norm_mix, w_in, pool_w, pool_scale, conv_w, conv_b, lru_wa, lru_ba, lru_wx, lru_bx,
                  lru_lambda, out_norm_pool, out_norm_lru, w_out, norm_ffn, router_w, router_b,
                  w_up, b_up, w_down, b_down):
    def gates(d):
        return dict(
            wg=jnp.concatenate([lru_wa[l, d], lru_wx[l, d]], axis=-1).astype(BF16),
            ba=lru_ba[l, d].reshape(1, D_LRU),
            bx=lru_bx[l, d].reshape(1, D_LRU),
            sp=jax.nn.softplus(-lru_lambda[l, d]).reshape(1, D_LRU),
        )
    gf, gb = gates(0), gates(1)
    pad = LANES - N_EXPERTS
    rw = jnp.pad(router_w[l], ((0, 0), (0, pad)))
    rw_hi = rw.astype(BF16)
    return dict(
        norm_mix=norm_mix[l].reshape(1, D_MODEL),
        w_in=w_in[l].astype(BF16),
        pool_w=pool_w[l].astype(BF16),
        pool_scale=pool_scale[l].reshape(1, D_POOL),
        conv_w=conv_w[l],
        conv_b=conv_b[l].reshape(1, D_LRU),
        wg_f=gf["wg"], ba_f=gf["ba"], bx_f=gf["bx"], sp_f=gf["sp"],
        wg_b=gb["wg"], ba_b=gb["ba"], bx_b=gb["bx"], sp_b=gb["sp"],
        out_norm_pool=out_norm_pool[l].reshape(1, D_POOL),
        out_norm_lru=out_norm_lru[l].reshape(1, D_LRU),
        w_out=w_out[l].astype(BF16),
        norm_ffn=norm_ffn[l].reshape(1, D_MODEL),
        router_w_hi=rw_hi,
        router_w_lo=(rw - rw_hi.astype(F32)).astype(BF16),
        router_b=jnp.pad(router_b[l].reshape(1, N_EXPERTS), ((0, 0), (0, pad)), constant_values=-jnp.inf),
    )


def _trunk(xs, weights, norm_final, *, ts, tr, tm, tc):
    groups = [(x.shape[0], x.shape[1]) for x in xs]
    sizes = [b * s for b, s in groups]
    n = sum(sizes)
    depth = weights[0].shape[0]
    srcs = [xi.reshape(-1, D_MODEL) for xi in xs]
    tile_pos, tile_len = _seq_tables(groups, ts)
    tile_pos, tile_len = jnp.asarray(tile_pos), jnp.asarray(tile_len)
    nt = (TOP_K * n) // tm + N_EXPERTS + 2
    n_out_rows = TOP_K * n + 2 * tm
    nf = norm_final.reshape(1, D_MODEL)
    w_up, b_up, w_down, b_down = weights[-4:]
    b_up3 = b_up.reshape(depth, N_EXPERTS, 1, 2 * D_FF)
    b_down3 = b_down.reshape(depth, N_EXPERTS, 1, D_MODEL)
    outs = None
    for l in range(depth):
        p = _layer_params(l, *weights)
        ypn, hf, gate, xc = _mix_a(srcs, tile_pos, tile_len, p, ts=ts)
        x = _mix_b(srcs, xc, hf, gate, ypn, tile_pos, tile_len, p, ts=ts)
        hf3, meta, wts, counts = _router(x, p, tr=tr)
        tile_expert, used, tok3, dst3 = _routing_plan(meta, counts, n=n, tm=tm, nt=nt)
        ys3 = _moe(hf3, tile_expert, used, tok3, dst3, w_up, b_up3, w_down, b_down3,
                   layer=l, tm=tm, n_out_rows=n_out_rows)
        if l + 1 < depth:
            x = _combine(x, ys3, wts, nf, tc=tc, final=False, row0=0, n_rows=n)
            srcs = [x]
        else:
            outs, row0 = [], 0
            for (b, s), sz in zip(groups, sizes):
                y = _combine(x, ys3, wts, nf, tc=tc, final=True, row0=row0, n_rows=sz)
                outs.append(y.reshape(b, s, D_MODEL))
                row0 += sz
    return outs


def kernel(x_prompt, x_sample, norm_mix, w_in, pool_w, pool_scale, conv_w, conv_b, lru_wa, lru_ba, lru_wx, lru_bx,
           lru_lambda, out_norm_pool, out_norm_lru, w_out, norm_ffn, router_w, router_b, w_up, b_up, w_down,
           b_down, norm_final):
    weights = (norm_mix, w_in, pool_w, pool_scale, conv_w, conv_b, lru_wa, lru_ba, lru_wx, lru_bx, lru_lambda,
               out_norm_pool, out_norm_lru, w_out, norm_ffn, router_w, router_b, w_up, b_up, w_down, b_down)
    y_prompt, y_sample = _trunk([x_prompt, x_sample], weights, norm_final, ts=1024, tr=512, tm=512, tc=512)
    return (y_prompt, y_sample)
```

```python
import functools

import numpy as np
import jax
import jax.numpy as jnp
from jax import lax
from jax.experimental import pallas as pl
from jax.experimental.pallas import tpu as pltpu

D_MODEL = 1024
D_POOL = 512
D_LRU = 512
D_IN = D_POOL + 2 * D_LRU
POOL_WINDOWS = (2, 4, 8, 16)
POOL_GROUP_W = D_POOL // len(POOL_WINDOWS)
N_LRU_HEADS = 4
LRU_BLOCK = D_LRU // N_LRU_HEADS
LRU_C = 8.0
CONV_W = 4
CONV_LEFT = 1
N_EXPERTS = 32
TOP_K = 4
D_FF = D_MODEL
SWIGLU_LIMIT = 7.0
SWIGLU_ALPHA = 1.702
EPS = 1e-6

LANES = 128
SUBLANES = 8
HALO = 16
VMEM_LIMIT = 56 * 1024 * 1024
K_CHUNKS = 2
HALF = D_MODEL // 2
SCAN_UNROLL = 4
OUT_SLOTS = 3

F32 = jnp.float32
BF16 = jnp.bfloat16


def _rms(x, g):
    return x * lax.rsqrt(jnp.mean(x * x, axis=-1, keepdims=True) + EPS) * g


def _pack_bf16_pairs(lo, hi):
    lo_bits = lax.bitcast_convert_type(lo.astype(BF16).astype(F32), jnp.uint32)
    hi_bits = lax.bitcast_convert_type(hi.astype(BF16).astype(F32), jnp.uint32)
    return lax.shift_right_logical(lo_bits, jnp.uint32(16)) | hi_bits


def _unpack_bf16_pairs(w):
    lo = lax.bitcast_convert_type(lax.shift_left(w, jnp.uint32(16)), F32)
    hi = lax.bitcast_convert_type(w & jnp.uint32(0xFFFF0000), F32)
    return lo, hi


def _shift_rows(v, s):
    n = v.shape[0]
    return pltpu.roll(v, s % n, axis=0)


def _lru_gates(xc, wg_ref, ba_ref, bx_ref, sp_ref, a_scr, b_scr):
    for h in range(N_LRU_HEADS):
        cs = slice(h * LRU_BLOCK, (h + 1) * LRU_BLOCK)
        xh = xc[:, cs]
        gr = jnp.dot(xh.astype(BF16), wg_ref[h], preferred_element_type=F32)
        r = jax.nn.sigmoid(gr[:, :LRU_BLOCK] + ba_ref[:, cs])
        i = jax.nn.sigmoid(gr[:, LRU_BLOCK:] + bx_ref[:, cs])
        log_a = -LRU_C * r * sp_ref[:, cs]
        a = jnp.exp(log_a)
        beta = jnp.sqrt(1.0 - a * a)
        a_scr[:, cs] = a
        b_scr[:, cs] = beta * i * xh


def _scan_rows(a_ref, b_ref, h_ref, carry0, n_rows, reverse):
    row = lax.broadcasted_iota(jnp.int32, (SUBLANES, D_LRU), 0)
    n_grp = n_rows // SUBLANES
    n_iter = n_grp // SCAN_UNROLL

    def prefix(r0):
        a = a_ref[pl.ds(r0, SUBLANES), :]
        b = b_ref[pl.ds(r0, SUBLANES), :]
        for s in (1, 2, 4):
            if reverse:
                ok = row < SUBLANES - s
                a_sh = jnp.where(ok, pltpu.roll(a, SUBLANES - s, axis=0), 1.0)
                b_sh = jnp.where(ok, pltpu.roll(b, SUBLANES - s, axis=0), 0.0)
            else:
                ok = row >= s
                a_sh = jnp.where(ok, pltpu.roll(a, s, axis=0), 1.0)
                b_sh = jnp.where(ok, pltpu.roll(b, s, axis=0), 0.0)
            b = a * b_sh + b
            a = a * a_sh
        return a, b

    def body(k, carry):
        first = (n_iter - 1 - k) if reverse else k
        order = range(SCAN_UNROLL - 1, -1, -1) if reverse else range(SCAN_UNROLL)
        starts = [pl.multiple_of((first * SCAN_UNROLL + u) * SUBLANES, SUBLANES) for u in order]
        parts = [prefix(r0) for r0 in starts]
        for r0, (a, b) in zip(starts, parts):
            h = a * carry + b
            h_ref[pl.ds(r0, SUBLANES), :] = h
            carry = h[0:1, :] if reverse else h[SUBLANES - 1:SUBLANES, :]
        return carry

    return lax.fori_loop(0, n_iter, body, carry0)


def _pick_source(g, starts, refs):
    v = refs[0][...]
    for start, ref in zip(starts[1:], refs[1:]):
        v = jnp.where(g >= start, ref[...], v)
    return v


def _mix_a_kernel(pos_ref, len_ref, *refs, ts, starts):
    n_src = len(starts)
    x_refs, xp_refs, xn_refs = refs[0:n_src], refs[n_src:2 * n_src], refs[2 * n_src:3 * n_src]
    (nm_ref, win_ref, pw_ref, ps_ref, cw_ref, cb_ref, wg_ref, ba_ref, bx_ref, sp_ref, onp_ref,
     ypn_ref, hf_ref, gate_ref, xc_ref, h_scr, a_scr, b_scr, carry_ref) = refs[3 * n_src:]
    g = pl.program_id(0)
    pos0 = pos_ref[g]
    slen = len_ref[g]
    keep_prev = jnp.where(pos0 == 0, 0.0, 1.0).astype(F32)
    keep_next = jnp.where(pos0 + ts == slen, 0.0, 1.0).astype(F32)
    nm = nm_ref[...]

    h_scr[0:HALO, :] = (_rms(_pick_source(g, starts, xp_refs), nm) * keep_prev).astype(BF16)
    h_scr[HALO:HALO + ts, :] = _rms(_pick_source(g, starts, x_refs), nm).astype(BF16)
    h_scr[HALO + ts:, :] = (_rms(_pick_source(g, starts, xn_refs), nm) * keep_next).astype(BF16)
    z = jnp.dot(h_scr[...], win_ref[...], preferred_element_type=F32)

    trow = pos0 + lax.broadcasted_iota(jnp.int32, (ts, POOL_GROUP_W), 0)
    ys = []
    for gi, win in enumerate(POOL_WINDOWS):
        u = z[:, gi * POOL_GROUP_W:(gi + 1) * POOL_GROUP_W]
        acc = u + _shift_rows(u, 1)
        half = 1
        while 2 * half < win:
            acc = _shift_rows(acc, half) + _shift_rows(acc, -half)
            half *= 2
        half = win // 2
        cnt = (jnp.minimum(trow + half, slen) - jnp.maximum(trow - half, 0)).astype(F32)
        p = acc[HALO:HALO + ts] / cnt - u[HALO:HALO + ts]
        ys.append(jnp.dot(p.astype(BF16), pw_ref[gi], preferred_element_type=F32))
    y_pool = jnp.concatenate(ys, axis=-1) * ps_ref[...]
    ypn_ref[...] = _rms(y_pool, onp_ref[...]).astype(BF16)

    ul = z[:, D_POOL:D_POOL + D_LRU]
    xc_ext = cb_ref[...] + _shift_rows(ul, CONV_LEFT) * cw_ref[0:1, :]
    xc_ext = xc_ext + ul * cw_ref[1:2, :]
    xc_ext = xc_ext + _shift_rows(ul, -1) * cw_ref[2:3, :]
    xc_ext = xc_ext + _shift_rows(ul, -2) * cw_ref[3:4, :]
    xc = xc_ext[HALO:HALO + ts]
    xc_ref[...] = xc

    gate_ref[...] = jax.nn.gelu(z[HALO:HALO + ts, D_POOL + D_LRU:], approximate=True)

    _lru_gates(xc, wg_ref, ba_ref, bx_ref, sp_ref, a_scr, b_scr)
    carry0 = jnp.where(pos0 == 0, 0.0, carry_ref[...])
    carry_ref[...] = _scan_rows(a_scr, b_scr, hf_ref, carry0, ts, reverse=False)


def _source_starts(xs, ts):
    starts, at = [], 0
    for x in xs:
        starts.append(at)
        at += x.shape[0] // ts
    return tuple(starts), at


def _mix_a(xs, tile_pos, tile_len, p, *, ts):
    starts, g_tiles = _source_starts(xs, ts)
    n = g_tiles * ts
    hb = ts // HALO
    row = lambda g, *_: (g, 0)
    const2 = lambda g, *_: (0, 0)
    const3 = lambda g, *_: (0, 0, 0)

    def cur(start, x):
        return pl.BlockSpec((ts, D_MODEL), lambda g, *_: (jnp.clip(g - start, 0, x.shape[0] // ts - 1), 0))

    def prev(start, x):
        return pl.BlockSpec((HALO, D_MODEL),
                            lambda g, *_: (jnp.clip((g - start) * hb - 1, 0, x.shape[0] // HALO - 1), 0))

    def nxt(start, x):
        return pl.BlockSpec((HALO, D_MODEL),
                            lambda g, *_: (jnp.clip((g - start + 1) * hb, 0, x.shape[0] // HALO - 1), 0))

    grid_spec = pltpu.PrefetchScalarGridSpec(
        num_scalar_prefetch=2,
        grid=(g_tiles,),
        in_specs=[
            *[cur(st, x) for st, x in zip(starts, xs)],
            *[prev(st, x) for st, x in zip(starts, xs)],
            *[nxt(st, x) for st, x in zip(starts, xs)],
            pl.BlockSpec((1, D_MODEL), const2),
            pl.BlockSpec((D_MODEL, D_IN), const2),
            pl.BlockSpec((len(POOL_WINDOWS), POOL_GROUP_W, POOL_GROUP_W), const3),
            pl.BlockSpec((1, D_POOL), const2),
            pl.BlockSpec((CONV_W, D_LRU), const2),
            pl.BlockSpec((1, D_LRU), const2),
            pl.BlockSpec((N_LRU_HEADS, LRU_BLOCK, 2 * LRU_BLOCK), const3),
            pl.BlockSpec((1, D_LRU), const2),
            pl.BlockSpec((1, D_LRU), const2),
            pl.BlockSpec((1, D_LRU), const2),
            pl.BlockSpec((1, D_POOL), const2),
        ],
        out_specs=[
            pl.BlockSpec((ts, D_POOL), row),
            pl.BlockSpec((ts, D_LRU), row),
            pl.BlockSpec((ts, D_LRU), row),
            pl.BlockSpec((ts, D_LRU), row),
        ],
        scratch_shapes=[
            pltpu.VMEM((ts + 2 * HALO, D_MODEL), BF16),
            pltpu.VMEM((ts, D_LRU), F32),
            pltpu.VMEM((ts, D_LRU), F32),
            pltpu.VMEM((1, D_LRU), F32),
        ],
    )
    return pl.pallas_call(
        functools.partial(_mix_a_kernel, ts=ts, starts=starts),
        grid_spec=grid_spec,
        out_shape=[
            jax.ShapeDtypeStruct((n, D_POOL), BF16),
            jax.ShapeDtypeStruct((n, D_LRU), F32),
            jax.ShapeDtypeStruct((n, D_LRU), F32),
            jax.ShapeDtypeStruct((n, D_LRU), F32),
        ],
        compiler_params=pltpu.CompilerParams(
            dimension_semantics=("arbitrary",), vmem_limit_bytes=VMEM_LIMIT),
        name="mix_a",
    )(tile_pos, tile_len, *xs, *xs, *xs, p["norm_mix"], p["w_in"], p["pool_w"], p["pool_scale"],
      p["conv_w"], p["conv_b"], p["wg_f"], p["ba_f"], p["bx_f"], p["sp_f"], p["out_norm_pool"])


def _mix_b_kernel(pos_ref, len_ref, *refs, ts, g_tiles, starts):
    n_src = len(starts)
    x_refs = refs[0:n_src]
    (xc_ref, hf_ref, gate_ref, ypn_ref, wg_ref, ba_ref, bx_ref, sp_ref, onl_ref, wout_ref,
     o_ref, a_scr, b_scr, hb_scr, carry_ref) = refs[n_src:]
    g = g_tiles - 1 - pl.program_id(0)
    at_end = pos_ref[g] + ts == len_ref[g]
    _lru_gates(xc_ref[...], wg_ref, ba_ref, bx_ref, sp_ref, a_scr, b_scr)
    carry0 = jnp.where(at_end, 0.0, carry_ref[...])
    carry_ref[...] = _scan_rows(a_scr, b_scr, hb_scr, carry0, ts, reverse=True)
    y_lru = (hf_ref[...] + hb_scr[...]) * gate_ref[...]
    yln = _rms(y_lru, onl_ref[...]).astype(BF16)
    out = jnp.dot(ypn_ref[...], wout_ref[0:D_POOL, :], preferred_element_type=F32)
    out = out + jnp.dot(yln, wout_ref[D_POOL:, :], preferred_element_type=F32)
    o_ref[...] = _pick_source(g, starts, x_refs) + out


def _mix_b(xs, xc, hf, gate, ypn, tile_pos, tile_len, p, *, ts):
    starts, g_tiles = _source_starts(xs, ts)
    n = g_tiles * ts
    row = lambda g, *_: (g_tiles - 1 - g, 0)

    def cur(start, x):
        return pl.BlockSpec((ts, D_MODEL),
                            lambda g, *_: (jnp.clip(g_tiles - 1 - g - start, 0, x.shape[0] // ts - 1), 0))

    const2 = lambda g, *_: (0, 0)
    const3 = lambda g, *_: (0, 0, 0)
    grid_spec = pltpu.PrefetchScalarGridSpec(
        num_scalar_prefetch=2,
        grid=(g_tiles,),
        in_specs=[
            *[cur(st, x) for st, x in zip(starts, xs)],
            pl.BlockSpec((ts, D_LRU), row),
            pl.BlockSpec((ts, D_LRU), row),
            pl.BlockSpec((ts, D_LRU), row),
            pl.BlockSpec((ts, D_POOL), row),
            pl.BlockSpec((N_LRU_HEADS, LRU_BLOCK, 2 * LRU_BLOCK), const3),
            pl.BlockSpec((1, D_LRU), const2),
            pl.BlockSpec((1, D_LRU), const2),
            pl.BlockSpec((1, D_LRU), const2),
            pl.BlockSpec((1, D_LRU), const2),
            pl.BlockSpec((D_POOL + D_LRU, D_MODEL), const2),
        ],
        out_specs=pl.BlockSpec((ts, D_MODEL), row),
        scratch_shapes=[
            pltpu.VMEM((ts, D_LRU), F32),
            pltpu.VMEM((ts, D_LRU), F32),
            pltpu.VMEM((ts, D_LRU), F32),
            pltpu.VMEM((1, D_LRU), F32),
        ],
    )
    return pl.pallas_call(
        functools.partial(_mix_b_kernel, ts=ts, g_tiles=g_tiles, starts=starts),
        grid_spec=grid_spec,
        out_shape=jax.ShapeDtypeStruct((n, D_MODEL), F32),
        compiler_params=pltpu.CompilerParams(
            dimension_semantics=("arbitrary",), vmem_limit_bytes=VMEM_LIMIT),
        name="mix_b",
    )(tile_pos, tile_len, *xs, xc, hf, gate, ypn, p["wg_b"], p["ba_b"], p["bx_b"], p["sp_b"],
      p["out_norm_lru"], p["w_out"])


def _router_kernel(x_ref, nf_ref, rwh_ref, rwl_ref, rb_ref,
                   hf_ref, meta_ref, wts_ref, cnt_ref,
                   carry_ref, *, tr):
    @pl.when(pl.program_id(0) == 0)
    def _():
        carry_ref[...] = jnp.zeros_like(carry_ref)

    hf = _rms(x_ref[...], nf_ref[...])
    hf_ref[...] = hf.reshape(tr, 1, D_MODEL)
    h_hi = hf.astype(BF16)
    h_lo = (hf - h_hi.astype(F32)).astype(BF16)
    logits = (jnp.dot(h_hi, rwh_ref[...], preferred_element_type=F32)
              + jnp.dot(h_lo, rwh_ref[...], preferred_element_type=F32)
              + jnp.dot(h_hi, rwl_ref[...], preferred_element_type=F32)) + rb_ref[...]
    lane = lax.broadcasted_iota(jnp.int32, (tr, LANES), 1).astype(F32)

    vals, idxs = [], []
    l = logits
    for _ in range(TOP_K):
        m = jnp.max(l, axis=-1, keepdims=True)
        ik = jnp.min(jnp.where(l == m, lane, float(LANES)), axis=-1, keepdims=True)
        vals.append(m)
        idxs.append(ik)
        l = jnp.where(lane == ik, -jnp.inf, l)
    es = [jnp.exp(v - vals[0]) for v in vals]
    den = es[0] + es[1] + es[2] + es[3]

    hits = [lane == ik for ik in idxs]
    chosen = jnp.zeros((tr, LANES), F32)
    for hit in hits:
        chosen = chosen + jnp.where(hit, 1.0, 0.0)
    tri = jnp.where(lax.broadcasted_iota(jnp.int32, (tr, tr), 1) < lax.broadcasted_iota(jnp.int32, (tr, tr), 0),
                    1.0, 0.0).astype(BF16)
    before = jnp.dot(tri, chosen.astype(BF16), preferred_element_type=F32) + carry_ref[...]
    total = carry_ref[...] + jnp.sum(chosen, axis=0, keepdims=True)
    carry_ref[...] = total
    cnt_ref[...] = total

    meta = jnp.zeros((tr, LANES), F32)
    wts = jnp.zeros((tr, LANES), F32)
    for k in range(TOP_K):
        pos_k = jnp.sum(jnp.where(hits[k], before, 0.0), axis=-1, keepdims=True)
        meta = jnp.where(lane == float(k), idxs[k], meta)
        meta = jnp.where(lane == float(TOP_K + k), pos_k, meta)
        wts = jnp.where(lane == float(k), es[k] / den, wts)
    meta_ref[...] = jnp.transpose(meta)[0:2 * TOP_K, :].astype(jnp.int32)
    wts_ref[...] = wts


def _router(x, p, *, tr):
    n = x.shape[0]
    row = lambda j: (j, 0)
    const2 = lambda j: (0, 0)
    return pl.pallas_call(
        functools.partial(_router_kernel, tr=tr),
        grid=(n // tr,),
        in_specs=[
            pl.BlockSpec((tr, D_MODEL), row),
            pl.BlockSpec((1, D_MODEL), const2),
            pl.BlockSpec((D_MODEL, LANES), const2),
            pl.BlockSpec((D_MODEL, LANES), const2),
            pl.BlockSpec((1, LANES), const2),
        ],
        out_specs=[
            pl.BlockSpec((tr, 1, D_MODEL), lambda j: (j, 0, 0)),
            pl.BlockSpec((2 * TOP_K, tr), lambda j: (0, j)),
            pl.BlockSpec((tr, LANES), row),
            pl.BlockSpec((1, LANES), const2),
        ],
        out_shape=[
            jax.ShapeDtypeStruct((n, 1, D_MODEL), F32),
            jax.ShapeDtypeStruct((2 * TOP_K, n), jnp.int32),
            jax.ShapeDtypeStruct((n, LANES), F32),
            jax.ShapeDtypeStruct((1, LANES), F32),
        ],
        scratch_shapes=[pltpu.VMEM((1, LANES), F32)],
        compiler_params=pltpu.CompilerParams(
            dimension_semantics=("arbitrary",), vmem_limit_bytes=VMEM_LIMIT),
        name="router",
    )(x, p["norm_ffn"], p["router_w_hi"], p["router_w_lo"], p["router_b"])


def _moe_kernel(te_ref, used_ref,
                tok0_ref, tokn_ref, dst_ref, hf_hbm, wup_ref, bup_ref, wdn_ref, bdn_ref,
                ys_hbm,
                xbuf, obuf, x2d, o2d, wup_bf, wdn_bf, gsem, ssem, *, tm, nt):
    s = pl.program_id(0)
    slot = lax.rem(s, 2)
    other = 1 - slot
    o_fill = lax.rem(s + (OUT_SLOTS - 1), OUT_SLOTS)
    o_send = lax.rem(s + (OUT_SLOTS - 2), OUT_SLOTS)

    def gather_wait(b):
        pltpu.make_async_copy(hf_hbm.at[pl.ds(0, tm)], xbuf.at[pl.ds(b * tm, tm)], gsem.at[b]).wait()

    def scatter_wait(b):
        pltpu.make_async_copy(obuf.at[pl.ds(b * tm, tm)], ys_hbm.at[pl.ds(0, tm)], ssem.at[b]).wait()

    def gather_row(tok, b, r, prio=0):
        pltpu.make_async_copy(hf_hbm.at[pl.ds(tok, 1)], xbuf.at[pl.ds(b * tm + r, 1)], gsem.at[b]).start(priority=prio)

    def scatter_row(dst, b, r, prio=0):
        pltpu.make_async_copy(obuf.at[pl.ds(b * tm + r, 1)], ys_hbm.at[pl.ds(dst, 1)], ssem.at[b]).start(priority=prio)

    @pl.when(s == 0)
    def _():
        obuf[...] = jnp.zeros_like(obuf)
        o2d[...] = jnp.zeros_like(o2d)

        def body(r, c):
            gather_row(tok0_ref[0, 0, r], 0, r)
            return c
        lax.fori_loop(0, tm, body, 0)

    last = used_ref[0] + 1

    @pl.when(s <= last)
    def _():
        @pl.when(s >= OUT_SLOTS - 1)
        def _():
            scatter_wait(o_fill)

        gather_wait(slot)

        @pl.when(jnp.logical_or(s == 0, te_ref[s] != te_ref[jnp.maximum(s - 1, 0)]))
        def _():
            wup_bf[...] = wup_ref[...].astype(BF16)
            wdn_bf[...] = wdn_ref[...].astype(BF16)

        xrows = xbuf.at[pl.ds(pl.multiple_of(slot * tm, tm), tm)]
        orows = obuf.at[pl.ds(pl.multiple_of(o_fill * tm, tm), tm)]
        orows[...] = _pack_bf16_pairs(o2d[:, :HALF], o2d[:, HALF:]).reshape(tm, 1, HALF)
        kw = D_MODEL // K_CHUNKS
        bounds = [(tm * g) // (K_CHUNKS - 1) for g in range(K_CHUNKS)] + [tm]
        hh = bup_ref[...]
        for kc in range(K_CHUNKS):
            cols = slice(kc * kw, (kc + 1) * kw)
            x2d[:, cols] = xrows[:, :, cols].reshape(tm, kw)
            for r in range(bounds[kc], bounds[kc + 1]):
                gather_row(tokn_ref[0, 0, r], other, r, r % 2)
                scatter_row(dst_ref[0, 0, r], o_send, r, (r + 1) % 2)
            hh = hh + jnp.dot(x2d[:, cols].astype(BF16), wup_bf[cols, :], preferred_element_type=F32)
        x_glu = jnp.minimum(hh[:, :D_FF], SWIGLU_LIMIT)
        x_lin = jnp.clip(hh[:, D_FF:], -SWIGLU_LIMIT, SWIGLU_LIMIT)
        act = (x_glu * jax.nn.sigmoid(SWIGLU_ALPHA * x_glu) * (x_lin + 1.0)).astype(BF16)
        o2d[...] = jnp.dot(act, wdn_bf[...], preferred_element_type=F32) + bdn_ref[...]


        @pl.when(s == last)
        def _():
            scatter_wait(o_send)
            scatter_wait(lax.rem(s + (OUT_SLOTS - 3), OUT_SLOTS))
            gather_wait(other)


def _moe(hf3, tile_expert, used, tok3, dst3, w_up, b_up, w_down, b_down, *, layer, tm, n_out_rows):
    nt = tok3.shape[0]
    smem_blk = lambda fn: pl.BlockSpec((1, 1, tm), fn, memory_space=pltpu.SMEM)
    grid_spec = pltpu.PrefetchScalarGridSpec(
        num_scalar_prefetch=2,
        grid=(nt,),
        in_specs=[
            smem_blk(lambda s, te, u: (0, 0, 0)),
            smem_blk(lambda s, te, u: (jnp.minimum(s + 1, nt - 1), 0, 0)),
            smem_blk(lambda s, te, u: (s, 0, 0)),
            pl.BlockSpec(memory_space=pl.ANY),
            pl.BlockSpec((None, None, D_MODEL, 2 * D_FF), lambda s, te, u: (layer, te[s], 0, 0)),
            pl.BlockSpec((None, None, 1, 2 * D_FF), lambda s, te, u: (layer, te[s], 0, 0)),
            pl.BlockSpec((None, None, D_FF, D_MODEL), lambda s, te, u: (layer, te[s], 0, 0)),
            pl.BlockSpec((None, None, 1, D_MODEL), lambda s, te, u: (layer, te[s], 0, 0)),
        ],
        out_specs=pl.BlockSpec(memory_space=pl.ANY),
        scratch_shapes=[
            pltpu.VMEM((2 * tm, 1, D_MODEL), F32),
            pltpu.VMEM((OUT_SLOTS * tm, 1, HALF), jnp.uint32),
            pltpu.VMEM((tm, D_MODEL), F32),
            pltpu.VMEM((tm, D_MODEL), F32),
            pltpu.VMEM((D_MODEL, 2 * D_FF), BF16),
            pltpu.VMEM((D_FF, D_MODEL), BF16),
            pltpu.SemaphoreType.DMA((2,)),
            pltpu.SemaphoreType.DMA((OUT_SLOTS,)),
        ],
    )
    return pl.pallas_call(
        functools.partial(_moe_kernel, tm=tm, nt=nt),
        grid_spec=grid_spec,
        out_shape=jax.ShapeDtypeStruct((n_out_rows, 1, HALF), jnp.uint32),
        compiler_params=pltpu.CompilerParams(
            dimension_semantics=("arbitrary",), vmem_limit_bytes=VMEM_LIMIT),
        name="moe",
    )(tile_expert, used, tok3, tok3, dst3, hf3, w_up, b_up, w_down, b_down)


def _combine_kernel(x_ref, y0_ref, y1_ref, y2_ref, y3_ref, wts_ref, nf_ref, o_ref, y2d, *, tc, final):
    for k, y_ref in enumerate((y0_ref, y1_ref, y2_ref, y3_ref)):
        y2d[k] = y_ref[...].reshape(tc, HALF)
    w = wts_ref[...]
    acc_lo = acc_hi = None
    for k in range(TOP_K):
        lo, hi = _unpack_bf16_pairs(y2d[k])
        acc_lo = w[:, k:k + 1] * lo if k == 0 else acc_lo + w[:, k:k + 1] * lo
        acc_hi = w[:, k:k + 1] * hi if k == 0 else acc_hi + w[:, k:k + 1] * hi
    x = x_ref[...] + jnp.concatenate([acc_lo, acc_hi], axis=-1)
    o_ref[...] = _rms(x, nf_ref[...]) if final else x


def _combine(x, ys3, wts, norm_final, *, tc, final, row0, n_rows):
    n = x.shape[0]
    b0 = row0 // tc
    kb = n // tc
    row = lambda j: (b0 + j, 0)
    y_spec = lambda k: pl.BlockSpec((tc, 1, HALF), lambda j: (k * kb + b0 + j, 0, 0))
    return pl.pallas_call(
        functools.partial(_combine_kernel, tc=tc, final=final),
        grid=(n_rows // tc,),
        in_specs=[
            pl.BlockSpec((tc, D_MODEL), row),
            y_spec(0), y_spec(1), y_spec(2), y_spec(3),
            pl.BlockSpec((tc, LANES), row),
            pl.BlockSpec((1, D_MODEL), lambda j: (0, 0)),
        ],
        out_specs=pl.BlockSpec((tc, D_MODEL), lambda j: (j, 0)),
        out_shape=jax.ShapeDtypeStruct((n_rows, D_MODEL), F32),
        scratch_shapes=[pltpu.VMEM((TOP_K, tc, HALF), jnp.uint32)],
        compiler_params=pltpu.CompilerParams(
            dimension_semantics=("arbitrary",), vmem_limit_bytes=VMEM_LIMIT),
        name="combine_final" if final else "combine",
    )(x, ys3, ys3, ys3, ys3, wts, norm_final)


def _seq_tables(groups, ts):
    pos, length = [], []
    for n_seq, seq_len in groups:
        assert seq_len % ts == 0
        for _ in range(n_seq):
            for t in range(seq_len // ts):
                pos.append(t * ts)
                length.append(seq_len)
    return np.asarray(pos, np.int32), np.asarray(length, np.int32)


def _routing_plan(meta, counts, *, n, tm, nt):
    idx = meta[0:TOP_K]
    pos = meta[TOP_K:2 * TOP_K]
    cnt = counts[0, :N_EXPERTS].astype(jnp.int32)
    tiles_e = (cnt + tm - 1) // tm
    tile_end = jnp.cumsum(tiles_e)
    base = (tile_end - tiles_e) * tm
    base_of = jnp.zeros_like(idx)
    for e in range(N_EXPERTS):
        base_of = jnp.where(idx == e, base[e], base_of)
    slot = (base_of + pos).reshape(-1)
    tile_ids = jnp.arange(nt, dtype=jnp.int32)
    tile_expert = jnp.minimum(jnp.sum(tile_end[None, :] <= tile_ids[:, None], axis=1), N_EXPERTS - 1).astype(jnp.int32)
    first = tile_end - tiles_e
    row0 = (tile_ids - jnp.take(first, tile_expert)) * tm
    in_use = tile_ids < tile_end[-1]
    n_valid = jnp.where(in_use, jnp.clip(jnp.take(cnt, tile_expert) - row0, 0, tm), 0)
    r = jnp.arange(tm, dtype=jnp.int32)[None, :]
    never = jnp.int32(2 * nt * tm)
    last_rows = cnt - (tiles_e - 1) * tm
    tail_of_last = jnp.where((tiles_e[:, None] > 0) & (r >= last_rows[:, None]),
                             (tile_end[:, None] - 1) * tm + r, never).reshape(-1)
    past = TOP_K * n + jnp.arange(nt * tm - TOP_K * n, dtype=jnp.int32)
    past_use = jnp.where(past >= tile_end[-1] * tm, past, never)
    pad_keys = jnp.concatenate([tail_of_last, past_use])
    keys = jnp.concatenate([slot, pad_keys])
    vals = jnp.concatenate([jnp.arange(TOP_K * n, dtype=jnp.int32), jnp.full(pad_keys.shape, -1, jnp.int32)])
    _, in_slot = lax.sort_key_val(keys, vals)
    win = in_slot[:nt * tm].reshape(nt, tm)
    valid = win >= 0
    spare = TOP_K * n + (tile_ids[:, None] % 2) * tm + r
    tok = jnp.where(valid, win % n, 0)
    dst_tiles = jnp.where(valid, win, spare)
    fake = TOP_K * n + jnp.arange(2, dtype=jnp.int32)[:, None] * tm + r
    dst = jnp.concatenate([fake, dst_tiles[:nt - 2]], axis=0)
    return tile_expert, tile_end[-1:], tok.reshape(nt, 1, tm), dst.reshape(nt, 1, tm)


def _layer_params(l, norm_mix, w_in, pool_w, pool_scale, conv_w, conv_b, lru_wa, lru_ba, lru_wx, lru_bx,
                  lru_lambda, out_norm_pool, out_norm_lru, w_out, norm_ffn, router_w, router_b,
                  w_up, b_up, w_down, b_down):
    def gates(d):
        return dict(
            wg=jnp.concatenate([lru_wa[l, d], lru_wx[l, d]], axis=-1).astype(BF16),
            ba=lru_ba[l, d].reshape(1, D_LRU),
            bx=lru_bx[l, d].reshape(1, D_LRU),
            sp=jax.nn.softplus(-lru_lambda[l, d]).reshape(1, D_LRU),
        )
    gf, gb = gates(0), gates(1)
    pad = LANES - N_EXPERTS
    rw = jnp.pad(router_w[l], ((0, 0), (0, pad)))
    rw_hi = rw.astype(BF16)
    return dict(
        norm_mix=norm_mix[l].reshape(1, D_MODEL),
        w_in=w_in[l].astype(BF16),
        pool_w=pool_w[l].astype(BF16),
        pool_scale=pool_scale[l].reshape(1, D_POOL),
        conv_w=conv_w[l],
        conv_b=conv_b[l].reshape(1, D_LRU),
        wg_f=gf["wg"], ba_f=gf["ba"], bx_f=gf["bx"], sp_f=gf["sp"],
        wg_b=gb["wg"], ba_b=gb["ba"], bx_b=gb["bx"], sp_b=gb["sp"],
        out_norm_pool=out_norm_pool[l].reshape(1, D_POOL),
        out_norm_lru=out_norm_lru[l].reshape(1, D_LRU),
        w_out=w_out[l].astype(BF16),
        norm_ffn=norm_ffn[l].reshape(1, D_MODEL),
        router_w_hi=rw_hi,
        router_w_lo=(rw - rw_hi.astype(F32)).astype(BF16),
        router_b=jnp.pad(router_b[l].reshape(1, N_EXPERTS), ((0, 0), (0, pad)), constant_values=-jnp.inf),
    )


def _trunk(xs, weights, norm_final, *, ts, tr, tm, tc):
    groups = [(x.shape[0], x.shape[1]) for x in xs]
    sizes = [b * s for b, s in groups]
    n = sum(sizes)
    depth = weights[0].shape[0]
    srcs = [xi.reshape(-1, D_MODEL) for xi in xs]
    tile_pos, tile_len = _seq_tables(groups, ts)
    tile_pos, tile_len = jnp.asarray(tile_pos), jnp.asarray(tile_len)
    nt = (TOP_K * n) // tm + N_EXPERTS + 2
    n_out_rows = TOP_K * n + 2 * tm
    nf = norm_final.reshape(1, D_MODEL)
    w_up, b_up, w_down, b_down = weights[-4:]
    b_up3 = b_up.reshape(depth, N_EXPERTS, 1, 2 * D_FF)
    b_down3 = b_down.reshape(depth, N_EXPERTS, 1, D_MODEL)
    outs = None
    for l in range(depth):
        p = _layer_params(l, *weights)
        ypn, hf, gate, xc = _mix_a(srcs, tile_pos, tile_len, p, ts=ts)
        x = _mix_b(srcs, xc, hf, gate, ypn, tile_pos, tile_len, p, ts=ts)
        hf3, meta, wts, counts = _router(x, p, tr=tr)
        tile_expert, used, tok3, dst3 = _routing_plan(meta, counts, n=n, tm=tm, nt=nt)
        ys3 = _moe(hf3, tile_expert, used, tok3, dst3, w_up, b_up3, w_down, b_down3,
                   layer=l, tm=tm, n_out_rows=n_out_rows)
        if l + 1 < depth:
            x = _combine(x, ys3, wts, nf, tc=tc, final=False, row0=0, n_rows=n)
            srcs = [x]
        else:
            outs, row0 = [], 0
            for (b, s), sz in zip(groups, sizes):
                y = _combine(x, ys3, wts, nf, tc=tc, final=True, row0=row0, n_rows=sz)
                outs.append(y.reshape(b, s, D_MODEL))
                row0 += sz
    return outs


def kernel(x_prompt, x_sample, norm_mix, w_in, pool_w, pool_scale, conv_w, conv_b, lru_wa, lru_ba, lru_wx, lru_bx,
           lru_lambda, out_norm_pool, out_norm_lru, w_out, norm_ffn, router_w, router_b, w_up, b_up, w_down,
           b_down, norm_final):
    weights = (norm_mix, w_in, pool_w, pool_scale, conv_w, conv_b, lru_wa, lru_ba, lru_wx, lru_bx, lru_lambda,
               out_norm_pool, out_norm_lru, w_out, norm_ffn, router_w, router_b, w_up, b_up, w_down, b_down)
    y_prompt, y_sample = _trunk([x_prompt, x_sample], weights, norm_final, ts=1024, tr=512, tm=512, tc=512)
    return (y_prompt, y_sample)
```
